```python
import math
import jax, jax.numpy as jnp
from jax import lax
import numpy as np

D_MODEL = 1024
BATCH = 8
SEQ = 4096
DEPTH = 4

RET_HEADS = 4
RET_DK = 128
RET_DV = 128
RET_CHUNK = 128
RET_THETA = 10000.0
DIL_HEADS = 8
DIL_DH = 64
DIL_PATTERNS = ((128, 1), (512, 4), (2048, 16))
DIL_BLOCK = 128
ROPE_THETA = 500000.0
ROT_DIM = DIL_DH // 4
GLA_HEADS = 4
GLA_DK = (D_MODEL // 2) // GLA_HEADS
GLA_DV = D_MODEL // GLA_HEADS
GLA_LOWRANK = 16
GLA_TAU = 16.0
GLA_CHUNK = 64
FFN_HIDDEN = 2816
EPS = 1e-6

EVEN_SIZES = (RET_HEADS * RET_DK, RET_HEADS * RET_DK, RET_HEADS * RET_DV, RET_HEADS * RET_DV,
              DIL_HEADS * DIL_DH, DIL_HEADS * DIL_DH, DIL_HEADS * DIL_DH)
EVEN_IN = sum(EVEN_SIZES)
EVEN_MIX = RET_HEADS * RET_DV + DIL_HEADS * DIL_DH
GLA_SIZES = (GLA_HEADS * GLA_DK, GLA_HEADS * GLA_DK, GLA_HEADS * GLA_DV, GLA_HEADS * GLA_DV, GLA_LOWRANK)
GLA_IN = sum(GLA_SIZES)
GLA_MIX = GLA_HEADS * GLA_DV

kernel_name = "hybrid_retention_dilated_gla_macaron"


def _split(p, sizes):
    out, start = [], 0
    for s in sizes:
        out.append(p[..., start:start + s])
        start += s
    return out


def _rmsnorm(x, g):
    xf = x.astype(jnp.float32)
    y = xf * lax.rsqrt(jnp.mean(xf * xf, axis=-1, keepdims=True) + EPS)
    return (y * g.astype(jnp.float32)).astype(x.dtype)


def _head_norm(o):
    mu = jnp.mean(o, axis=-1, keepdims=True)
    var = jnp.mean(jnp.square(o - mu), axis=-1, keepdims=True)
    return (o - mu) * lax.rsqrt(var + EPS)


def _swiglu(h, wg, wu, wd):
    return (jax.nn.silu(h @ wg) * (h @ wu)) @ wd


def _rotary(x, inv_freq):
    rot = 2 * inv_freq.shape[0]
    pos = jnp.arange(x.shape[1], dtype=jnp.float32)
    ang = pos[:, None] * inv_freq[None, :]
    cos = jnp.cos(ang)[None, :, None, :].astype(x.dtype)
    sin = jnp.sin(ang)[None, :, None, :].astype(x.dtype)
    x1 = x[..., :rot // 2]
    x2 = x[..., rot // 2:rot]
    return jnp.concatenate([x1 * cos - x2 * sin, x2 * cos + x1 * sin, x[..., rot:]], axis=-1)


def _retention(q, k, v):
    B, S, H, Dk = q.shape
    Dv = v.shape[-1]
    C = RET_CHUNK
    N = S // C
    log_g = jnp.log(1.0 - 2.0 ** (-5.0 - jnp.arange(H, dtype=jnp.float32)))
    to_chunks = lambda t: t.reshape(B, N, C, H, t.shape[-1]).transpose(0, 3, 1, 2, 4)
    qc, kc, vc = to_chunks(q), to_chunks(k), to_chunks(v)
    i = jnp.arange(C, dtype=jnp.float32)
    rel = i[:, None] - i[None, :]
    dmask = jnp.where(rel >= 0, jnp.exp(log_g[:, None, None] * jnp.maximum(rel, 0.0)), 0.0)
    scores = jnp.einsum('bhnid,bhnjd->bhnij', qc, kc) * dmask[None, :, None]
    inner = jnp.einsum('bhnij,bhnje->bhnie', scores, vc)
    k_dec = kc * jnp.exp(log_g[:, None] * (C - 1 - i))[None, :, None, :, None]
    kv = jnp.einsum('bhnjd,bhnje->nbhde', k_dec, vc)
    chunk_decay = jnp.exp(log_g * C)[None, :, None, None]

    def step(s, kv_n):
        return s * chunk_decay + kv_n, s

    _, s_prev = lax.scan(step, jnp.zeros((B, H, Dk, Dv), jnp.float32), kv)
    q_dec = qc * jnp.exp(log_g[:, None] * (i + 1.0))[None, :, None, :, None]
    cross = jnp.einsum('bhnid,nbhde->bhnie', q_dec, s_prev)
    o = inner + cross
    return o.transpose(0, 2, 3, 1, 4).reshape(B, S, H, Dv)


def _dilated_branch(q, k, v, window, dil):
    B, S, H, Dh = q.shape
    L = S // dil
    W = window // dil
    c = DIL_BLOCK
    nb = -(-L // c)
    Lp = nb * c

    def split(t):
        return t.reshape(B, L, dil, H, Dh).transpose(0, 2, 3, 1, 4).reshape(B * dil, H, L, Dh)

    qs = jnp.pad(split(q), ((0, 0), (0, 0), (0, Lp - L), (0, 0))).reshape(B * dil, H, nb, c, Dh)

    def kblocks(t):
        t = jnp.pad(split(t), ((0, 0), (0, 0), (c, Lp - L), (0, 0))).reshape(B * dil, H, nb + 1, c, Dh)
        return jnp.concatenate([t[:, :, :-1], t[:, :, 1:]], axis=3)

    kb, vb = kblocks(k), kblocks(v)
    s = jnp.einsum('zhnqd,zhnkd->zhnqk', qs, kb).astype(jnp.float32) * (Dh ** -0.5)
    qi = jnp.arange(c)[:, None]
    kj = jnp.arange(2 * c)[None, :]
    dist = qi + c - kj
    blk = jnp.arange(nb)[:, None, None]
    valid = (dist >= 0) & (dist <= W) & (blk * c + kj - c >= 0)
    s = jnp.where(valid, s, -jnp.inf)
    m = jnp.max(s, axis=-1, keepdims=True)
    p = jnp.exp(s - m)
    l = jnp.sum(p, axis=-1, keepdims=True)
    o = jnp.einsum('zhnqk,zhnkd->zhnqd', p, vb.astype(jnp.float32)) / l
    lse = (m + jnp.log(l))[..., 0]
    o = o.reshape(B, dil, H, Lp, Dh)[:, :, :, :L].transpose(0, 3, 1, 2, 4).reshape(B, S, H, Dh)
    lse = lse.reshape(B, dil, H, Lp)[..., :L].transpose(0, 3, 1, 2).reshape(B, S, H)
    return o, lse


def _dilated_attention(q, k, v):
    outs, lses = [], []
    for window, dil in DIL_PATTERNS:
        o, lse = _dilated_branch(q, k, v, window, dil)
        outs.append(o)
        lses.append(lse)
    w = jax.nn.softmax(jnp.stack(lses, axis=0), axis=0)
    return jnp.sum(w[..., None] * jnp.stack(outs, axis=0), axis=0)


def _retention_dilated_mixer(h, w_in, w_out):
    B, S, _ = h.shape
    rq, rk, rv, rg, dq, dk, dv = _split(h @ w_in, EVEN_SIZES)
    ret_freq = RET_THETA ** (-jnp.linspace(0.0, 1.0, RET_DK // 2, dtype=jnp.float32))
    rq = _rotary(rq.reshape(B, S, RET_HEADS, RET_DK), ret_freq).astype(jnp.float32)
    rk = _rotary(rk.reshape(B, S, RET_HEADS, RET_DK), ret_freq).astype(jnp.float32) * (RET_DK ** -0.5)
    rv = rv.reshape(B, S, RET_HEADS, RET_DV).astype(jnp.float32)
    o_r = _head_norm(_retention(rq, rk, rv)).reshape(B, S, RET_HEADS * RET_DV)
    o_r = o_r * jax.nn.silu(rg.astype(jnp.float32))
    rope_freq = ROPE_THETA ** (-jnp.arange(0, ROT_DIM, 2, dtype=jnp.float32) / ROT_DIM)
    dq = _rotary(dq.reshape(B, S, DIL_HEADS, DIL_DH), rope_freq)
    dk = _rotary(dk.reshape(B, S, DIL_HEADS, DIL_DH), rope_freq)
    dv = dv.reshape(B, S, DIL_HEADS, DIL_DH)
    o_d = _dilated_attention(dq, dk, dv).reshape(B, S, DIL_HEADS * DIL_DH)
    cat = jnp.concatenate([o_r.astype(h.dtype), o_d.astype(h.dtype)], axis=-1)
    return cat @ w_out


def _gla(q, k, v, log_a):
    B, S, H, Dk = q.shape
    Dv = v.shape[-1]
    C = GLA_CHUNK
    N = S // C
    to_chunks = lambda t: t.reshape(B, N, C, H, t.shape[-1]).transpose(0, 3, 1, 2, 4)
    qc, kc, vc, lac = to_chunks(q), to_chunks(k), to_chunks(v), to_chunks(log_a)
    b = jnp.cumsum(lac, axis=3)
    b_last = b[:, :, :, -1:]
    q_t = qc * jnp.exp(b)
    k_t = kc * jnp.exp(-b)
    causal = jnp.tril(jnp.ones((C, C), dtype=bool))
    scores = jnp.where(causal, jnp.einsum('bhnid,bhnjd->bhnij', q_t, k_t), 0.0)
    inner = jnp.einsum('bhnij,bhnje->bhnie', scores, vc)
    kv = jnp.einsum('bhnjd,bhnje->nbhde', kc * jnp.exp(b_last - b), vc)
    decay = jnp.exp(b_last[:, :, :, 0]).transpose(2, 0, 1, 3)

    def step(s, inp):
        kv_n, d_n = inp
        return s * d_n[..., None] + kv_n, s

    _, s_prev = lax.scan(step, jnp.zeros((B, H, Dk, Dv), jnp.float32), (kv, decay))
    cross = jnp.einsum('bhnid,nbhde->bhnie', q_t, s_prev)
    o = inner + cross
    return o.transpose(0, 2, 3, 1, 4).reshape(B, S, H, Dv)


def _gla_mixer(h, w_in, w_a2, b_a, w_out):
    B, S, _ = h.shape
    q, k, v, r, a_low = _split(h @ w_in, GLA_SIZES)
    log_a = jax.nn.log_sigmoid((a_low @ w_a2 + b_a).astype(jnp.float32)) / GLA_TAU
    q = q.reshape(B, S, GLA_HEADS, GLA_DK).astype(jnp.float32) * (GLA_DK ** -0.5)
    k = k.reshape(B, S, GLA_HEADS, GLA_DK).astype(jnp.float32)
    v = v.reshape(B, S, GLA_HEADS, GLA_DV).astype(jnp.float32)
    log_a = log_a.reshape(B, S, GLA_HEADS, GLA_DK)
    o = _head_norm(_gla(q, k, v, log_a)).reshape(B, S, GLA_MIX)
    o = o * jax.nn.silu(r.astype(jnp.float32))
    return o.astype(h.dtype) @ w_out


def setup_inputs(seed: int = 0) -> dict:
    key = jax.random.key(seed)
    ks = jax.random.split(key, 24)
    n_even = (DEPTH + 1) // 2
    n_odd = DEPTH // 2
    f32 = jnp.float32

    def w(k, shape, fan_in):
        return jax.random.normal(k, shape, f32) * (fan_in ** -0.5)

    def gain(k, shape):
        return 1.0 + 0.05 * jax.random.normal(k, shape, f32)

    return {
        "x": jax.random.normal(ks[0], (BATCH, SEQ, D_MODEL), f32),
        "ffn_pre_norm": gain(ks[1], (DEPTH, D_MODEL)),
        "ffn_pre_w_gate": w(ks[2], (DEPTH, D_MODEL, FFN_HIDDEN), D_MODEL),
        "ffn_pre_w_up": w(ks[3], (DEPTH, D_MODEL, FFN_HIDDEN), D_MODEL),
        "ffn_pre_w_down": w(ks[4], (DEPTH, FFN_HIDDEN, D_MODEL), FFN_HIDDEN),
        "mix_norm": gain(ks[5], (DEPTH, D_MODEL)),
        "ab_w_in": w(ks[6], (n_even, D_MODEL, EVEN_IN), D_MODEL),
        "ab_w_out": w(ks[7], (n_even, EVEN_MIX, D_MODEL), EVEN_MIX),
        "gla_w_in": w(ks[8], (n_odd, D_MODEL, GLA_IN), D_MODEL),
        "gla_w_a2": w(ks[9], (n_odd, GLA_LOWRANK, GLA_HEADS * GLA_DK), GLA_LOWRANK),
        "gla_b_a": 0.1 * jax.random.normal(ks[10], (n_odd, GLA_HEADS * GLA_DK), f32),
        "gla_w_out": w(ks[11], (n_odd, GLA_MIX, D_MODEL), GLA_MIX),
        "ffn_post_norm": gain(ks[12], (DEPTH, D_MODEL)),
        "ffn_post_w_gate": w(ks[13], (DEPTH, D_MODEL, FFN_HIDDEN), D_MODEL),
        "ffn_post_w_up": w(ks[14], (DEPTH, D_MODEL, FFN_HIDDEN), D_MODEL),
        "ffn_post_w_down": w(ks[15], (DEPTH, FFN_HIDDEN, D_MODEL), FFN_HIDDEN),
        "final_norm": gain(ks[16], (D_MODEL,)),
    }


def reference(x, ffn_pre_norm, ffn_pre_w_gate, ffn_pre_w_up, ffn_pre_w_down, mix_norm,
              ab_w_in, ab_w_out, gla_w_in, gla_w_a2, gla_b_a, gla_w_out,
              ffn_post_norm, ffn_post_w_gate, ffn_post_w_up, ffn_post_w_down, final_norm):
    for l in range(DEPTH):
        h = _rmsnorm(x, ffn_pre_norm[l])
        x = x + 0.5 * _swiglu(h, ffn_pre_w_gate[l], ffn_pre_w_up[l], ffn_pre_w_down[l])
        h = _rmsnorm(x, mix_norm[l])
        if l % 2 == 0:
            x = x + _retention_dilated_mixer(h, ab_w_in[l // 2], ab_w_out[l // 2])
        else:
            x = x + _gla_mixer(h, gla_w_in[l // 2], gla_w_a2[l // 2], gla_b_a[l // 2], gla_w_out[l // 2])
        h = _rmsnorm(x, ffn_post_norm[l])
        x = x + 0.5 * _swiglu(h, ffn_post_w_gate[l], ffn_post_w_up[l], ffn_post_w_down[l])
    return _rmsnorm(x, final_norm)
```

```python
import functools

import jax
import jax.numpy as jnp
import numpy as np
from jax import lax
from jax.experimental import pallas as pl
from jax.experimental.pallas import tpu as pltpu

F32 = jnp.float32
BF16 = jnp.bfloat16

LANES = 128
VMEM_LIMIT_BYTES = 56 * 1024 * 1024

D_MODEL = 1024
FFN_HIDDEN = 2816
EPS = 1e-6

RET_HEADS = 4
RET_DK = 128
RET_DV = 128
RET_CHUNK = 128
RET_THETA = 10000.0

DIL_HEADS = 8
DIL_DH = 64
DIL_PATTERNS = ((128, 1), (512, 4), (2048, 16))
DIL_BLOCK = 128
ROPE_THETA = 500000.0
ROT_DIM = DIL_DH // 4
DIL_MAX_PAD = DIL_BLOCK * max(d for _, d in DIL_PATTERNS)

GLA_HEADS = 4
GLA_DK = (D_MODEL // 2) // GLA_HEADS
GLA_DV = D_MODEL // GLA_HEADS
GLA_LOWRANK = 16
GLA_TAU = 16.0
GLA_CHUNK = 64

EVEN_IN = 4 * RET_HEADS * RET_DK + 3 * DIL_HEADS * DIL_DH
GLA_IN_PADDED = 2 * GLA_HEADS * GLA_DK + 2 * GLA_HEADS * GLA_DV + LANES

ROW_TILE = 512

NT_DIMS = (((1,), (1,)), ((), ()))
TN_DIMS = (((0,), (0,)), ((), ()))


def _params(*semantics):
    return pltpu.CompilerParams(dimension_semantics=semantics,
                                vmem_limit_bytes=VMEM_LIMIT_BYTES)


def _resident(shape):
    zeros = (0,) * len(shape)
    return pl.BlockSpec(shape, lambda *_: zeros, pipeline_mode=pl.Buffered(1))


def _rmsnorm(x, g):
    return x * lax.rsqrt(jnp.mean(x * x, axis=-1, keepdims=True) + EPS) * g


def _head_norm(o):
    mu = jnp.mean(o, axis=-1, keepdims=True)
    c = o - mu
    return c * lax.rsqrt(jnp.mean(c * c, axis=-1, keepdims=True) + EPS)


def _silu(x):
    return x * jax.nn.sigmoid(x)


def _mm(a, b):
    return jnp.dot(a, b, preferred_element_type=F32)


def _ffn_kernel(*refs, n_mix, final):
    x_ref = refs[0]
    mix_refs = refs[1:1 + 2 * n_mix]
    g_ref, wg_ref, wu_ref, wd_ref = refs[1 + 2 * n_mix:5 + 2 * n_mix]
    o_ref = refs[-1]
    x = x_ref[...]
    for i in range(n_mix):
        x = x + _mm(mix_refs[2 * i][...], mix_refs[2 * i + 1][...])
    h = _rmsnorm(x, g_ref[...]).astype(BF16)
    act = (_silu(_mm(h, wg_ref[...])) * _mm(h, wu_ref[...])).astype(BF16)
    y = x + 0.5 * _mm(act, wd_ref[...])
    if final:
        y = _rmsnorm(y, refs[-2][...])
    o_ref[...] = y


def _ffn(x, mixes, g, wg, wu, wd, final_g=None):
    t, d = x.shape
    row = lambda w: pl.BlockSpec((ROW_TILE, w), lambda i: (i, 0))
    args, specs = [x], [row(d)]
    for m, w in mixes:
        args += [m, w]
        specs += [row(m.shape[1]), _resident(w.shape)]
    args += [g, wg, wu, wd]
    specs += [_resident(g.shape), _resident(wg.shape), _resident(wu.shape), _resident(wd.shape)]
    if final_g is not None:
        args.append(final_g)
        specs.append(_resident(final_g.shape))
    return pl.pallas_call(
        functools.partial(_ffn_kernel, n_mix=len(mixes), final=final_g is not None),
        grid=(t // ROW_TILE,),
        in_specs=specs,
        out_specs=row(d),
        out_shape=jax.ShapeDtypeStruct((t, d), F32),
        compiler_params=_params("parallel"),
        name="ffn",
    )(*args)


def _inproj_kernel(x_ref, g_ref, w_ref, o_ref):
    h = _rmsnorm(x_ref[...], g_ref[...]).astype(BF16)
    o_ref[...] = _mm(h, w_ref[...]).astype(BF16)


def _inproj(x, g, w):
    t, d = x.shape
    n = w.shape[1]
    return pl.pallas_call(
        _inproj_kernel,
        grid=(t // ROW_TILE,),
        in_specs=[pl.BlockSpec((ROW_TILE, d), lambda i: (i, 0)), _resident(g.shape), _resident(w.shape)],
        out_specs=pl.BlockSpec((ROW_TILE, n), lambda i: (i, 0)),
        out_shape=jax.ShapeDtypeStruct((t, n), BF16),
        compiler_params=_params("parallel"),
        name="inproj",
    )(x, g, w)


def _retention_kernel(q_ref, k_ref, v_ref, g_ref, cos_ref, sin_ref, dm_ref, qd_ref, kd_ref, cd_ref,
                      o_ref, state_ref):
    c = RET_CHUNK
    n_chunks = q_ref.shape[0] // c
    state_ref[...] = jnp.zeros_like(state_ref)
    dmask = dm_ref[...]
    q_decay = qd_ref[...]
    k_decay = kd_ref[...]
    chunk_decay = cd_ref[...]

    def body(n, carry):
        rows = pl.ds(pl.multiple_of(n * c, c), c)
        cos = cos_ref[rows, :]
        sin = sin_ref[rows, :]
        q = q_ref[rows, :].astype(F32)
        k = k_ref[rows, :].astype(F32)
        q = q * cos + pltpu.roll(q, RET_DK // 2, 1) * sin
        k = (k * cos + pltpu.roll(k, RET_DK // 2, 1) * sin) * (RET_DK ** -0.5)
        v = v_ref[rows, :]
        state = state_ref[...]
        scores = lax.dot_general(q.astype(BF16), k.astype(BF16), NT_DIMS, preferred_element_type=F32) * dmask
        inner = _mm(scores.astype(BF16), v)
        cross = _mm((q * q_decay).astype(BF16), state.astype(BF16))
        kv = lax.dot_general((k * k_decay).astype(BF16), v, TN_DIMS, preferred_element_type=F32)
        state_ref[...] = state * chunk_decay + kv
        o = _head_norm(inner + cross) * _silu(g_ref[rows, :].astype(F32))
        o_ref[rows, :] = o.astype(BF16)
        return carry

    lax.fori_loop(0, n_chunks, body, 0)


def _retention_tables(seq):
    c = RET_CHUNK
    log_g = np.log(1.0 - 2.0 ** (-5.0 - np.arange(RET_HEADS, dtype=np.float64)))
    i = np.arange(c, dtype=np.float64)
    rel = i[:, None] - i[None, :]
    dmask = np.where(rel >= 0, np.exp(log_g[:, None, None] * np.maximum(rel, 0.0)), 0.0)
    ones = np.ones((1, 1, RET_DK))
    q_decay = np.exp(log_g[:, None] * (i + 1.0))[:, :, None] * ones
    k_decay = np.exp(log_g[:, None] * (c - 1 - i))[:, :, None] * ones
    chunk_decay = np.exp(log_g * c)[:, None, None] * ones
    freq = RET_THETA ** (-jnp.linspace(0.0, 1.0, RET_DK // 2, dtype=F32))
    ang = jnp.arange(seq, dtype=F32)[:, None] * freq[None, :]
    cos = jnp.concatenate([jnp.cos(ang), jnp.cos(ang)], axis=1)
    sin = jnp.concatenate([-jnp.sin(ang), jnp.sin(ang)], axis=1)
    f = lambda a: jnp.asarray(a, F32)
    return cos, sin, f(dmask), f(q_decay), f(k_decay), f(chunk_decay)


def _retention(p, batch, seq):
    h = RET_HEADS
    cos, sin, dmask, q_decay, k_decay, chunk_decay = _retention_tables(seq)
    head_block = lambda off: pl.BlockSpec((None, seq, RET_DK), lambda b, j: (b, 0, off + j))
    per_head = lambda a: pl.BlockSpec((None,) + a.shape[1:], lambda b, j: (j, 0, 0))
    return pl.pallas_call(
        _retention_kernel,
        grid=(batch, h),
        in_specs=[head_block(0), head_block(h), head_block(2 * h), head_block(3 * h),
                  _resident(cos.shape), _resident(sin.shape),
                  per_head(dmask), per_head(q_decay), per_head(k_decay), per_head(chunk_decay)],
        out_specs=pl.BlockSpec((None, seq, RET_DV), lambda b, j: (b, 0, j)),
        out_shape=jax.ShapeDtypeStruct((batch, seq, h * RET_DV), BF16),
        scratch_shapes=[pltpu.VMEM((RET_DK, RET_DV), F32)],
        compiler_params=_params("parallel", "parallel"),
        name="retention",
    )(p, p, p, p, cos, sin, dmask, q_decay, k_decay, chunk_decay)


def _dilated_kernel(q_ref, k_ref, v_ref, cos_ref, sina_ref, sinb_ref, o_ref,
                    qs_ref, ks_ref, vs_ref, ob_ref, lb_ref):
    seq = q_ref.shape[0]
    c = DIL_BLOCK
    pad = DIL_MAX_PAD
    half = ROT_DIM // 2
    chunk = 512

    ks_ref[pl.ds(0, pad), :] = jnp.zeros((pad, LANES), F32)
    vs_ref[pl.ds(0, pad), :] = jnp.zeros((pad, LANES), F32)

    def stage(i, carry):
        rows = pl.ds(pl.multiple_of(i * chunk, chunk), chunk)
        dst = pl.ds(pl.multiple_of(pad + i * chunk, chunk), chunk)
        cos, sina, sinb = cos_ref[rows, :], sina_ref[rows, :], sinb_ref[rows, :]
        rot = lambda x: x * cos + pltpu.roll(x, LANES - half, 1) * sina + pltpu.roll(x, half, 1) * sinb
        qs_ref[rows, :] = rot(q_ref[rows, :].astype(F32)) * (DIL_DH ** -0.5)
        ks_ref[dst, :] = rot(k_ref[rows, :].astype(F32))
        vs_ref[dst, :] = v_ref[rows, :].astype(F32)
        return carry

    lax.fori_loop(0, seq // chunk, stage, 0)

    lane = lax.broadcasted_iota(jnp.int32, (c, LANES), 1)
    head0 = lane < DIL_DH
    qi = lax.broadcasted_iota(jnp.int32, (c, 2 * c), 0)
    kj = lax.broadcasted_iota(jnp.int32, (c, 2 * c), 1)
    dist = qi + c - kj
    band = (dist >= 0) & (dist <= c)

    for br, (window, dil) in enumerate(DIL_PATTERNS):
        assert window // dil == c
        blocks = seq // dil // c

        def block(t, carry, br=br, dil=dil, blocks=blocks):
            r = t // blocks
            j = t % blocks
            q_start = r + j * (c * dil)
            q = qs_ref[pl.ds(q_start, c, stride=dil), :] if dil > 1 else qs_ref[pl.ds(q_start, c), :]
            k_start = pad + q_start - c * dil
            if dil > 1:
                k = ks_ref[pl.ds(k_start, 2 * c, stride=dil), :]
                v = vs_ref[pl.ds(k_start, 2 * c, stride=dil), :]
            else:
                k = ks_ref[pl.ds(k_start, 2 * c), :]
                v = vs_ref[pl.ds(k_start, 2 * c), :]
            k = k.astype(BF16)
            v = v.astype(BF16)
            valid = band & (kj >= jnp.where(j > 0, 0, c))
            outs, lses = [], []
            for hd in range(2):
                qh = jnp.where(head0 if hd == 0 else ~head0, q, 0.0).astype(BF16)
                s = lax.dot_general(qh, k, NT_DIMS, preferred_element_type=F32)
                s = jnp.where(valid, s, -jnp.inf)
                m = jnp.max(s, axis=-1, keepdims=True)
                p = jnp.exp(s - m)
                l = jnp.sum(p, axis=-1, keepdims=True)
                outs.append(_mm(p.astype(BF16), v) / l)
                lses.append(m + jnp.log(l))
            o = jnp.where(head0, outs[0], outs[1])
            lse = jnp.where(head0, lses[0], lses[1])
            if dil > 1:
                ob_ref[br, pl.ds(q_start, c, stride=dil), :] = o
                lb_ref[br, pl.ds(q_start, c, stride=dil), :] = lse
            else:
                ob_ref[br, pl.ds(q_start, c), :] = o
                lb_ref[br, pl.ds(q_start, c), :] = lse
            return carry

        lax.fori_loop(0, dil * blocks, block, 0)

    def combine(i, carry):
        rows = pl.ds(pl.multiple_of(i * chunk, chunk), chunk)
        lse = [lb_ref[b, rows, :] for b in range(len(DIL_PATTERNS))]
        m = functools.reduce(jnp.maximum, lse)
        w = [jnp.exp(x - m) for x in lse]
        num = sum(wb * ob_ref[b, rows, :] for b, wb in enumerate(w))
        o_ref[rows, :] = (num / sum(w)).astype(BF16)
        return carry

    lax.fori_loop(0, seq // chunk, combine, 0)


def _dilated_tables(seq):
    half = ROT_DIM // 2
    freq = ROPE_THETA ** (-jnp.arange(0, ROT_DIM, 2, dtype=F32) / ROT_DIM)
    ang = jnp.arange(seq, dtype=F32)[:, None] * freq[None, :]
    cos, sin = jnp.cos(ang), jnp.sin(ang)
    zeros = jnp.zeros_like(sin)
    rest = DIL_DH - ROT_DIM
    head = lambda a, b, fill: jnp.concatenate([a, b, jnp.full((seq, rest), fill, F32)], axis=1)
    two = lambda x: jnp.concatenate([x, x], axis=1)
    return two(head(cos, cos, 1.0)), two(head(-sin, zeros, 0.0)), two(head(zeros, sin, 0.0))


def _dilated(p, batch, seq):
    pairs = DIL_HEADS * DIL_DH // LANES
    base = 4 * RET_HEADS * RET_DK // LANES
    cos, sina, sinb = _dilated_tables(seq)
    pair_block = lambda off: pl.BlockSpec((None, seq, LANES), lambda b, j: (b, 0, off + j))
    n_br = len(DIL_PATTERNS)
    return pl.pallas_call(
        _dilated_kernel,
        grid=(batch, pairs),
        in_specs=[pair_block(base), pair_block(base + pairs), pair_block(base + 2 * pairs),
                  _resident(cos.shape), _resident(sina.shape), _resident(sinb.shape)],
        out_specs=pl.BlockSpec((None, seq, LANES), lambda b, j: (b, 0, j)),
        out_shape=jax.ShapeDtypeStruct((batch, seq, pairs * LANES), BF16),
        scratch_shapes=[pltpu.VMEM((seq, LANES), F32),
                        pltpu.VMEM((seq + DIL_MAX_PAD, LANES), F32),
                        pltpu.VMEM((seq + DIL_MAX_PAD, LANES), F32),
                        pltpu.VMEM((n_br, seq, LANES), F32),
                        pltpu.VMEM((n_br, seq, LANES), F32)],
        compiler_params=_params("parallel", "parallel"),
        name="dilated",
    )(p, p, p, cos, sina, sinb)


def _gla_kernel(q_ref, k_ref, v_ref, r_ref, a_ref, wa_ref, ba_ref, o_ref, state_ref):
    c = GLA_CHUNK
    n_chunks = q_ref.shape[0] // c
    state_ref[...] = jnp.zeros_like(state_ref)
    wa = wa_ref[...]
    ba = ba_ref[...]
    row = lax.broadcasted_iota(jnp.int32, (c, GLA_DK), 0)
    causal = lax.broadcasted_iota(jnp.int32, (c, c), 0) >= lax.broadcasted_iota(jnp.int32, (c, c), 1)

    def body(n, carry):
        rows = pl.ds(pl.multiple_of(n * c, c), c)
        z = _mm(a_ref[rows, :], wa) + ba
        b = (jnp.minimum(z, 0.0) - jnp.log1p(jnp.exp(-jnp.abs(z)))) * (1.0 / GLA_TAU)
        shift = 1
        while shift < c:
            b = b + jnp.where(row >= shift, pltpu.roll(b, shift, 0), 0.0)
            shift *= 2
        b_last = b[c - 1:c, :]
        q = q_ref[rows, :].astype(F32) * (GLA_DK ** -0.5)
        k = k_ref[rows, :].astype(F32)
        v = v_ref[rows, :]
        q_t = (q * jnp.exp(b)).astype(BF16)
        k_t = (k * jnp.exp(-b)).astype(BF16)
        k_l = (k * jnp.exp(b_last - b)).astype(BF16)
        state = state_ref[...]
        scores = jnp.where(causal, lax.dot_general(q_t, k_t, NT_DIMS, preferred_element_type=F32), 0.0)
        inner = _mm(scores.astype(BF16), v)
        cross = lax.dot_general(q_t, state.astype(BF16), NT_DIMS, preferred_element_type=F32)
        kv_t = lax.dot_general(v, k_l, TN_DIMS, preferred_element_type=F32)
        state_ref[...] = state * jnp.exp(b_last) + kv_t
        o = _head_norm(inner + cross) * _silu(r_ref[rows, :].astype(F32))
        o_ref[rows, :] = o.astype(BF16)
        return carry

    lax.fori_loop(0, n_chunks, body, 0)


def _gla(p, wa, ba, batch, seq):
    h = GLA_HEADS
    k_block = lambda off: pl.BlockSpec((None, seq, GLA_DK), lambda b, j: (b, 0, off + j))
    v_block = lambda off: pl.BlockSpec((None, seq, GLA_DV), lambda b, j: (b, 0, off + j))
    v_base = 2 * h * GLA_DK // GLA_DV
    a_base = (2 * h * GLA_DK + 2 * h * GLA_DV) // LANES
    return pl.pallas_call(
        _gla_kernel,
        grid=(batch, h),
        in_specs=[k_block(0), k_block(h), v_block(v_base), v_block(v_base + h),
                  pl.BlockSpec((None, seq, LANES), lambda b, j: (b, 0, a_base)),
                  pl.BlockSpec((LANES, GLA_DK), lambda b, j: (0, j)),
                  pl.BlockSpec((1, GLA_DK), lambda b, j: (0, j))],
        out_specs=pl.BlockSpec((None, seq, GLA_DV), lambda b, j: (b, 0, j)),
        out_shape=jax.ShapeDtypeStruct((batch, seq, h * GLA_DV), BF16),
        scratch_shapes=[pltpu.VMEM((GLA_DV, GLA_DK), F32)],
        compiler_params=_params("parallel", "parallel"),
        name="gla",
    )(p, p, p, p, p, wa, ba)


def kernel(x, ffn_pre_norm, ffn_pre_w_gate, ffn_pre_w_up, ffn_pre_w_down, mix_norm, ab_w_in, ab_w_out,
           gla_w_in, gla_w_a2, gla_b_a, gla_w_out, ffn_post_norm, ffn_post_w_gate, ffn_post_w_up,
           ffn_post_w_down, final_norm):
    batch, seq, d = x.shape
    depth = ffn_pre_norm.shape[0]
    bf = lambda w: w.astype(BF16)
    gain = lambda g: g.reshape(1, d)
    ret_out = RET_HEADS * RET_DV
    pad_cols = GLA_IN_PADDED - gla_w_in.shape[2]
    pad_rows = LANES - gla_w_a2.shape[1]

    xt = x.reshape(batch * seq, d)
    for l in range(depth):
        xt = _ffn(xt, [], gain(ffn_pre_norm[l]), bf(ffn_pre_w_gate[l]), bf(ffn_pre_w_up[l]),
                  bf(ffn_pre_w_down[l]))
        if l % 2 == 0:
            p = _inproj(xt, gain(mix_norm[l]), bf(ab_w_in[l // 2]))
            p = p.reshape(batch, seq, EVEN_IN)
            o_r = _retention(p, batch, seq).reshape(batch * seq, ret_out)
            o_d = _dilated(p, batch, seq).reshape(batch * seq, DIL_HEADS * DIL_DH)
            w_out = bf(ab_w_out[l // 2])
            mixes = [(o_r, w_out[:ret_out]), (o_d, w_out[ret_out:])]
        else:
            w_in = jnp.pad(bf(gla_w_in[l // 2]), ((0, 0), (0, pad_cols)))
            p = _inproj(xt, gain(mix_norm[l]), w_in).reshape(batch, seq, GLA_IN_PADDED)
            wa = jnp.pad(bf(gla_w_a2[l // 2]), ((0, pad_rows), (0, 0)))
            o = _gla(p, wa, gla_b_a[l // 2].reshape(1, -1), batch, seq)
            mixes = [(o.reshape(batch * seq, GLA_HEADS * GLA_DV), bf(gla_w_out[l // 2]))]
        xt = _ffn(xt, mixes, gain(ffn_post_norm[l]), bf(ffn_post_w_gate[l]), bf(ffn_post_w_up[l]),
                  bf(ffn_post_w_down[l]), final_g=gain(final_norm) if l == depth - 1 else None)
    return xt.reshape(batch, seq, d)
```

```python
import functools

import jax
import jax.numpy as jnp
import numpy as np
from jax import lax
from jax.experimental import pallas as pl
from jax.experimental.pallas import tpu as pltpu

F32 = jnp.float32
BF16 = jnp.bfloat16

LANES = 128
VMEM_LIMIT_BYTES = 56 * 1024 * 1024

D_MODEL = 1024
FFN_HIDDEN = 2816
EPS = 1e-6

RET_HEADS = 4
RET_DK = 128
RET_DV = 128
RET_CHUNK = 128
RET_THETA = 10000.0

DIL_HEADS = 8
DIL_DH = 64
DIL_PATTERNS = ((128, 1), (512, 4), (2048, 16))
DIL_BLOCK = 128
DIL_GROUP = 8
CHUNK_GROUP = 8
ROPE_THETA = 500000.0
ROT_DIM = DIL_DH // 4
DIL_MAX_PAD = DIL_BLOCK * max(d for _, d in DIL_PATTERNS)

GLA_HEADS = 4
GLA_DK = (D_MODEL // 2) // GLA_HEADS
GLA_DV = D_MODEL // GLA_HEADS
GLA_LOWRANK = 16
GLA_TAU = 16.0
GLA_CHUNK = 64

EVEN_IN = 4 * RET_HEADS * RET_DK + 3 * DIL_HEADS * DIL_DH
GLA_IN_PADDED = 2 * GLA_HEADS * GLA_DK + 2 * GLA_HEADS * GLA_DV + LANES

ROW_TILE = 512
PREP_TILE = 512

NT_DIMS = (((1,), (1,)), ((), ()))
TN_DIMS = (((0,), (0,)), ((), ()))


def _params(*semantics):
    return pltpu.CompilerParams(dimension_semantics=semantics,
                                vmem_limit_bytes=VMEM_LIMIT_BYTES)


def _resident(shape):
    zeros = (0,) * len(shape)
    return pl.BlockSpec(shape, lambda *_: zeros, pipeline_mode=pl.Buffered(1))


def _rmsnorm(x, g):
    return x * lax.rsqrt(jnp.mean(x * x, axis=-1, keepdims=True) + EPS) * g


def _head_norm(o):
    mu = jnp.mean(o, axis=-1, keepdims=True)
    c = o - mu
    return c * lax.rsqrt(jnp.mean(c * c, axis=-1, keepdims=True) + EPS)


def _silu(x):
    return x * jax.nn.sigmoid(x)


def _mm(a, b):
    return jnp.dot(a, b, preferred_element_type=F32)


def _mm_nt(a, b):
    return lax.dot_general(a, b, NT_DIMS, preferred_element_type=F32)


def _mm_tn(a, b):
    return lax.dot_general(a, b, TN_DIMS, preferred_element_type=F32)


def _tile_rows(i, n):
    return pl.ds(pl.multiple_of(i * n, n), n)


def _ffn_kernel(*refs, n_mix, final):
    x_ref = refs[0]
    mix_refs = refs[1:1 + 2 * n_mix]
    g_ref, wg_ref, wu_ref, wd_ref = refs[1 + 2 * n_mix:5 + 2 * n_mix]
    o_ref = refs[-1]
    x = x_ref[...]
    for i in range(n_mix):
        x = x + _mm(mix_refs[2 * i][...], mix_refs[2 * i + 1][...])
    h = _rmsnorm(x, g_ref[...]).astype(BF16)
    act = (_silu(_mm(h, wg_ref[...])) * _mm(h, wu_ref[...])).astype(BF16)
    y = x + 0.5 * _mm(act, wd_ref[...])
    if final:
        y = _rmsnorm(y, refs[-2][...])
    o_ref[...] = y


def _ffn(x, mixes, g, wg, wu, wd, final_g=None):
    t, d = x.shape
    row = lambda w: pl.BlockSpec((ROW_TILE, w), lambda i: (i, 0))
    args, specs = [x], [row(d)]
    for m, w in mixes:
        args += [m, w]
        specs += [row(m.shape[1]), _resident(w.shape)]
    args += [g, wg, wu, wd]
    specs += [_resident(g.shape), _resident(wg.shape), _resident(wu.shape), _resident(wd.shape)]
    if final_g is not None:
        args.append(final_g)
        specs.append(_resident(final_g.shape))
    return pl.pallas_call(
        functools.partial(_ffn_kernel, n_mix=len(mixes), final=final_g is not None),
        grid=(t // ROW_TILE,),
        in_specs=specs,
        out_specs=row(d),
        out_shape=jax.ShapeDtypeStruct((t, d), F32),
        compiler_params=_params("parallel"),
        name="ffn",
    )(*args)


def _inproj_kernel(x_ref, g_ref, w_ref, o_ref):
    h = _rmsnorm(x_ref[...], g_ref[...]).astype(BF16)
    o_ref[...] = _mm(h, w_ref[...]).astype(BF16)


def _inproj(x, g, w):
    t, d = x.shape
    n = w.shape[1]
    return pl.pallas_call(
        _inproj_kernel,
        grid=(t // ROW_TILE,),
        in_specs=[pl.BlockSpec((ROW_TILE, d), lambda i: (i, 0)), _resident(g.shape), _resident(w.shape)],
        out_specs=pl.BlockSpec((ROW_TILE, n), lambda i: (i, 0)),
        out_shape=jax.ShapeDtypeStruct((t, n), BF16),
        compiler_params=_params("parallel"),
        name="inproj",
    )(x, g, w)


def _retention_kernel(q_ref, k_ref, v_ref, g_ref, cos_ref, sin_ref, dm_ref, qd_ref, kd_ref, cd_ref,
                      o_ref, qs_ref, qdec_ref, ks_ref, kdec_ref):
    c = RET_CHUNK
    seq = q_ref.shape[0]

    def prep(i, carry):
        rows = _tile_rows(i, PREP_TILE)
        cos = cos_ref[rows, :]
        sin = sin_ref[rows, :]
        q = q_ref[rows, :].astype(F32)
        k = k_ref[rows, :].astype(F32)
        q = q * cos + pltpu.roll(q, RET_DK // 2, 1) * sin
        k = (k * cos + pltpu.roll(k, RET_DK // 2, 1) * sin) * (RET_DK ** -0.5)
        qs_ref[rows, :] = q.astype(BF16)
        qdec_ref[rows, :] = (q * qd_ref[...]).astype(BF16)
        ks_ref[rows, :] = k.astype(BF16)
        kdec_ref[rows, :] = (k * kd_ref[...]).astype(BF16)
        return carry

    lax.fori_loop(0, seq // PREP_TILE, prep, 0)

    dmask = dm_ref[...]
    chunk_decay = cd_ref[...]

    def body(n, state):
        rows = [_tile_rows(n * CHUNK_GROUP + g, c) for g in range(CHUNK_GROUP)]
        vs = [v_ref[r, :] for r in rows]
        scores = [_mm_nt(qs_ref[r, :], ks_ref[r, :]) for r in rows]
        kvs = [_mm_tn(kdec_ref[r, :], v) for r, v in zip(rows, vs)]
        for r, v, s, kv in zip(rows, vs, scores, kvs):
            lhs = jnp.concatenate([(s * dmask).astype(BF16), qdec_ref[r, :]], axis=1)
            rhs = jnp.concatenate([v, state.astype(BF16)], axis=0)
            o = _head_norm(_mm(lhs, rhs)) * _silu(g_ref[r, :].astype(F32))
            o_ref[r, :] = o.astype(BF16)
            state = state * chunk_decay + kv
        return state

    lax.fori_loop(0, seq // c // CHUNK_GROUP, body, jnp.zeros((RET_DK, RET_DV), F32))


def _retention_tables(seq):
    c = RET_CHUNK
    log_g = np.log(1.0 - 2.0 ** (-5.0 - np.arange(RET_HEADS, dtype=np.float64)))
    i = np.arange(c, dtype=np.float64)
    rel = i[:, None] - i[None, :]
    dmask = np.where(rel >= 0, np.exp(log_g[:, None, None] * np.maximum(rel, 0.0)), 0.0)
    ones = np.ones((1, PREP_TILE // c, RET_DK))
    tile = lambda a: np.kron(ones, a[:, :, None])
    q_decay = tile(np.exp(log_g[:, None] * (i + 1.0)))
    k_decay = tile(np.exp(log_g[:, None] * (c - 1 - i)))
    chunk_decay = np.exp(log_g * c)[:, None, None] * np.ones((1, 1, RET_DV))
    freq = RET_THETA ** (-jnp.linspace(0.0, 1.0, RET_DK // 2, dtype=F32))
    ang = jnp.arange(seq, dtype=F32)[:, None] * freq[None, :]
    cos = jnp.concatenate([jnp.cos(ang), jnp.cos(ang)], axis=1)
    sin = jnp.concatenate([-jnp.sin(ang), jnp.sin(ang)], axis=1)
    f = lambda a: jnp.asarray(a, F32)
    return cos, sin, f(dmask), f(q_decay), f(k_decay), f(chunk_decay)


def _retention(p, batch, seq):
    h = RET_HEADS
    cos, sin, dmask, q_decay, k_decay, chunk_decay = _retention_tables(seq)
    head_block = lambda off: pl.BlockSpec((None, seq, RET_DK), lambda b, j: (b, 0, off + j))
    per_head = lambda a: pl.BlockSpec((None,) + a.shape[1:], lambda b, j: (j, 0, 0))
    return pl.pallas_call(
        _retention_kernel,
        grid=(batch, h),
        in_specs=[head_block(0), head_block(h), head_block(2 * h), head_block(3 * h),
                  _resident(cos.shape), _resident(sin.shape),
                  per_head(dmask), per_head(q_decay), per_head(k_decay), per_head(chunk_decay)],
        out_specs=pl.BlockSpec((None, seq, RET_DV), lambda b, j: (b, 0, j)),
        out_shape=jax.ShapeDtypeStruct((batch, seq, h * RET_DV), BF16),
        scratch_shapes=[pltpu.VMEM((seq, RET_DK), BF16)] * 4,
        compiler_params=_params("parallel", "parallel"),
        name="retention",
    )(p, p, p, p, cos, sin, dmask, q_decay, k_decay, chunk_decay)


def _dilated_kernel(q_ref, k_ref, v_ref, cos_ref, sina_ref, sinb_ref, o_ref,
                    qs_ref, ks_ref, vs_ref, ob_ref, mb_ref, lb_ref):
    seq = q_ref.shape[0]
    c = DIL_BLOCK
    pad = DIL_MAX_PAD
    half = ROT_DIM // 2

    ks_ref[pl.ds(0, pad), :] = jnp.zeros((pad, LANES), F32)
    vs_ref[pl.ds(0, pad), :] = jnp.zeros((pad, LANES), F32)

    def stage(i, carry):
        rows = _tile_rows(i, PREP_TILE)
        dst = pl.ds(pl.multiple_of(pad + i * PREP_TILE, PREP_TILE), PREP_TILE)
        cos, sina, sinb = cos_ref[rows, :], sina_ref[rows, :], sinb_ref[rows, :]
        rot = lambda x: x * cos + pltpu.roll(x, LANES - half, 1) * sina + pltpu.roll(x, half, 1) * sinb
        qs_ref[rows, :] = rot(q_ref[rows, :].astype(F32)) * (DIL_DH ** -0.5)
        ks_ref[dst, :] = rot(k_ref[rows, :].astype(F32))
        vs_ref[dst, :] = v_ref[rows, :].astype(F32)
        return carry

    lax.fori_loop(0, seq // PREP_TILE, stage, 0)

    head0 = lax.broadcasted_iota(jnp.int32, (c, LANES), 1) < DIL_DH
    qi = lax.broadcasted_iota(jnp.int32, (c, 2 * c), 0)
    kj = lax.broadcasted_iota(jnp.int32, (c, 2 * c), 1)
    dist = qi + c - kj
    band = (dist >= 0) & (dist <= c)

    def strided(ref, start, n, dil, lead=None):
        idx = pl.ds(start, n, stride=dil) if dil > 1 else pl.ds(start, n)
        return (ref, (idx, slice(None)) if lead is None else (lead, idx, slice(None)))

    for br, (window, dil) in enumerate(DIL_PATTERNS):
        assert window // dil == c
        blocks = seq // dil // c
        run = min(DIL_GROUP, blocks)
        runs = blocks // run
        classes = DIL_GROUP // run

        def block_group(t, carry, br=br, dil=dil, run=run, runs=runs, classes=classes):
            jg = t % runs
            tiles = []
            for ci in range(classes):
                q_start = (t // runs) * classes + ci + jg * (run * c * dil)
                ref, idx = strided(qs_ref, q_start, run * c, dil)
                q = ref[idx]
                k_start = pad + q_start - c * dil
                ref, idx = strided(ks_ref, k_start, (run + 1) * c, dil)
                k = ref[idx].astype(BF16)
                ref, idx = strided(vs_ref, k_start, (run + 1) * c, dil)
                v = ref[idx]
                head0_kv = lax.broadcasted_iota(jnp.int32, v.shape, 1) < DIL_DH
                v_heads = (jnp.where(head0_kv, v, 1.0).astype(BF16), jnp.where(head0_kv, 1.0, v).astype(BF16))
                for u in range(run):
                    qu = q[u * c:(u + 1) * c]
                    q_heads = (jnp.where(head0, qu, 0.0).astype(BF16), jnp.where(head0, 0.0, qu).astype(BF16))
                    valid = band & (kj >= jnp.where(jg > 0, 0, c)) if u == 0 else band
                    tiles.append((q_start + u * (c * dil), valid, q_heads, k[u * c:(u + 2) * c],
                                  [vh[u * c:(u + 2) * c] for vh in v_heads]))
            scores = [[_mm_nt(qh, ku) for qh in q_heads] for _, _, q_heads, ku, _ in tiles]
            for (row0, valid, _, _, vu), s_heads in zip(tiles, scores):
                res, mx = [], []
                for hd in range(2):
                    s = jnp.where(valid, s_heads[hd], -jnp.inf)
                    m = jnp.max(s, axis=-1, keepdims=True)
                    res.append(_mm(jnp.exp(s - m).astype(BF16), vu[hd]))
                    mx.append(jnp.broadcast_to(m, (c, LANES)))
                ref, idx = strided(ob_ref, row0, c, dil, lead=br)
                ref[idx] = jnp.where(head0, res[0], res[1])
                ref, idx = strided(lb_ref, row0, c, dil, lead=br)
                ref[idx] = jnp.where(head0, res[1], res[0])
                ref, idx = strided(mb_ref, row0, c, dil, lead=br)
                ref[idx] = jnp.where(head0, mx[0], mx[1])
            return carry

        lax.fori_loop(0, dil * blocks // DIL_GROUP, block_group, 0)

    def combine(i, carry):
        rows = _tile_rows(i, PREP_TILE)
        n_br = len(DIL_PATTERNS)
        ms = [mb_ref[b, rows, :] for b in range(n_br)]
        m = functools.reduce(jnp.maximum, ms)
        es = [jnp.exp(x - m) for x in ms]
        num = sum(es[b] * ob_ref[b, rows, :] for b in range(n_br))
        den = sum(es[b] * pltpu.roll(lb_ref[b, rows, :], DIL_DH, 1) for b in range(n_br))
        o_ref[rows, :] = (num / den).astype(BF16)
        return carry

    lax.fori_loop(0, seq // PREP_TILE, combine, 0)


def _dilated_tables(seq):
    freq = ROPE_THETA ** (-jnp.arange(0, ROT_DIM, 2, dtype=F32) / ROT_DIM)
    ang = jnp.arange(seq, dtype=F32)[:, None] * freq[None, :]
    cos, sin = jnp.cos(ang), jnp.sin(ang)
    zeros = jnp.zeros_like(sin)
    rest = DIL_DH - ROT_DIM
    head = lambda a, b, fill: jnp.concatenate([a, b, jnp.full((seq, rest), fill, F32)], axis=1)
    two = lambda x: jnp.concatenate([x, x], axis=1)
    return two(head(cos, cos, 1.0)), two(head(-sin, zeros, 0.0)), two(head(zeros, sin, 0.0))


def _dilated(p, batch, seq):
    pairs = DIL_HEADS * DIL_DH // LANES
    base = 4 * RET_HEADS * RET_DK // LANES
    cos, sina, sinb = _dilated_tables(seq)
    pair_block = lambda off: pl.BlockSpec((None, seq, LANES), lambda b, j: (b, 0, off + j))
    n_br = len(DIL_PATTERNS)
    return pl.pallas_call(
        _dilated_kernel,
        grid=(batch, pairs),
        in_specs=[pair_block(base), pair_block(base + pairs), pair_block(base + 2 * pairs),
                  _resident(cos.shape), _resident(sina.shape), _resident(sinb.shape)],
        out_specs=pl.BlockSpec((None, seq, LANES), lambda b, j: (b, 0, j)),
        out_shape=jax.ShapeDtypeStruct((batch, seq, pairs * LANES), BF16),
        scratch_shapes=[pltpu.VMEM((seq, LANES), F32),
                        pltpu.VMEM((seq + DIL_MAX_PAD, LANES), F32),
                        pltpu.VMEM((seq + DIL_MAX_PAD, LANES), F32),
                        pltpu.VMEM((n_br, seq, LANES), F32),
                        pltpu.VMEM((n_br, seq, LANES), F32),
                        pltpu.VMEM((n_br, seq, LANES), F32)],
        compiler_params=_params("parallel", "parallel"),
        name="dilated",
    )(p, p, p, cos, sina, sinb)


def _gla_kernel(q_ref, k_ref, v_ref, r_ref, a_ref, wa_ref, ba_ref, o_ref,
                qt_ref, kt_ref, kl_ref, dec_ref):
    c = GLA_CHUNK
    seq = q_ref.shape[0]
    per_tile = PREP_TILE // c
    row_in_chunk = lax.broadcasted_iota(jnp.int32, (PREP_TILE, GLA_DK), 0) % c

    def prep(i, carry):
        rows = _tile_rows(i, PREP_TILE)
        z = _mm(a_ref[rows, :], wa_ref[...]) + ba_ref[...]
        b = (jnp.minimum(z, 0.0) - jnp.log1p(jnp.exp(-jnp.abs(z)))) * (1.0 / GLA_TAU)
        shift = 1
        while shift < c:
            b = b + jnp.where(row_in_chunk >= shift, pltpu.roll(b, shift, 0), 0.0)
            shift *= 2
        b3 = b.reshape(per_tile, c, GLA_DK)
        b_last = b3[:, c - 1:c, :]
        q = q_ref[rows, :].astype(F32) * (GLA_DK ** -0.5)
        k = k_ref[rows, :].astype(F32)
        qt_ref[rows, :] = (q * jnp.exp(b)).astype(BF16)
        kt_ref[rows, :] = (k * jnp.exp(-b)).astype(BF16)
        k_l = k.reshape(per_tile, c, GLA_DK) * jnp.exp(b_last - b3)
        kl_ref[rows, :] = k_l.reshape(PREP_TILE, GLA_DK).astype(BF16)
        dec_ref[_tile_rows(i, per_tile)] = jnp.exp(b_last)
        return carry

    lax.fori_loop(0, seq // PREP_TILE, prep, 0)

    causal = lax.broadcasted_iota(jnp.int32, (c, c), 0) >= lax.broadcasted_iota(jnp.int32, (c, c), 1)

    def body(n, state):
        chunks = [n * CHUNK_GROUP + g for g in range(CHUNK_GROUP)]
        rows = [_tile_rows(j, c) for j in chunks]
        vs = [v_ref[r, :] for r in rows]
        qts = [qt_ref[r, :] for r in rows]
        scores = [_mm_nt(q_t, kt_ref[r, :]) for r, q_t in zip(rows, qts)]
        kvs = [_mm_tn(v, kl_ref[r, :]) for r, v in zip(rows, vs)]
        for j, r, v, q_t, s, kv in zip(chunks, rows, vs, qts, scores, kvs):
            o = _mm(jnp.where(causal, s, 0.0).astype(BF16), v) + _mm_nt(q_t, state.astype(BF16))
            o = _head_norm(o) * _silu(r_ref[r, :].astype(F32))
            o_ref[r, :] = o.astype(BF16)
            state = state * dec_ref[j] + kv
        return state

    lax.fori_loop(0, seq // c // CHUNK_GROUP, body, jnp.zeros((GLA_DV, GLA_DK), F32))


def _gla(p, wa, ba, batch, seq):
    h = GLA_HEADS
    k_block = lambda off: pl.BlockSpec((None, seq, GLA_DK), lambda b, j: (b, 0, off + j))
    v_block = lambda off: pl.BlockSpec((None, seq, GLA_DV), lambda b, j: (b, 0, off + j))
    v_base = 2 * h * GLA_DK // GLA_DV
    a_base = (2 * h * GLA_DK + 2 * h * GLA_DV) // LANES
    return pl.pallas_call(
        _gla_kernel,
        grid=(batch, h),
        in_specs=[k_block(0), k_block(h), v_block(v_base), v_block(v_base + h),
                  pl.BlockSpec((None, seq, LANES), lambda b, j: (b, 0, a_base)),
                  pl.BlockSpec((LANES, GLA_DK), lambda b, j: (0, j)),
                  pl.BlockSpec((1, GLA_DK), lambda b, j: (0, j))],
        out_specs=pl.BlockSpec((None, seq, GLA_DV), lambda b, j: (b, 0, j)),
        out_shape=jax.ShapeDtypeStruct((batch, seq, h * GLA_DV), BF16),
        scratch_shapes=[pltpu.VMEM((seq, GLA_DK), BF16)] * 3 + [pltpu.VMEM((seq // GLA_CHUNK, 1, GLA_DK), F32)],
        compiler_params=_params("parallel", "parallel"),
        name="gla",
    )(p, p, p, p, p, wa, ba)


def kernel(x, ffn_pre_norm, ffn_pre_w_gate, ffn_pre_w_up, ffn_pre_w_down, mix_norm, ab_w_in, ab_w_out,
           gla_w_in, gla_w_a2, gla_b_a, gla_w_out, ffn_post_norm, ffn_post_w_gate, ffn_post_w_up,
           ffn_post_w_down, final_norm):
    batch, seq, d = x.shape
    depth = ffn_pre_norm.shape[0]
    bf = lambda w: w.astype(BF16)
    gain = lambda g: g.reshape(1, d)
    ret_out = RET_HEADS * RET_DV
    pad_cols = GLA_IN_PADDED - gla_w_in.shape[2]
    pad_rows = LANES - gla_w_a2.shape[1]

    xt = x.reshape(batch * seq, d)
    for l in range(depth):
        xt = _ffn(xt, [], gain(ffn_pre_norm[l]), bf(ffn_pre_w_gate[l]), bf(ffn_pre_w_up[l]),
                  bf(ffn_pre_w_down[l]))
        if l % 2 == 0:
            p = _inproj(xt, gain(mix_norm[l]), bf(ab_w_in[l // 2]))
            p = p.reshape(batch, seq, EVEN_IN)
            o_r = _retention(p, batch, seq).reshape(batch * seq, ret_out)
            o_d = _dilated(p, batch, seq).reshape(batch * seq, DIL_HEADS * DIL_DH)
            w_out = bf(ab_w_out[l // 2])
            mixes = [(o_r, w_out[:ret_out]), (o_d, w_out[ret_out:])]
        else:
            w_in = jnp.pad(bf(gla_w_in[l // 2]), ((0, 0), (0, pad_cols)))
            p = _inproj(xt, gain(mix_norm[l]), w_in).reshape(batch, seq, GLA_IN_PADDED)
            wa = jnp.pad(bf(gla_w_a2[l // 2]), ((0, pad_rows), (0, 0)))
            o = _gla(p, wa, gla_b_a[l // 2].reshape(1, -1), batch, seq)
            mixes = [(o.reshape(batch * seq, GLA_HEADS * GLA_DV), bf(gla_w_out[l // 2]))]
        xt = _ffn(xt, mixes, gain(ffn_post_norm[l]), bf(ffn_post_w_gate[l]), bf(ffn_post_w_up[l]),
                  bf(ffn_post_w_down[l]), final_g=gain(final_norm) if l == depth - 1 else None)
    return xt.reshape(batch, seq, d)
```

```python
import functools

import jax
import jax.numpy as jnp
import numpy as np
from jax import lax
from jax.experimental import pallas as pl
from jax.experimental.pallas import tpu as pltpu

F32 = jnp.float32
BF16 = jnp.bfloat16

LANES = 128
VMEM_LIMIT_BYTES = 56 * 1024 * 1024

D_MODEL = 1024
FFN_HIDDEN = 2816
EPS = 1e-6

RET_HEADS = 4
RET_DK = 128
RET_DV = 128
RET_CHUNK = 128
RET_THETA = 10000.0
RET_WIDTH = RET_HEADS * RET_DK

DIL_HEADS = 8
DIL_DH = 64
DIL_PATTERNS = ((128, 1), (512, 4), (2048, 16))
DIL_BLOCK = 128
DIL_GROUP = 8
ROPE_THETA = 500000.0
ROT_DIM = DIL_DH // 4
DIL_MAX_PAD = DIL_BLOCK * max(d for _, d in DIL_PATTERNS)
DIL_WIDTH = DIL_HEADS * DIL_DH

GLA_HEADS = 4
GLA_DK = (D_MODEL // 2) // GLA_HEADS
GLA_DV = D_MODEL // GLA_HEADS
GLA_LOWRANK = 16
GLA_TAU = 16.0
GLA_CHUNK = 64
GLA_KW = GLA_HEADS * GLA_DK
GLA_VW = GLA_HEADS * GLA_DV

CHUNK_GROUP = 16

EVEN_RQ, EVEN_RQD, EVEN_RK, EVEN_RKD, EVEN_RV, EVEN_RG, EVEN_DQ, EVEN_DK, EVEN_DV = range(0, 36, 4)
EVEN_PREPARED = 9 * RET_WIDTH
GLA_PREPARED = 3 * GLA_KW + 2 * GLA_VW
GLA_IN_PADDED = 2 * GLA_KW + 2 * GLA_VW + LANES

ROW_TILE = 512
PREP_TILE = 512

NT_DIMS = (((1,), (1,)), ((), ()))
TN_DIMS = (((0,), (0,)), ((), ()))


def _params(*semantics):
    return pltpu.CompilerParams(dimension_semantics=semantics,
                                vmem_limit_bytes=VMEM_LIMIT_BYTES)


def _resident(a):
    zeros = (0,) * a.ndim
    return a, pl.BlockSpec(a.shape, lambda *_: zeros, pipeline_mode=pl.Buffered(1))


def _layer(w, l, rows=None, row_block=0):
    shape = (None, w.shape[1] if rows is None else rows, w.shape[2])
    return w, pl.BlockSpec(shape, lambda *_: (l, row_block, 0), pipeline_mode=pl.Buffered(1))


def _rmsnorm(x, g):
    return x * lax.rsqrt(jnp.mean(x * x, axis=-1, keepdims=True) + EPS) * g


def _head_norm(o):
    mu = jnp.mean(o, axis=-1, keepdims=True)
    c = o - mu
    return c * lax.rsqrt(jnp.mean(c * c, axis=-1, keepdims=True) + EPS)


def _silu(x):
    return x * jax.nn.sigmoid(x)


def _mm(a, b):
    return jnp.dot(a, b, preferred_element_type=F32)


def _mm_nt(a, b):
    return lax.dot_general(a, b, NT_DIMS, preferred_element_type=F32)


def _mm_tn(a, b):
    return lax.dot_general(a, b, TN_DIMS, preferred_element_type=F32)


def _tile_rows(i, n):
    return pl.ds(pl.multiple_of(i * n, n), n)


def _lane_block(j, n=1):
    return slice(j * LANES, (j + n) * LANES)


def _ffn_kernel(*refs, n_mix, final):
    x_ref = refs[0]
    mix_refs = refs[1:1 + 2 * n_mix]
    g_ref, wg_ref, wu_ref, wd_ref = refs[1 + 2 * n_mix:5 + 2 * n_mix]
    o_ref = refs[-1]
    x = x_ref[...]
    for i in range(n_mix):
        x = x + _mm(mix_refs[2 * i][...], mix_refs[2 * i + 1][...])
    h = _rmsnorm(x, g_ref[...]).astype(BF16)
    act = (_silu(_mm(h, wg_ref[...])) * _mm(h, wu_ref[...])).astype(BF16)
    y = x + 0.5 * _mm(act, wd_ref[...])
    if final:
        y = _rmsnorm(y, refs[-2][...])
    o_ref[...] = y


def _ffn(x, mixes, weights, final_g=None):
    t, d = x.shape
    row = lambda w: pl.BlockSpec((ROW_TILE, w), lambda i: (i, 0))
    args, specs = [x], [row(d)]
    for m, (w, spec) in mixes:
        args += [m, w]
        specs += [row(m.shape[1]), spec]
    for w, spec in weights + ([_resident(final_g)] if final_g is not None else []):
        args.append(w)
        specs.append(spec)
    return pl.pallas_call(
        functools.partial(_ffn_kernel, n_mix=len(mixes), final=final_g is not None),
        grid=(t // ROW_TILE,),
        in_specs=specs,
        out_specs=row(d),
        out_shape=jax.ShapeDtypeStruct((t, d), F32),
        compiler_params=_params("parallel"),
        name="ffn",
    )(*args)


def _inproj_even_kernel(x_ref, g_ref, w_ref, rcos_ref, rsin_ref, qd_ref, kd_ref,
                        dcos_ref, dsina_ref, dsinb_ref, o_ref):
    h = _rmsnorm(x_ref[...], g_ref[...]).astype(BF16)
    proj = lambda first_block, n_blocks: _mm(h, w_ref[:, _lane_block(first_block, n_blocks)])
    heads = RET_WIDTH // LANES
    half = ROT_DIM // 2

    rcos, rsin = rcos_ref[...], rsin_ref[...]
    ret_rot = lambda x: x * rcos + pltpu.roll(x, RET_DK // 2, 1) * rsin
    q = proj(0, heads)
    for j in range(heads):
        qj = ret_rot(q[:, _lane_block(j)])
        o_ref[:, _lane_block(EVEN_RQ + j)] = qj.astype(BF16)
        o_ref[:, _lane_block(EVEN_RQD + j)] = (qj * qd_ref[:, _lane_block(j)]).astype(BF16)
    k = proj(heads, heads)
    for j in range(heads):
        kj = ret_rot(k[:, _lane_block(j)]) * (RET_DK ** -0.5)
        o_ref[:, _lane_block(EVEN_RK + j)] = kj.astype(BF16)
        o_ref[:, _lane_block(EVEN_RKD + j)] = (kj * kd_ref[:, _lane_block(j)]).astype(BF16)
    o_ref[:, _lane_block(EVEN_RV, 2 * heads)] = proj(2 * heads, 2 * heads).astype(BF16)

    dcos, dsina, dsinb = dcos_ref[...], dsina_ref[...], dsinb_ref[...]
    dil_rot = lambda x: x * dcos + pltpu.roll(x, LANES - half, 1) * dsina + pltpu.roll(x, half, 1) * dsinb
    pairs = DIL_WIDTH // LANES
    dq = proj(4 * heads, pairs)
    for j in range(pairs):
        o_ref[:, _lane_block(EVEN_DQ + j)] = (dil_rot(dq[:, _lane_block(j)]) * (DIL_DH ** -0.5)).astype(BF16)
    dk = proj(4 * heads + pairs, pairs)
    for j in range(pairs):
        o_ref[:, _lane_block(EVEN_DK + j)] = dil_rot(dk[:, _lane_block(j)]).astype(BF16)
    o_ref[:, _lane_block(EVEN_DV, pairs)] = proj(4 * heads + 2 * pairs, pairs).astype(BF16)


def _even_tables(seq):
    c = RET_CHUNK
    log_g = np.log(1.0 - 2.0 ** (-5.0 - np.arange(RET_HEADS, dtype=np.float64)))
    i = np.arange(c, dtype=np.float64)
    rel = i[:, None] - i[None, :]
    dmask = np.where(rel >= 0, np.exp(log_g[:, None, None] * np.maximum(rel, 0.0)), 0.0)
    spread = lambda a: np.tile(np.repeat(a.T, RET_DK, axis=1), (ROW_TILE // c, 1))
    q_decay = spread(np.exp(log_g[:, None] * (i + 1.0)))
    k_decay = spread(np.exp(log_g[:, None] * (c - 1 - i)))
    chunk_decay = np.exp(log_g * c)[:, None, None] * np.ones((1, 1, RET_DV))
    pos = np.arange(seq, dtype=np.float32)[:, None]

    freq = np.float32(RET_THETA) ** (-np.linspace(0.0, 1.0, RET_DK // 2, dtype=np.float32))
    ang = pos * freq[None, :]
    rcos = np.concatenate([np.cos(ang), np.cos(ang)], axis=1)
    rsin = np.concatenate([-np.sin(ang), np.sin(ang)], axis=1)

    freq = np.float32(ROPE_THETA) ** (-np.arange(0, ROT_DIM, 2, dtype=np.float32) / np.float32(ROT_DIM))
    ang = pos * freq[None, :]
    cos, sin = np.cos(ang), np.sin(ang)
    zeros = np.zeros_like(sin)
    head = lambda a, b, fill: np.concatenate([a, b, np.full((seq, DIL_DH - ROT_DIM), fill, np.float32)], axis=1)
    two = lambda x: np.concatenate([x, x], axis=1)
    dcos, dsina, dsinb = two(head(cos, cos, 1.0)), two(head(-sin, zeros, 0.0)), two(head(zeros, sin, 0.0))
    f = lambda a: np.asarray(a, np.float32)
    return dict(rcos=f(rcos), rsin=f(rsin), q_decay=f(q_decay), k_decay=f(k_decay), dcos=f(dcos),
                dsina=f(dsina), dsinb=f(dsinb), dmask=f(dmask), chunk_decay=f(chunk_decay))


def _inproj_even(x, gain, weight, tables, seq):
    t, d = x.shape
    tiles_per_seq = seq // ROW_TILE
    by_position = lambda a: (a, pl.BlockSpec((ROW_TILE, LANES), lambda i: (i % tiles_per_seq, 0)))
    operands = [(x, pl.BlockSpec((ROW_TILE, d), lambda i: (i, 0))), gain, weight,
                by_position(tables["rcos"]), by_position(tables["rsin"]),
                _resident(tables["q_decay"]), _resident(tables["k_decay"]),
                by_position(tables["dcos"]), by_position(tables["dsina"]), by_position(tables["dsinb"])]
    return pl.pallas_call(
        _inproj_even_kernel,
        grid=(t // ROW_TILE,),
        in_specs=[spec for _, spec in operands],
        out_specs=pl.BlockSpec((ROW_TILE, EVEN_PREPARED), lambda i: (i, 0)),
        out_shape=jax.ShapeDtypeStruct((t, EVEN_PREPARED), BF16),
        compiler_params=_params("parallel"),
        name="inproj_even",
    )(*[a for a, _ in operands])


def _retention_kernel(q_ref, qdec_ref, k_ref, kdec_ref, v_ref, g_ref, dm_ref, cd_ref, o_ref):
    c = RET_CHUNK
    seq = q_ref.shape[0]
    dmask = dm_ref[...]
    chunk_decay = cd_ref[...]

    def body(n, state):
        rows = [_tile_rows(n * CHUNK_GROUP + g, c) for g in range(CHUNK_GROUP)]
        vs = [v_ref[r, :] for r in rows]
        scores = [_mm_nt(q_ref[r, :], k_ref[r, :]) for r in rows]
        kvs = [_mm_tn(kdec_ref[r, :], v) for r, v in zip(rows, vs)]
        for r, v, s, kv in zip(rows, vs, scores, kvs):
            lhs = jnp.concatenate([(s * dmask).astype(BF16), qdec_ref[r, :]], axis=1)
            rhs = jnp.concatenate([v, state.astype(BF16)], axis=0)
            o = _head_norm(_mm(lhs, rhs)) * _silu(g_ref[r, :].astype(F32))
            o_ref[r, :] = o.astype(BF16)
            state = state * chunk_decay + kv
        return state

    lax.fori_loop(0, seq // c // CHUNK_GROUP, body, jnp.zeros((RET_DK, RET_DV), F32))


def _retention(p, tables, batch, seq):
    head_block = lambda first: pl.BlockSpec((None, seq, RET_DK), lambda b, j: (b, 0, first + j))
    per_head = lambda a: pl.BlockSpec((None,) + a.shape[1:], lambda b, j: (j, 0, 0))
    dmask, chunk_decay = tables["dmask"], tables["chunk_decay"]
    return pl.pallas_call(
        _retention_kernel,
        grid=(batch, RET_HEADS),
        in_specs=[head_block(EVEN_RQ), head_block(EVEN_RQD), head_block(EVEN_RK), head_block(EVEN_RKD),
                  head_block(EVEN_RV), head_block(EVEN_RG), per_head(dmask), per_head(chunk_decay)],
        out_specs=pl.BlockSpec((None, seq, RET_DV), lambda b, j: (b, 0, j)),
        out_shape=jax.ShapeDtypeStruct((batch, seq, RET_HEADS * RET_DV), BF16),
        compiler_params=_params("parallel", "parallel"),
        name="retention",
    )(p, p, p, p, p, p, dmask, chunk_decay)


def _dilated_kernel(q_ref, k_ref, v_ref, o_ref, qs_ref, ks_ref, vs_ref, ob_ref, mb_ref, lb_ref):
    seq = q_ref.shape[0]
    c = DIL_BLOCK
    pad = DIL_MAX_PAD

    ks_ref[pl.ds(0, pad), :] = jnp.zeros((pad, LANES), F32)
    vs_ref[pl.ds(0, pad), :] = jnp.zeros((pad, LANES), F32)

    def stage(i, carry):
        rows = _tile_rows(i, PREP_TILE)
        dst = pl.ds(pl.multiple_of(pad + i * PREP_TILE, PREP_TILE), PREP_TILE)
        qs_ref[rows, :] = q_ref[rows, :].astype(F32)
        ks_ref[dst, :] = k_ref[rows, :].astype(F32)
        vs_ref[dst, :] = v_ref[rows, :].astype(F32)
        return carry

    lax.fori_loop(0, seq // PREP_TILE, stage, 0)

    head0 = lax.broadcasted_iota(jnp.int32, (c, LANES), 1) < DIL_DH
    qi = lax.broadcasted_iota(jnp.int32, (c, 2 * c), 0)
    kj = lax.broadcasted_iota(jnp.int32, (c, 2 * c), 1)
    dist = qi + c - kj
    band = (dist >= 0) & (dist <= c)

    def rows_of(start, n, dil):
        return pl.ds(start, n, stride=dil) if dil > 1 else pl.ds(start, n)

    def issue_scores(dil, run, starts):
        tiles = []
        for q_start in starts:
            q = qs_ref[rows_of(q_start, run * c, dil), :]
            k = ks_ref[rows_of(pad + q_start - c * dil, (run + 1) * c, dil), :].astype(BF16)
            for u in range(run):
                qu = q[u * c:(u + 1) * c]
                ku = k[u * c:(u + 2) * c]
                q_heads = (jnp.where(head0, qu, 0.0).astype(BF16), jnp.where(head0, 0.0, qu).astype(BF16))
                tiles.append([_mm_nt(qh, ku) for qh in q_heads])
        return tiles

    def finish(br, dil, run, starts, first_run, tiles):
        tiles = iter(tiles)
        band_first = band & (kj >= jnp.where(first_run, c, 0))
        for q_start in starts:
            v = vs_ref[rows_of(pad + q_start - c * dil, (run + 1) * c, dil), :]
            head0_kv = lax.broadcasted_iota(jnp.int32, v.shape, 1) < DIL_DH
            v_heads = (jnp.where(head0_kv, v, 1.0).astype(BF16), jnp.where(head0_kv, 1.0, v).astype(BF16))
            for u in range(run):
                valid = band_first if u == 0 else band
                res, mx = [], []
                for s, vh in zip(next(tiles), v_heads):
                    s = jnp.where(valid, s, -jnp.inf)
                    m = jnp.max(s, axis=-1, keepdims=True)
                    res.append(_mm(jnp.exp(s - m).astype(BF16), vh[u * c:(u + 2) * c]))
                    mx.append(jnp.broadcast_to(m, (c, LANES)))
                rows = rows_of(q_start + u * (c * dil), c, dil)
                ob_ref[br, rows, :] = jnp.where(head0, res[0], res[1])
                lb_ref[br, rows, :] = jnp.where(head0, res[1], res[0])
                mb_ref[br, rows, :] = jnp.where(head0, mx[0], mx[1])

    for br, (window, dil) in enumerate(DIL_PATTERNS):
        assert window // dil == c
        blocks = seq // dil // c
        run = min(DIL_GROUP, blocks)
        runs = blocks // run
        classes = DIL_GROUP // run

        def block_group(t, carry, br=br, dil=dil, run=run, runs=runs, classes=classes):
            jg = t % runs
            starts = [(t // runs) * classes + ci + jg * (run * c * dil) for ci in range(classes)]
            tiles = issue_scores(dil, run, starts)
            finish(br, dil, run, starts, jg == 0, tiles)
            return carry

        lax.fori_loop(0, dil * blocks // DIL_GROUP, block_group, 0)

    def combine(i, carry):
        rows = _tile_rows(i, PREP_TILE)
        n_br = len(DIL_PATTERNS)
        ms = [mb_ref[b, rows, :] for b in range(n_br)]
        m = functools.reduce(jnp.maximum, ms)
        es = [jnp.exp(x - m) for x in ms]
        num = sum(es[b] * ob_ref[b, rows, :] for b in range(n_br))
        den = sum(es[b] * pltpu.roll(lb_ref[b, rows, :], DIL_DH, 1) for b in range(n_br))
        o_ref[rows, :] = (num / den).astype(BF16)
        return carry

    lax.fori_loop(0, seq // PREP_TILE, combine, 0)


def _dilated(p, batch, seq):
    pairs = DIL_WIDTH // LANES
    pair_block = lambda first: pl.BlockSpec((None, seq, LANES), lambda b, j: (b, 0, first + j))
    n_br = len(DIL_PATTERNS)
    return pl.pallas_call(
        _dilated_kernel,
        grid=(batch, pairs),
        in_specs=[pair_block(EVEN_DQ), pair_block(EVEN_DK), pair_block(EVEN_DV)],
        out_specs=pl.BlockSpec((None, seq, LANES), lambda b, j: (b, 0, j)),
        out_shape=jax.ShapeDtypeStruct((batch, seq, DIL_WIDTH), BF16),
        scratch_shapes=[pltpu.VMEM((seq, LANES), F32),
                        pltpu.VMEM((seq + DIL_MAX_PAD, LANES), F32),
                        pltpu.VMEM((seq + DIL_MAX_PAD, LANES), F32),
                        pltpu.VMEM((n_br, seq, LANES), F32),
                        pltpu.VMEM((n_br, seq, LANES), F32),
                        pltpu.VMEM((n_br, seq, LANES), F32)],
        compiler_params=_params("parallel", "parallel"),
        name="dilated",
    )(p, p, p)


def _inproj_gla_kernel(x_ref, g_ref, w_ref, wa_ref, ba_ref, o_ref, dec_ref):
    c = GLA_CHUNK
    per_tile = ROW_TILE // c
    kw, vw = GLA_KW, GLA_VW
    h = _rmsnorm(x_ref[...], g_ref[...]).astype(BF16)
    a_low = _mm(h, w_ref[:, 2 * kw + 2 * vw:]).astype(BF16)
    z = _mm(a_low, wa_ref[...]) + ba_ref[...]
    q = _mm(h, w_ref[:, :kw]) * (GLA_DK ** -0.5)
    k = _mm(h, w_ref[:, kw:2 * kw])
    o_ref[:, 3 * kw:] = _mm(h, w_ref[:, 2 * kw:2 * kw + 2 * vw]).astype(BF16)
    b = (jnp.minimum(z, 0.0) - jnp.log1p(jnp.exp(-jnp.abs(z)))) * (1.0 / GLA_TAU)
    row_in_chunk = lax.broadcasted_iota(jnp.int32, b.shape, 0) % c
    shift = 1
    while shift < c:
        b = b + jnp.where(row_in_chunk >= shift, pltpu.roll(b, shift, 0), 0.0)
        shift *= 2
    b3 = b.reshape(per_tile, c, kw)
    b_last = b3[:, c - 1:c, :]
    dec_ref[...] = jnp.exp(b_last)
    o_ref[:, :kw] = (q * jnp.exp(b)).astype(BF16)
    o_ref[:, kw:2 * kw] = (k * jnp.exp(-b)).astype(BF16)
    k_l = k.reshape(per_tile, c, kw) * jnp.exp(b_last - b3)
    o_ref[:, 2 * kw:3 * kw] = k_l.reshape(ROW_TILE, kw).astype(BF16)


def _inproj_gla(x, gain, weight, wa, ba):
    t, d = x.shape
    per_tile = ROW_TILE // GLA_CHUNK
    operands = [(x, pl.BlockSpec((ROW_TILE, d), lambda i: (i, 0))), gain, weight, wa, ba]
    return pl.pallas_call(
        _inproj_gla_kernel,
        grid=(t // ROW_TILE,),
        in_specs=[spec for _, spec in operands],
        out_specs=[pl.BlockSpec((ROW_TILE, GLA_PREPARED), lambda i: (i, 0)),
                   pl.BlockSpec((per_tile, 1, GLA_KW), lambda i: (i, 0, 0))],
        out_shape=[jax.ShapeDtypeStruct((t, GLA_PREPARED), BF16),
                   jax.ShapeDtypeStruct((t // GLA_CHUNK, 1, GLA_KW), F32)],
        compiler_params=_params("parallel"),
        name="inproj_gla",
    )(*[a for a, _ in operands])


def _gla_kernel(qt_ref, kt_ref, kl_ref, v_ref, r_ref, dec_ref, o_ref):
    c = GLA_CHUNK
    seq = qt_ref.shape[0]
    causal = lax.broadcasted_iota(jnp.int32, (c, c), 0) >= lax.broadcasted_iota(jnp.int32, (c, c), 1)

    def body(n, state):
        chunks = [n * CHUNK_GROUP + g for g in range(CHUNK_GROUP)]
        rows = [_tile_rows(j, c) for j in chunks]
        vs = [v_ref[r, :] for r in rows]
        qts = [qt_ref[r, :] for r in rows]
        scores = [_mm_nt(q_t, kt_ref[r, :]) for r, q_t in zip(rows, qts)]
        kvs = [_mm_tn(v, kl_ref[r, :]) for r, v in zip(rows, vs)]
        for j, r, v, q_t, s, kv in zip(chunks, rows, vs, qts, scores, kvs):
            o = _mm(jnp.where(causal, s, 0.0).astype(BF16), v) + _mm_nt(q_t, state.astype(BF16))
            o = _head_norm(o) * _silu(r_ref[r, :].astype(F32))
            o_ref[r, :] = o.astype(BF16)
            state = state * dec_ref[j] + kv
        return state

    lax.fori_loop(0, seq // c // CHUNK_GROUP, body, jnp.zeros((GLA_DV, GLA_DK), F32))


def _gla(p, dec, batch, seq):
    h = GLA_HEADS
    k_block = lambda first: pl.BlockSpec((None, seq, GLA_DK), lambda b, j: (b, 0, first + j))
    v_block = lambda first: pl.BlockSpec((None, seq, GLA_DV), lambda b, j: (b, 0, first + j))
    v_first = 3 * GLA_KW // GLA_DV
    return pl.pallas_call(
        _gla_kernel,
        grid=(batch, h),
        in_specs=[k_block(0), k_block(h), k_block(2 * h), v_block(v_first), v_block(v_first + h),
                  pl.BlockSpec((None, seq // GLA_CHUNK, 1, GLA_DK), lambda b, j: (b, 0, 0, j))],
        out_specs=pl.BlockSpec((None, seq, GLA_DV), lambda b, j: (b, 0, j)),
        out_shape=jax.ShapeDtypeStruct((batch, seq, GLA_VW), BF16),
        compiler_params=_params("parallel", "parallel"),
        name="gla",
    )(p, p, p, p, p, dec)


def kernel(x, ffn_pre_norm, ffn_pre_w_gate, ffn_pre_w_up, ffn_pre_w_down, mix_norm, ab_w_in, ab_w_out,
           gla_w_in, gla_w_a2, gla_b_a, gla_w_out, ffn_post_norm, ffn_post_w_gate, ffn_post_w_up,
           ffn_post_w_down, final_norm):
    batch, seq, d = x.shape
    depth = ffn_pre_norm.shape[0]
    bf = lambda w: w.astype(BF16)
    gains = lambda g: g.reshape(g.shape[0], 1, d)
    assert RET_HEADS * RET_DV == DIL_WIDTH

    pre = [gains(ffn_pre_norm), bf(ffn_pre_w_gate), bf(ffn_pre_w_up), bf(ffn_pre_w_down)]
    post = [gains(ffn_post_norm), bf(ffn_post_w_gate), bf(ffn_post_w_up), bf(ffn_post_w_down)]
    mix_g = gains(mix_norm)
    ab_in, ab_out = bf(ab_w_in), bf(ab_w_out)
    gla_in = jnp.pad(bf(gla_w_in), ((0, 0), (0, 0), (0, GLA_IN_PADDED - gla_w_in.shape[2])))
    gla_a2 = jnp.pad(bf(gla_w_a2), ((0, 0), (0, LANES - gla_w_a2.shape[1]), (0, 0)))
    gla_ba = gla_b_a.reshape(gla_b_a.shape[0], 1, -1)
    gla_out = bf(gla_w_out)
    tables = _even_tables(seq)

    xt = x.reshape(batch * seq, d)
    for l in range(depth):
        xt = _ffn(xt, [], [_layer(w, l) for w in pre])
        if l % 2 == 0:
            p = _inproj_even(xt, _layer(mix_g, l), _layer(ab_in, l // 2), tables, seq)
            p = p.reshape(batch, seq, EVEN_PREPARED)
            o_r = _retention(p, tables, batch, seq).reshape(batch * seq, DIL_WIDTH)
            o_d = _dilated(p, batch, seq).reshape(batch * seq, DIL_WIDTH)
            mixes = [(o_r, _layer(ab_out, l // 2, rows=DIL_WIDTH, row_block=0)),
                     (o_d, _layer(ab_out, l // 2, rows=DIL_WIDTH, row_block=1))]
        else:
            p, dec = _inproj_gla(xt, _layer(mix_g, l), _layer(gla_in, l // 2), _layer(gla_a2, l // 2),
                                 _layer(gla_ba, l // 2))
            o = _gla(p.reshape(batch, seq, GLA_PREPARED),
                     dec.reshape(batch, seq // GLA_CHUNK, 1, GLA_KW), batch, seq)
            mixes = [(o.reshape(batch * seq, GLA_VW), _layer(gla_out, l // 2))]
        xt = _ffn(xt, mixes, [_layer(w, l) for w in post],
                  final_g=final_norm.reshape(1, d) if l == depth - 1 else None)
    return xt.reshape(batch, seq, d)
```

```python
import functools

import jax
import jax.numpy as jnp
import numpy as np
from jax import lax
from jax.experimental import pallas as pl
from jax.experimental.pallas import tpu as pltpu

F32 = jnp.float32
BF16 = jnp.bfloat16

LANES = 128
VMEM_LIMIT_BYTES = 56 * 1024 * 1024

D_MODEL = 1024
FFN_HIDDEN = 2816
EPS = 1e-6

RET_HEADS = 4
RET_DK = 128
RET_DV = 128
RET_CHUNK = 128
RET_THETA = 10000.0
RET_WIDTH = RET_HEADS * RET_DK

DIL_HEADS = 8
DIL_DH = 64
DIL_PATTERNS = ((128, 1), (512, 4), (2048, 16))
DIL_BLOCK = 128
DIL_GROUP = 16
ROPE_THETA = 500000.0
ROT_DIM = DIL_DH // 4
DIL_PAD = DIL_BLOCK
DIL_WIDTH = DIL_HEADS * DIL_DH

GLA_HEADS = 4
GLA_DK = (D_MODEL // 2) // GLA_HEADS
GLA_DV = D_MODEL // GLA_HEADS
GLA_LOWRANK = 16
GLA_TAU = 16.0
GLA_CHUNK = 64
GLA_KW = GLA_HEADS * GLA_DK
GLA_VW = GLA_HEADS * GLA_DV

CHUNK_GROUP = 16

EVEN_RQ, EVEN_RQD, EVEN_RK, EVEN_RKD, EVEN_RV, EVEN_RG, EVEN_DQ, EVEN_DK, EVEN_DV = range(0, 36, 4)
EVEN_PREPARED = 9 * RET_WIDTH
GLA_PREPARED = 3 * GLA_KW + 2 * GLA_VW
GLA_IN_PADDED = 2 * GLA_KW + 2 * GLA_VW + LANES

ROW_TILE = 512
PREP_TILE = 512

NT_DIMS = (((1,), (1,)), ((), ()))
TN_DIMS = (((0,), (0,)), ((), ()))


def _params(*semantics):
    return pltpu.CompilerParams(dimension_semantics=semantics,
                                vmem_limit_bytes=VMEM_LIMIT_BYTES)


def _resident(a):
    zeros = (0,) * a.ndim
    return a, pl.BlockSpec(a.shape, lambda *_: zeros, pipeline_mode=pl.Buffered(1))


def _layer(w, l, rows=None, row_block=0):
    shape = (None, w.shape[1] if rows is None else rows, w.shape[2])
    return w, pl.BlockSpec(shape, lambda *_: (l, row_block, 0), pipeline_mode=pl.Buffered(1))


def _rmsnorm(x, g):
    return x * lax.rsqrt(jnp.mean(x * x, axis=-1, keepdims=True) + EPS) * g


def _norm_parts(x, g):
    return (x * g).astype(BF16), lax.rsqrt(jnp.mean(x * x, axis=-1, keepdims=True) + EPS)


def _head_norm(o):
    mu = jnp.mean(o, axis=-1, keepdims=True)
    c = o - mu
    return c * lax.rsqrt(jnp.mean(c * c, axis=-1, keepdims=True) + EPS)


def _silu(x):
    return x * jax.nn.sigmoid(x)


def _mm(a, b):
    return jnp.dot(a, b, preferred_element_type=F32)


def _mm_nt(a, b):
    return lax.dot_general(a, b, NT_DIMS, preferred_element_type=F32)


def _mm_tn(a, b):
    return lax.dot_general(a, b, TN_DIMS, preferred_element_type=F32)


def _tile_rows(i, n):
    return pl.ds(pl.multiple_of(i * n, n), n)


def _lane_block(j, n=1):
    return slice(j * LANES, (j + n) * LANES)


def _ffn_kernel(*refs, n_mix, final):
    x_ref = refs[0]
    mix_refs = refs[1:1 + 2 * n_mix]
    g_ref, wg_ref, wu_ref, wd_ref = refs[1 + 2 * n_mix:5 + 2 * n_mix]
    o_ref = refs[-1]
    x = x_ref[...]
    for i in range(n_mix):
        x = x + _mm(mix_refs[2 * i][...], mix_refs[2 * i + 1][...])
    xg, r = _norm_parts(x, g_ref[...])
    act = (_silu(_mm(xg, wg_ref[...]) * r) * (_mm(xg, wu_ref[...]) * r)).astype(BF16)
    y = x + 0.5 * _mm(act, wd_ref[...])
    if final:
        y = _rmsnorm(y, refs[-2][...])
    o_ref[...] = y


def _ffn(x, mixes, weights, final_g=None):
    t, d = x.shape
    row = lambda w: pl.BlockSpec((ROW_TILE, w), lambda i: (i, 0))
    args, specs = [x], [row(d)]
    for m, (w, spec) in mixes:
        args += [m, w]
        specs += [row(m.shape[1]), spec]
    for w, spec in weights + ([_resident(final_g)] if final_g is not None else []):
        args.append(w)
        specs.append(spec)
    return pl.pallas_call(
        functools.partial(_ffn_kernel, n_mix=len(mixes), final=final_g is not None),
        grid=(t // ROW_TILE,),
        in_specs=specs,
        out_specs=row(d),
        out_shape=jax.ShapeDtypeStruct((t, d), F32),
        compiler_params=_params("parallel"),
        name="ffn",
    )(*args)


def _inproj_even_kernel(x_ref, g_ref, w_ref, rcos_ref, rsin_ref, qd_ref, kd_ref,
                        dcos_ref, dsina_ref, dsinb_ref, o_ref):
    xg, r = _norm_parts(x_ref[...], g_ref[...])
    proj = lambda first_block, n_blocks: _mm(xg, w_ref[:, _lane_block(first_block, n_blocks)]) * r
    heads = RET_WIDTH // LANES
    half = ROT_DIM // 2

    rcos, rsin = rcos_ref[...], rsin_ref[...]
    ret_rot = lambda x: x * rcos + pltpu.roll(x, RET_DK // 2, 1) * rsin
    q = proj(0, heads)
    for j in range(heads):
        qj = ret_rot(q[:, _lane_block(j)])
        o_ref[:, _lane_block(EVEN_RQ + j)] = qj.astype(BF16)
        o_ref[:, _lane_block(EVEN_RQD + j)] = (qj * qd_ref[:, _lane_block(j)]).astype(BF16)
    k = proj(heads, heads)
    for j in range(heads):
        kj = ret_rot(k[:, _lane_block(j)]) * (RET_DK ** -0.5)
        o_ref[:, _lane_block(EVEN_RK + j)] = kj.astype(BF16)
        o_ref[:, _lane_block(EVEN_RKD + j)] = (kj * kd_ref[:, _lane_block(j)]).astype(BF16)
    o_ref[:, _lane_block(EVEN_RV, 2 * heads)] = proj(2 * heads, 2 * heads).astype(BF16)

    dcos, dsina, dsinb = dcos_ref[...], dsina_ref[...], dsinb_ref[...]
    dil_rot = lambda x: x * dcos + pltpu.roll(x, LANES - half, 1) * dsina + pltpu.roll(x, half, 1) * dsinb
    pairs = DIL_WIDTH // LANES
    dq = proj(4 * heads, pairs)
    for j in range(pairs):
        o_ref[:, _lane_block(EVEN_DQ + j)] = (dil_rot(dq[:, _lane_block(j)]) * (DIL_DH ** -0.5)).astype(BF16)
    dk = proj(4 * heads + pairs, pairs)
    for j in range(pairs):
        o_ref[:, _lane_block(EVEN_DK + j)] = dil_rot(dk[:, _lane_block(j)]).astype(BF16)
    o_ref[:, _lane_block(EVEN_DV, pairs)] = proj(4 * heads + 2 * pairs, pairs).astype(BF16)


def _even_tables(seq):
    c = RET_CHUNK
    log_g = np.log(1.0 - 2.0 ** (-5.0 - np.arange(RET_HEADS, dtype=np.float64)))
    i = np.arange(c, dtype=np.float64)
    rel = i[:, None] - i[None, :]
    dmask = np.where(rel >= 0, np.exp(log_g[:, None, None] * np.maximum(rel, 0.0)), 0.0)
    spread = lambda a: np.tile(np.repeat(a.T, RET_DK, axis=1), (ROW_TILE // c, 1))
    q_decay = spread(np.exp(log_g[:, None] * (i + 1.0)))
    k_decay = spread(np.exp(log_g[:, None] * (c - 1 - i)))
    chunk_decay = np.exp(log_g * c)[:, None, None] * np.ones((1, 1, RET_DV))
    pos = np.arange(seq, dtype=np.float32)[:, None]

    freq = np.float32(RET_THETA) ** (-np.linspace(0.0, 1.0, RET_DK // 2, dtype=np.float32))
    ang = pos * freq[None, :]
    rcos = np.concatenate([np.cos(ang), np.cos(ang)], axis=1)
    rsin = np.concatenate([-np.sin(ang), np.sin(ang)], axis=1)

    freq = np.float32(ROPE_THETA) ** (-np.arange(0, ROT_DIM, 2, dtype=np.float32) / np.float32(ROT_DIM))
    ang = pos * freq[None, :]
    cos, sin = np.cos(ang), np.sin(ang)
    zeros = np.zeros_like(sin)
    head = lambda a, b, fill: np.concatenate([a, b, np.full((seq, DIL_DH - ROT_DIM), fill, np.float32)], axis=1)
    two = lambda x: np.concatenate([x, x], axis=1)
    dcos, dsina, dsinb = two(head(cos, cos, 1.0)), two(head(-sin, zeros, 0.0)), two(head(zeros, sin, 0.0))
    f = lambda a: np.asarray(a, np.float32)
    return dict(rcos=f(rcos), rsin=f(rsin), q_decay=f(q_decay), k_decay=f(k_decay), dcos=f(dcos),
                dsina=f(dsina), dsinb=f(dsinb), dmask=f(dmask), chunk_decay=f(chunk_decay))


def _inproj_even(x, gain, weight, tables, seq):
    t, d = x.shape
    tiles_per_seq = seq // ROW_TILE
    by_position = lambda a: (a, pl.BlockSpec((ROW_TILE, LANES), lambda i: (i % tiles_per_seq, 0)))
    operands = [(x, pl.BlockSpec((ROW_TILE, d), lambda i: (i, 0))), gain, weight,
                by_position(tables["rcos"]), by_position(tables["rsin"]),
                _resident(tables["q_decay"]), _resident(tables["k_decay"]),
                by_position(tables["dcos"]), by_position(tables["dsina"]), by_position(tables["dsinb"])]
    return pl.pallas_call(
        _inproj_even_kernel,
        grid=(t // ROW_TILE,),
        in_specs=[spec for _, spec in operands],
        out_specs=pl.BlockSpec((ROW_TILE, EVEN_PREPARED), lambda i: (i, 0)),
        out_shape=jax.ShapeDtypeStruct((t, EVEN_PREPARED), BF16),
        compiler_params=_params("parallel"),
        name="inproj_even",
    )(*[a for a, _ in operands])


def _retention_kernel(q_ref, qdec_ref, k_ref, kdec_ref, v_ref, g_ref, dm_ref, cd_ref, o_ref):
    c = RET_CHUNK
    seq = q_ref.shape[0]
    dmask = dm_ref[...]
    chunk_decay = cd_ref[...]

    def body(n, state):
        rows = [_tile_rows(n * CHUNK_GROUP + g, c) for g in range(CHUNK_GROUP)]
        vs = [v_ref[r, :] for r in rows]
        scores = [_mm_nt(q_ref[r, :], k_ref[r, :]) for r in rows]
        kvs = [_mm_tn(kdec_ref[r, :], v) for r, v in zip(rows, vs)]
        for r, v, s, kv in zip(rows, vs, scores, kvs):
            lhs = jnp.concatenate([(s * dmask).astype(BF16), qdec_ref[r, :]], axis=1)
            rhs = jnp.concatenate([v, state.astype(BF16)], axis=0)
            o = _head_norm(_mm(lhs, rhs)) * _silu(g_ref[r, :].astype(F32))
            o_ref[r, :] = o.astype(BF16)
            state = state * chunk_decay + kv
        return state

    lax.fori_loop(0, seq // c // CHUNK_GROUP, body, jnp.zeros((RET_DK, RET_DV), F32))


def _retention(p, tables, batch, seq):
    head_block = lambda first: pl.BlockSpec((None, seq, RET_DK), lambda b, j: (b, 0, first + j))
    per_head = lambda a: pl.BlockSpec((None,) + a.shape[1:], lambda b, j: (j, 0, 0))
    dmask, chunk_decay = tables["dmask"], tables["chunk_decay"]
    return pl.pallas_call(
        _retention_kernel,
        grid=(batch, RET_HEADS),
        in_specs=[head_block(EVEN_RQ), head_block(EVEN_RQD), head_block(EVEN_RK), head_block(EVEN_RKD),
                  head_block(EVEN_RV), head_block(EVEN_RG), per_head(dmask), per_head(chunk_decay)],
        out_specs=pl.BlockSpec((None, seq, RET_DV), lambda b, j: (b, 0, j)),
        out_shape=jax.ShapeDtypeStruct((batch, seq, RET_HEADS * RET_DV), BF16),
        compiler_params=_params("parallel", "parallel"),
        name="retention",
    )(p, p, p, p, p, p, dmask, chunk_decay)


def _dilated_kernel(q_ref, k_ref, v_ref, o_ref, qs_ref, ks_ref, vs_ref, ob_ref, mb_ref, lb_ref):
    seq = q_ref.shape[0]
    c = DIL_BLOCK
    pad = DIL_PAD

    ks_ref[pl.ds(0, pad), :] = jnp.zeros((pad, LANES), F32)
    vs_ref[pl.ds(0, pad), :] = jnp.zeros((pad, LANES), F32)

    def stage(i, carry):
        rows = _tile_rows(i, PREP_TILE)
        dst = pl.ds(pl.multiple_of(pad + i * PREP_TILE, DIL_BLOCK), PREP_TILE)
        qs_ref[rows, :] = q_ref[rows, :].astype(F32)
        ks_ref[dst, :] = k_ref[rows, :].astype(F32)
        vs_ref[dst, :] = v_ref[rows, :].astype(F32)
        return carry

    lax.fori_loop(0, seq // PREP_TILE, stage, 0)

    head0 = lax.broadcasted_iota(jnp.int32, (c, LANES), 1) < DIL_DH
    qi = lax.broadcasted_iota(jnp.int32, (c, 2 * c), 0)
    kj = lax.broadcasted_iota(jnp.int32, (c, 2 * c), 1)
    dist = qi + c - kj
    band = (dist >= 0) & (dist <= c)
    causal = lax.broadcasted_iota(jnp.int32, (c, c), 0) >= lax.broadcasted_iota(jnp.int32, (c, c), 1)

    def rows_of(start, n, dil):
        return pl.ds(start, n, stride=dil) if dil > 1 else pl.ds(start, n)

    plans = []
    for br, (window, dil) in enumerate(DIL_PATTERNS):
        assert window // dil == c
        blocks = seq // dil // c
        run = min(DIL_GROUP, blocks)
        assert blocks == run or c * dil <= pad
        plans.append((br, dil, run, blocks // run, DIL_GROUP // run, dil * blocks // DIL_GROUP))

    def geometry(plan, t):
        _, dil, run, runs, classes, _ = plan
        jg = t % runs
        return [(t // runs) * classes + ci + jg * (run * c * dil) for ci in range(classes)], jg == 0

    def key_rows(plan, q_start):
        _, dil, run, runs, _, _ = plan
        lead = 0 if runs == 1 else 1
        return rows_of(pad + q_start - lead * c * dil, (run + lead) * c, dil), lead

    def tile_keys(strip, u, lead):
        first = u + lead - 1
        return strip[max(first, 0) * c:(first + 2) * c]

    def issue_scores(plan, t):
        _, dil, run, _, _, _ = plan
        starts, _ = geometry(plan, t)
        tiles = []
        for q_start in starts:
            q = qs_ref[rows_of(q_start, run * c, dil), :]
            rows, lead = key_rows(plan, q_start)
            k = ks_ref[rows, :].astype(BF16)
            for u in range(run):
                qu = q[u * c:(u + 1) * c]
                ku = tile_keys(k, u, lead)
                for qh in (jnp.where(head0, qu, 0.0).astype(BF16), jnp.where(head0, 0.0, qu).astype(BF16)):
                    tiles.append(_mm_nt(qh, ku))
        return tiles

    def finish(plan, t, tiles):
        br, dil, run, _, _, _ = plan
        starts, first_run = geometry(plan, t)
        band_first = band & (kj >= jnp.where(first_run, c, 0))
        tiles = iter(tiles)
        for q_start in starts:
            rows, lead = key_rows(plan, q_start)
            v = vs_ref[rows, :]
            head0_kv = lax.broadcasted_iota(jnp.int32, v.shape, 1) < DIL_DH
            v_heads = (jnp.where(head0_kv, v, 1.0).astype(BF16), jnp.where(head0_kv, 1.0, v).astype(BF16))
            for u in range(run):
                valid = band if u > 0 else (band_first if lead else causal)
                res, mx = [], []
                for vh in v_heads:
                    s = jnp.where(valid, next(tiles), -jnp.inf)
                    m = jnp.max(s, axis=-1, keepdims=True)
                    res.append(_mm(jnp.exp(s - m).astype(BF16), tile_keys(vh, u, lead)))
                    mx.append(jnp.broadcast_to(m, (c, LANES)))
                rows = rows_of(q_start + u * (c * dil), c, dil)
                ob_ref[br, rows, :] = jnp.where(head0, res[0], res[1])
                lb_ref[br, rows, :] = jnp.where(head0, res[1], res[0])
                mb_ref[br, rows, :] = jnp.where(head0, mx[0], mx[1])

    for plan in plans:
        def step(t, carry, plan=plan):
            finish(plan, t, issue_scores(plan, t))
            return carry

        lax.fori_loop(0, plan[-1], step, 0)


    def combine(i, carry):
        rows = _tile_rows(i, PREP_TILE)
        n_br = len(DIL_PATTERNS)
        ms = [mb_ref[b, rows, :] for b in range(n_br)]
        m = functools.reduce(jnp.maximum, ms)
        es = [jnp.exp(x - m) for x in ms]
        num = sum(es[b] * ob_ref[b, rows, :] for b in range(n_br))
        den = sum(es[b] * pltpu.roll(lb_ref[b, rows, :], DIL_DH, 1) for b in range(n_br))
        o_ref[rows, :] = (num / den).astype(BF16)
        return carry

    lax.fori_loop(0, seq // PREP_TILE, combine, 0)


def _dilated(p, batch, seq):
    pairs = DIL_WIDTH // LANES
    pair_block = lambda first: pl.BlockSpec((None, seq, LANES), lambda b, j: (b, 0, first + j))
    n_br = len(DIL_PATTERNS)
    return pl.pallas_call(
        _dilated_kernel,
        grid=(batch, pairs),
        in_specs=[pair_block(EVEN_DQ), pair_block(EVEN_DK), pair_block(EVEN_DV)],
        out_specs=pl.BlockSpec((None, seq, LANES), lambda b, j: (b, 0, j)),
        out_shape=jax.ShapeDtypeStruct((batch, seq, DIL_WIDTH), BF16),
        scratch_shapes=[pltpu.VMEM((seq, LANES), F32),
                        pltpu.VMEM((seq + DIL_PAD, LANES), F32),
                        pltpu.VMEM((seq + DIL_PAD, LANES), F32),
                        pltpu.VMEM((n_br, seq, LANES), F32),
                        pltpu.VMEM((n_br, seq, LANES), F32),
                        pltpu.VMEM((n_br, seq, LANES), F32)],
        compiler_params=_params("parallel", "parallel"),
        name="dilated",
    )(p, p, p)


def _inproj_gla_kernel(x_ref, g_ref, w_ref, wa_ref, ba_ref, o_ref, dec_ref):
    c = GLA_CHUNK
    per_tile = ROW_TILE // c
    kw, vw = GLA_KW, GLA_VW
    xg, r = _norm_parts(x_ref[...], g_ref[...])
    proj = lambda lo, hi: _mm(xg, w_ref[:, lo:hi]) * r
    a_low = proj(2 * kw + 2 * vw, GLA_IN_PADDED).astype(BF16)
    z = _mm(a_low, wa_ref[...]) + ba_ref[...]
    q = proj(0, kw) * (GLA_DK ** -0.5)
    k = proj(kw, 2 * kw)
    o_ref[:, 3 * kw:] = proj(2 * kw, 2 * kw + 2 * vw).astype(BF16)
    b = (jnp.minimum(z, 0.0) - jnp.log1p(jnp.exp(-jnp.abs(z)))) * (1.0 / GLA_TAU)
    row_in_chunk = lax.broadcasted_iota(jnp.int32, b.shape, 0) % c
    shift = 1
    while shift < c:
        b = b + jnp.where(row_in_chunk >= shift, pltpu.roll(b, shift, 0), 0.0)
        shift *= 2
    b3 = b.reshape(per_tile, c, kw)
    b_last = b3[:, c - 1:c, :]
    dec_ref[...] = jnp.exp(b_last)
    o_ref[:, :kw] = (q * jnp.exp(b)).astype(BF16)
    o_ref[:, kw:2 * kw] = (k * jnp.exp(-b)).astype(BF16)
    k_l = k.reshape(per_tile, c, kw) * jnp.exp(b_last - b3)
    o_ref[:, 2 * kw:3 * kw] = k_l.reshape(ROW_TILE, kw).astype(BF16)


def _inproj_gla(x, gain, weight, wa, ba):
    t, d = x.shape
    per_tile = ROW_TILE // GLA_CHUNK
    operands = [(x, pl.BlockSpec((ROW_TILE, d), lambda i: (i, 0))), gain, weight, wa, ba]
    return pl.pallas_call(
        _inproj_gla_kernel,
        grid=(t // ROW_TILE,),
        in_specs=[spec for _, spec in operands],
        out_specs=[pl.BlockSpec((ROW_TILE, GLA_PREPARED), lambda i: (i, 0)),
                   pl.BlockSpec((per_tile, 1, GLA_KW), lambda i: (i, 0, 0))],
        out_shape=[jax.ShapeDtypeStruct((t, GLA_PREPARED), BF16),
                   jax.ShapeDtypeStruct((t // GLA_CHUNK, 1, GLA_KW), F32)],
        compiler_params=_params("parallel"),
        name="inproj_gla",
    )(*[a for a, _ in operands])


def _gla_kernel(qt_ref, kt_ref, kl_ref, v_ref, r_ref, dec_ref, o_ref):
    c = GLA_CHUNK
    seq = qt_ref.shape[0]
    causal = lax.broadcasted_iota(jnp.int32, (c, c), 0) >= lax.broadcasted_iota(jnp.int32, (c, c), 1)

    def body(n, state):
        chunks = [n * CHUNK_GROUP + g for g in range(CHUNK_GROUP)]
        rows = [_tile_rows(j, c) for j in chunks]
        vs = [v_ref[r, :] for r in rows]
        qts = [qt_ref[r, :] for r in rows]
        scores = [_mm_nt(q_t, kt_ref[r, :]) for r, q_t in zip(rows, qts)]
        kvs = [_mm_tn(v, kl_ref[r, :]) for r, v in zip(rows, vs)]
        for j, r, v, q_t, s, kv in zip(chunks, rows, vs, qts, scores, kvs):
            o = _mm(jnp.where(causal, s, 0.0).astype(BF16), v) + _mm_nt(q_t, state.astype(BF16))
            o = _head_norm(o) * _silu(r_ref[r, :].astype(F32))
            o_ref[r, :] = o.astype(BF16)
            state = state * dec_ref[j] + kv
        return state

    lax.fori_loop(0, seq // c // CHUNK_GROUP, body, jnp.zeros((GLA_DV, GLA_DK), F32))


def _gla(p, dec, batch, seq):
    h = GLA_HEADS
    k_block = lambda first: pl.BlockSpec((None, seq, GLA_DK), lambda b, j: (b, 0, first + j))
    v_block = lambda first: pl.BlockSpec((None, seq, GLA_DV), lambda b, j: (b, 0, first + j))
    v_first = 3 * GLA_KW // GLA_DV
    return pl.pallas_call(
        _gla_kernel,
        grid=(batch, h),
        in_specs=[k_block(0), k_block(h), k_block(2 * h), v_block(v_first), v_block(v_first + h),
                  pl.BlockSpec((None, seq // GLA_CHUNK, 1, GLA_DK), lambda b, j: (b, 0, 0, j))],
        out_specs=pl.BlockSpec((None, seq, GLA_DV), lambda b, j: (b, 0, j)),
        out_shape=jax.ShapeDtypeStruct((batch, seq, GLA_VW), BF16),
        compiler_params=_params("parallel", "parallel"),
        name="gla",
    )(p, p, p, p, p, dec)


def kernel(x, ffn_pre_norm, ffn_pre_w_gate, ffn_pre_w_up, ffn_pre_w_down, mix_norm, ab_w_in, ab_w_out,
           gla_w_in, gla_w_a2, gla_b_a, gla_w_out, ffn_post_norm, ffn_post_w_gate, ffn_post_w_up,
           ffn_post_w_down, final_norm):
    batch, seq, d = x.shape
    depth = ffn_pre_norm.shape[0]
    bf = lambda w: w.astype(BF16)
    gains = lambda g: g.reshape(g.shape[0], 1, d)
    assert RET_HEADS * RET_DV == DIL_WIDTH

    pre = [gains(ffn_pre_norm), bf(ffn_pre_w_gate), bf(ffn_pre_w_up), bf(ffn_pre_w_down)]
    post = [gains(ffn_post_norm), bf(ffn_post_w_gate), bf(ffn_post_w_up), bf(ffn_post_w_down)]
    mix_g = gains(mix_norm)
    ab_in, ab_out = bf(ab_w_in), bf(ab_w_out)
    gla_in = jnp.pad(bf(gla_w_in), ((0, 0), (0, 0), (0, GLA_IN_PADDED - gla_w_in.shape[2])))
    gla_a2 = jnp.pad(bf(gla_w_a2), ((0, 0), (0, LANES - gla_w_a2.shape[1]), (0, 0)))
    gla_ba = gla_b_a.reshape(gla_b_a.shape[0], 1, -1)
    gla_out = bf(gla_w_out)
    tables = _even_tables(seq)

    xt = x.reshape(batch * seq, d)
    for l in range(depth):
        xt = _ffn(xt, [], [_layer(w, l) for w in pre])
        if l % 2 == 0:
            p = _inproj_even(xt, _layer(mix_g, l), _layer(ab_in, l // 2), tables, seq)
            p = p.reshape(batch, seq, EVEN_PREPARED)
            o_r = _retention(p, tables, batch, seq).reshape(batch * seq, DIL_WIDTH)
            o_d = _dilated(p, batch, seq).reshape(batch * seq, DIL_WIDTH)
            mixes = [(o_r, _layer(ab_out, l // 2, rows=DIL_WIDTH, row_block=0)),
                     (o_d, _layer(ab_out, l // 2, rows=DIL_WIDTH, row_block=1))]
        else:
            p, dec = _inproj_gla(xt, _layer(mix_g, l), _layer(gla_in, l // 2), _layer(gla_a2, l // 2),
                                 _layer(gla_ba, l // 2))
            o = _gla(p.reshape(batch, seq, GLA_PREPARED),
                     dec.reshape(batch, seq // GLA_CHUNK, 1, GLA_KW), batch, seq)
            mixes = [(o.reshape(batch * seq, GLA_VW), _layer(gla_out, l // 2))]
        xt = _ffn(xt, mixes, [_layer(w, l) for w in post],
                  final_g=final_norm.reshape(1, d) if l == depth - 1 else None)
    return xt.reshape(batch, seq, d)
```

```python
import functools

import jax
import jax.numpy as jnp
import numpy as np
from jax import lax
from jax.experimental import pallas as pl
from jax.experimental.pallas import tpu as pltpu

F32 = jnp.float32
BF16 = jnp.bfloat16

LANES = 128
VMEM_LIMIT_BYTES = 56 * 1024 * 1024

D_MODEL = 1024
FFN_HIDDEN = 2816
EPS = 1e-6

RET_HEADS = 4
RET_DK = 128
RET_DV = 128
RET_CHUNK = 128
RET_THETA = 10000.0
RET_WIDTH = RET_HEADS * RET_DK

DIL_HEADS = 8
DIL_DH = 64
DIL_PATTERNS = ((128, 1), (512, 4), (2048, 16))
DIL_BLOCK = 128
DIL_GROUP = 16
ROPE_THETA = 500000.0
ROT_DIM = DIL_DH // 4
DIL_PAD = DIL_BLOCK
DIL_WIDTH = DIL_HEADS * DIL_DH

GLA_HEADS = 4
GLA_DK = (D_MODEL // 2) // GLA_HEADS
GLA_DV = D_MODEL // GLA_HEADS
GLA_LOWRANK = 16
GLA_TAU = 16.0
GLA_CHUNK = 64
GLA_KW = GLA_HEADS * GLA_DK
GLA_VW = GLA_HEADS * GLA_DV

CHUNK_GROUP = 16

EVEN_RQ, EVEN_RQD, EVEN_RK, EVEN_RKD, EVEN_RV, EVEN_RG, EVEN_DQ, EVEN_DK, EVEN_DV = range(0, 36, 4)
EVEN_SLABS = 9 * RET_WIDTH // LANES
GLA_IN_PADDED = 2 * GLA_KW + 2 * GLA_VW + LANES

ROW_TILE = 512
PREP_TILE = 512

NT_DIMS = (((1,), (1,)), ((), ()))
TN_DIMS = (((0,), (0,)), ((), ()))


def _params(*semantics):
    return pltpu.CompilerParams(dimension_semantics=semantics,
                                vmem_limit_bytes=VMEM_LIMIT_BYTES)


def _resident(a):
    zeros = (0,) * a.ndim
    return a, pl.BlockSpec(a.shape, lambda *_: zeros, pipeline_mode=pl.Buffered(1))


def _layer(w, l, rows=None, row_block=0):
    shape = (None, w.shape[1] if rows is None else rows, w.shape[2])
    return w, pl.BlockSpec(shape, lambda *_: (l, row_block, 0), pipeline_mode=pl.Buffered(1))


def _rmsnorm(x, g):
    return x * lax.rsqrt(jnp.mean(x * x, axis=-1, keepdims=True) + EPS) * g


def _norm_parts(x, g):
    return (x * g).astype(BF16), lax.rsqrt(jnp.mean(x * x, axis=-1, keepdims=True) + EPS)


def _head_norm(o):
    mu = jnp.mean(o, axis=-1, keepdims=True)
    c = o - mu
    return c * lax.rsqrt(jnp.mean(c * c, axis=-1, keepdims=True) + EPS)


def _silu(x):
    return x * jax.nn.sigmoid(x)


def _mm(a, b):
    return jnp.dot(a, b, preferred_element_type=F32)


def _mm_nt(a, b):
    return lax.dot_general(a, b, NT_DIMS, preferred_element_type=F32)


def _mm_tn(a, b):
    return lax.dot_general(a, b, TN_DIMS, preferred_element_type=F32)


def _tile_rows(i, n):
    return pl.ds(pl.multiple_of(i * n, n), n)


def _lane_block(j, n=1):
    return slice(j * LANES, (j + n) * LANES)


def _ffn_kernel(*refs, n_mix, final):
    x_ref = refs[0]
    mix_refs = refs[1:1 + 2 * n_mix]
    g_ref, wg_ref, wu_ref, wd_ref = refs[1 + 2 * n_mix:5 + 2 * n_mix]
    o_ref = refs[-1]
    x = x_ref[...]
    for i in range(n_mix):
        mix_ref = mix_refs[2 * i]
        mix = jnp.concatenate([mix_ref[g] for g in range(mix_ref.shape[0])], axis=1)
        x = x + _mm(mix, mix_refs[2 * i + 1][...])
    xg, r = _norm_parts(x, g_ref[...])
    act = (_silu(_mm(xg, wg_ref[...]) * r) * (_mm(xg, wu_ref[...]) * r)).astype(BF16)
    y = x + 0.5 * _mm(act, wd_ref[...])
    if final:
        y = _rmsnorm(y, refs[-2][...])
    o_ref[...] = y


def _ffn(x, mixes, weights, final_g=None):
    t, d = x.shape
    row = lambda w: pl.BlockSpec((ROW_TILE, w), lambda i: (i, 0))
    args, specs = [x], [row(d)]
    for m, (w, spec) in mixes:
        args += [m, w]
        specs += [pl.BlockSpec((m.shape[0], ROW_TILE, m.shape[2]), lambda i: (0, i, 0)), spec]
    for w, spec in weights + ([_resident(final_g)] if final_g is not None else []):
        args.append(w)
        specs.append(spec)
    return pl.pallas_call(
        functools.partial(_ffn_kernel, n_mix=len(mixes), final=final_g is not None),
        grid=(t // ROW_TILE,),
        in_specs=specs,
        out_specs=row(d),
        out_shape=jax.ShapeDtypeStruct((t, d), F32),
        compiler_params=_params("parallel"),
        name="ffn",
    )(*args)


def _inproj_even_kernel(x_ref, g_ref, w_ref, rcos_ref, rsin_ref, qd_ref, kd_ref,
                        dcos_ref, dsina_ref, dsinb_ref, o_ref):
    xg, r = _norm_parts(x_ref[...], g_ref[...])
    proj = lambda first_block, n_blocks: _mm(xg, w_ref[:, _lane_block(first_block, n_blocks)]) * r
    heads = RET_WIDTH // LANES
    half = ROT_DIM // 2

    rcos, rsin = rcos_ref[...], rsin_ref[...]
    ret_rot = lambda x: x * rcos + pltpu.roll(x, RET_DK // 2, 1) * rsin
    q = proj(0, heads)
    for j in range(heads):
        qj = ret_rot(q[:, _lane_block(j)])
        o_ref[EVEN_RQ + j] = qj.astype(BF16)
        o_ref[EVEN_RQD + j] = (qj * qd_ref[:, _lane_block(j)]).astype(BF16)
    k = proj(heads, heads)
    for j in range(heads):
        kj = ret_rot(k[:, _lane_block(j)]) * (RET_DK ** -0.5)
        o_ref[EVEN_RK + j] = kj.astype(BF16)
        o_ref[EVEN_RKD + j] = (kj * kd_ref[:, _lane_block(j)]).astype(BF16)
    vg = proj(2 * heads, 2 * heads)
    for j in range(2 * heads):
        o_ref[EVEN_RV + j] = vg[:, _lane_block(j)].astype(BF16)

    dcos, dsina, dsinb = dcos_ref[...], dsina_ref[...], dsinb_ref[...]
    dil_rot = lambda x: x * dcos + pltpu.roll(x, LANES - half, 1) * dsina + pltpu.roll(x, half, 1) * dsinb
    pairs = DIL_WIDTH // LANES
    dq = proj(4 * heads, pairs)
    for j in range(pairs):
        o_ref[EVEN_DQ + j] = (dil_rot(dq[:, _lane_block(j)]) * (DIL_DH ** -0.5)).astype(BF16)
    dk = proj(4 * heads + pairs, pairs)
    for j in range(pairs):
        o_ref[EVEN_DK + j] = dil_rot(dk[:, _lane_block(j)]).astype(BF16)
    dv = proj(4 * heads + 2 * pairs, pairs)
    for j in range(pairs):
        o_ref[EVEN_DV + j] = dv[:, _lane_block(j)].astype(BF16)


def _even_tables(seq):
    c = RET_CHUNK
    log_g = np.log(1.0 - 2.0 ** (-5.0 - np.arange(RET_HEADS, dtype=np.float64)))
    i = np.arange(c, dtype=np.float64)
    rel = i[:, None] - i[None, :]
    dmask = np.where(rel >= 0, np.exp(log_g[:, None, None] * np.maximum(rel, 0.0)), 0.0)
    spread = lambda a: np.tile(np.repeat(a.T, RET_DK, axis=1), (ROW_TILE // c, 1))
    q_decay = spread(np.exp(log_g[:, None] * (i + 1.0)))
    k_decay = spread(np.exp(log_g[:, None] * (c - 1 - i)))
    chunk_decay = np.exp(log_g * c)[:, None, None] * np.ones((1, 1, RET_DV))
    pos = np.arange(seq, dtype=np.float32)[:, None]

    freq = np.float32(RET_THETA) ** (-np.linspace(0.0, 1.0, RET_DK // 2, dtype=np.float32))
    ang = pos * freq[None, :]
    rcos = np.concatenate([np.cos(ang), np.cos(ang)], axis=1)
    rsin = np.concatenate([-np.sin(ang), np.sin(ang)], axis=1)

    freq = np.float32(ROPE_THETA) ** (-np.arange(0, ROT_DIM, 2, dtype=np.float32) / np.float32(ROT_DIM))
    ang = pos * freq[None, :]
    cos, sin = np.cos(ang), np.sin(ang)
    zeros = np.zeros_like(sin)
    head = lambda a, b, fill: np.concatenate([a, b, np.full((seq, DIL_DH - ROT_DIM), fill, np.float32)], axis=1)
    two = lambda x: np.concatenate([x, x], axis=1)
    dcos, dsina, dsinb = two(head(cos, cos, 1.0)), two(head(-sin, zeros, 0.0)), two(head(zeros, sin, 0.0))
    f = lambda a: np.asarray(a, np.float32)
    return dict(rcos=f(rcos), rsin=f(rsin), q_decay=f(q_decay), k_decay=f(k_decay), dcos=f(dcos),
                dsina=f(dsina), dsinb=f(dsinb), dmask=f(dmask), chunk_decay=f(chunk_decay))


def _inproj_even(x, gain, weight, tables, seq):
    t, d = x.shape
    tiles_per_seq = seq // ROW_TILE
    by_position = lambda a: (a, pl.BlockSpec((ROW_TILE, LANES), lambda i: (i % tiles_per_seq, 0)))
    operands = [(x, pl.BlockSpec((ROW_TILE, d), lambda i: (i, 0))), gain, weight,
                by_position(tables["rcos"]), by_position(tables["rsin"]),
                _resident(tables["q_decay"]), _resident(tables["k_decay"]),
                by_position(tables["dcos"]), by_position(tables["dsina"]), by_position(tables["dsinb"])]
    return pl.pallas_call(
        _inproj_even_kernel,
        grid=(t // ROW_TILE,),
        in_specs=[spec for _, spec in operands],
        out_specs=pl.BlockSpec((EVEN_SLABS, ROW_TILE, LANES), lambda i: (0, i, 0)),
        out_shape=jax.ShapeDtypeStruct((EVEN_SLABS, t, LANES), BF16),
        compiler_params=_params("parallel"),
        name="inproj_even",
    )(*[a for a, _ in operands])


def _retention_kernel(q_ref, qdec_ref, k_ref, kdec_ref, v_ref, g_ref, dm_ref, cd_ref, o_ref):
    c = RET_CHUNK
    seq = q_ref.shape[0]
    dmask = dm_ref[...]
    chunk_decay = cd_ref[...]

    def body(n, state):
        rows = [_tile_rows(n * CHUNK_GROUP + g, c) for g in range(CHUNK_GROUP)]
        vs = [v_ref[r, :] for r in rows]
        scores = [_mm_nt(q_ref[r, :], k_ref[r, :]) for r in rows]
        kvs = [_mm_tn(kdec_ref[r, :], v) for r, v in zip(rows, vs)]
        for r, v, s, kv in zip(rows, vs, scores, kvs):
            lhs = jnp.concatenate([(s * dmask).astype(BF16), qdec_ref[r, :]], axis=1)
            rhs = jnp.concatenate([v, state.astype(BF16)], axis=0)
            o = _head_norm(_mm(lhs, rhs)) * _silu(g_ref[r, :].astype(F32))
            o_ref[r, :] = o.astype(BF16)
            state = state * chunk_decay + kv
        return state

    lax.fori_loop(0, seq // c // CHUNK_GROUP, body, jnp.zeros((RET_DK, RET_DV), F32))


def _retention(p, tables, batch, seq):
    head_block = lambda first: pl.BlockSpec((None, None, seq, RET_DK), lambda b, j: (first + j, b, 0, 0))
    per_head = lambda a: pl.BlockSpec((None,) + a.shape[1:], lambda b, j: (j, 0, 0))
    dmask, chunk_decay = tables["dmask"], tables["chunk_decay"]
    return pl.pallas_call(
        _retention_kernel,
        grid=(batch, RET_HEADS),
        in_specs=[head_block(EVEN_RQ), head_block(EVEN_RQD), head_block(EVEN_RK), head_block(EVEN_RKD),
                  head_block(EVEN_RV), head_block(EVEN_RG), per_head(dmask), per_head(chunk_decay)],
        out_specs=pl.BlockSpec((None, None, seq, RET_DV), lambda b, j: (j, b, 0, 0)),
        out_shape=jax.ShapeDtypeStruct((RET_HEADS, batch, seq, RET_DV), BF16),
        compiler_params=_params("parallel", "parallel"),
        name="retention",
    )(p, p, p, p, p, p, dmask, chunk_decay)


def _dilated_kernel(q_ref, k_ref, v_ref, o_ref, qs_ref, ks_ref, vs_ref, ob_ref, mb_ref, lb_ref):
    seq = q_ref.shape[0]
    c = DIL_BLOCK
    pad = DIL_PAD

    ks_ref[pl.ds(0, pad), :] = jnp.zeros((pad, LANES), F32)
    vs_ref[pl.ds(0, pad), :] = jnp.zeros((pad, LANES), F32)

    def stage(i, carry):
        rows = _tile_rows(i, PREP_TILE)
        dst = pl.ds(pl.multiple_of(pad + i * PREP_TILE, DIL_BLOCK), PREP_TILE)
        qs_ref[rows, :] = q_ref[rows, :].astype(F32)
        ks_ref[dst, :] = k_ref[rows, :].astype(F32)
        vs_ref[dst, :] = v_ref[rows, :].astype(F32)
        return carry

    lax.fori_loop(0, seq // PREP_TILE, stage, 0)

    head0 = lax.broadcasted_iota(jnp.int32, (c, LANES), 1) < DIL_DH
    qi = lax.broadcasted_iota(jnp.int32, (c, 2 * c), 0)
    kj = lax.broadcasted_iota(jnp.int32, (c, 2 * c), 1)
    dist = qi + c - kj
    band = (dist >= 0) & (dist <= c)
    causal = lax.broadcasted_iota(jnp.int32, (c, c), 0) >= lax.broadcasted_iota(jnp.int32, (c, c), 1)

    def rows_of(start, n, dil):
        return pl.ds(start, n, stride=dil) if dil > 1 else pl.ds(start, n)

    plans = []
    for br, (window, dil) in enumerate(DIL_PATTERNS):
        assert window // dil == c
        blocks = seq // dil // c
        run = min(DIL_GROUP, blocks)
        assert blocks == run or c * dil <= pad
        plans.append((br, dil, run, blocks // run, DIL_GROUP // run, dil * blocks // DIL_GROUP))

    def geometry(plan, t):
        _, dil, run, runs, classes, _ = plan
        jg = t % runs
        return [(t // runs) * classes + ci + jg * (run * c * dil) for ci in range(classes)], jg == 0

    def key_rows(plan, q_start):
        _, dil, run, runs, _, _ = plan
        lead = 0 if runs == 1 else 1
        return rows_of(pad + q_start - lead * c * dil, (run + lead) * c, dil), lead

    def tile_keys(strip, u, lead):
        first = u + lead - 1
        return strip[max(first, 0) * c:(first + 2) * c]

    def issue_scores(plan, t):
        _, dil, run, _, _, _ = plan
        starts, _ = geometry(plan, t)
        tiles = []
        for q_start in starts:
            q = qs_ref[rows_of(q_start, run * c, dil), :]
            rows, lead = key_rows(plan, q_start)
            k = ks_ref[rows, :].astype(BF16)
            for u in range(run):
                qu = q[u * c:(u + 1) * c]
                ku = tile_keys(k, u, lead)
                for qh in (jnp.where(head0, qu, 0.0).astype(BF16), jnp.where(head0, 0.0, qu).astype(BF16)):
                    tiles.append(_mm_nt(qh, ku))
        return tiles

    def finish(plan, t, tiles):
        br, dil, run, _, _, _ = plan
        starts, first_run = geometry(plan, t)
        band_first = band & (kj >= jnp.where(first_run, c, 0))
        tiles = iter(tiles)
        for q_start in starts:
            rows, lead = key_rows(plan, q_start)
            v = vs_ref[rows, :]
            head0_kv = lax.broadcasted_iota(jnp.int32, v.shape, 1) < DIL_DH
            v_heads = (jnp.where(head0_kv, v, 1.0).astype(BF16), jnp.where(head0_kv, 1.0, v).astype(BF16))
            for u in range(run):
                valid = band if u > 0 else (band_first if lead else causal)
                res, mx = [], []
                for vh in v_heads:
                    s = jnp.where(valid, next(tiles), -jnp.inf)
                    m = jnp.max(s, axis=-1, keepdims=True)
                    res.append(_mm(jnp.exp(s - m).astype(BF16), tile_keys(vh, u, lead)))
                    mx.append(jnp.broadcast_to(m, (c, LANES)))
                rows = rows_of(q_start + u * (c * dil), c, dil)
                ob_ref[br, rows, :] = jnp.where(head0, res[0], res[1])
                lb_ref[br, rows, :] = jnp.where(head0, res[1], res[0])
                mb_ref[br, rows, :] = jnp.where(head0, mx[0], mx[1])

    for plan in plans:
        def step(t, carry, plan=plan):
            finish(plan, t, issue_scores(plan, t))
            return carry

        lax.fori_loop(0, plan[-1], step, 0)

    def combine(i, carry):
        rows = _tile_rows(i, PREP_TILE)
        n_br = len(DIL_PATTERNS)
        ms = [mb_ref[b, rows, :] for b in range(n_br)]
        m = functools.reduce(jnp.maximum, ms)
        es = [jnp.exp(x - m) for x in ms]
        num = sum(es[b] * ob_ref[b, rows, :] for b in range(n_br))
        den = sum(es[b] * pltpu.roll(lb_ref[b, rows, :], DIL_DH, 1) for b in range(n_br))
        o_ref[rows, :] = (num / den).astype(BF16)
        return carry

    lax.fori_loop(0, seq // PREP_TILE, combine, 0)


def _dilated(p, batch, seq):
    pairs = DIL_WIDTH // LANES
    pair_block = lambda first: pl.BlockSpec((None, None, seq, LANES), lambda b, j: (first + j, b, 0, 0))
    n_br = len(DIL_PATTERNS)
    return pl.pallas_call(
        _dilated_kernel,
        grid=(batch, pairs),
        in_specs=[pair_block(EVEN_DQ), pair_block(EVEN_DK), pair_block(EVEN_DV)],
        out_specs=pl.BlockSpec((None, None, seq, LANES), lambda b, j: (j, b, 0, 0)),
        out_shape=jax.ShapeDtypeStruct((pairs, batch, seq, LANES), BF16),
        scratch_shapes=[pltpu.VMEM((seq, LANES), F32),
                        pltpu.VMEM((seq + DIL_PAD, LANES), F32),
                        pltpu.VMEM((seq + DIL_PAD, LANES), F32),
                        pltpu.VMEM((n_br, seq, LANES), F32),
                        pltpu.VMEM((n_br, seq, LANES), F32),
                        pltpu.VMEM((n_br, seq, LANES), F32)],
        compiler_params=_params("parallel", "parallel"),
        name="dilated",
    )(p, p, p)


def _inproj_gla_kernel(x_ref, g_ref, w_ref, wa_ref, ba_ref, qk_ref, vr_ref, dec_ref):
    c = GLA_CHUNK
    per_tile = ROW_TILE // c
    kw, vw = GLA_KW, GLA_VW
    xg, r = _norm_parts(x_ref[...], g_ref[...])
    proj = lambda lo, hi: _mm(xg, w_ref[:, lo:hi]) * r
    a_low = proj(2 * kw + 2 * vw, GLA_IN_PADDED).astype(BF16)
    z = _mm(a_low, wa_ref[...]) + ba_ref[...]
    q = proj(0, kw) * (GLA_DK ** -0.5)
    k = proj(kw, 2 * kw)
    vr = proj(2 * kw, 2 * kw + 2 * vw)
    for j in range(2 * GLA_HEADS):
        vr_ref[j] = vr[:, j * GLA_DV:(j + 1) * GLA_DV].astype(BF16)
    b = (jnp.minimum(z, 0.0) - jnp.log1p(jnp.exp(-jnp.abs(z)))) * (1.0 / GLA_TAU)
    row_in_chunk = lax.broadcasted_iota(jnp.int32, b.shape, 0) % c
    shift = 1
    while shift < c:
        b = b + jnp.where(row_in_chunk >= shift, pltpu.roll(b, shift, 0), 0.0)
        shift *= 2
    b3 = b.reshape(per_tile, c, kw)
    b_last = b3[:, c - 1:c, :]
    dec_ref[...] = jnp.exp(b_last)
    q_t = q * jnp.exp(b)
    k_t = k * jnp.exp(-b)
    k_l = (k.reshape(per_tile, c, kw) * jnp.exp(b_last - b3)).reshape(ROW_TILE, kw)
    for i, operand in enumerate((q_t, k_t, k_l)):
        for j in range(GLA_HEADS):
            qk_ref[i * GLA_HEADS + j] = operand[:, j * GLA_DK:(j + 1) * GLA_DK].astype(BF16)


def _inproj_gla(x, gain, weight, wa, ba):
    t, d = x.shape
    per_tile = ROW_TILE // GLA_CHUNK
    operands = [(x, pl.BlockSpec((ROW_TILE, d), lambda i: (i, 0))), gain, weight, wa, ba]
    return pl.pallas_call(
        _inproj_gla_kernel,
        grid=(t // ROW_TILE,),
        in_specs=[spec for _, spec in operands],
        out_specs=[pl.BlockSpec((3 * GLA_HEADS, ROW_TILE, GLA_DK), lambda i: (0, i, 0)),
                   pl.BlockSpec((2 * GLA_HEADS, ROW_TILE, GLA_DV), lambda i: (0, i, 0)),
                   pl.BlockSpec((per_tile, 1, GLA_KW), lambda i: (i, 0, 0))],
        out_shape=[jax.ShapeDtypeStruct((3 * GLA_HEADS, t, GLA_DK), BF16),
                   jax.ShapeDtypeStruct((2 * GLA_HEADS, t, GLA_DV), BF16),
                   jax.ShapeDtypeStruct((t // GLA_CHUNK, 1, GLA_KW), F32)],
        compiler_params=_params("parallel"),
        name="inproj_gla",
    )(*[a for a, _ in operands])


def _gla_kernel(qt_ref, kt_ref, kl_ref, v_ref, r_ref, dec_ref, o_ref):
    c = GLA_CHUNK
    seq = qt_ref.shape[0]
    causal = lax.broadcasted_iota(jnp.int32, (c, c), 0) >= lax.broadcasted_iota(jnp.int32, (c, c), 1)

    def body(n, state):
        chunks = [n * CHUNK_GROUP + g for g in range(CHUNK_GROUP)]
        rows = [_tile_rows(j, c) for j in chunks]
        vs = [v_ref[r, :] for r in rows]
        qts = [qt_ref[r, :] for r in rows]
        scores = [_mm_nt(q_t, kt_ref[r, :]) for r, q_t in zip(rows, qts)]
        kvs = [_mm_tn(v, kl_ref[r, :]) for r, v in zip(rows, vs)]
        for j, r, v, q_t, s, kv in zip(chunks, rows, vs, qts, scores, kvs):
            o = _mm(jnp.where(causal, s, 0.0).astype(BF16), v) + _mm_nt(q_t, state.astype(BF16))
            o = _head_norm(o) * _silu(r_ref[r, :].astype(F32))
            o_ref[r, :] = o.astype(BF16)
            state = state * dec_ref[j] + kv
        return state

    lax.fori_loop(0, seq // c // CHUNK_GROUP, body, jnp.zeros((GLA_DV, GLA_DK), F32))


def _gla(qk, vr, dec, batch, seq):
    h = GLA_HEADS
    slab = lambda first, width: pl.BlockSpec((None, None, seq, width), lambda b, j: (first + j, b, 0, 0))
    return pl.pallas_call(
        _gla_kernel,
        grid=(batch, h),
        in_specs=[slab(0, GLA_DK), slab(h, GLA_DK), slab(2 * h, GLA_DK), slab(0, GLA_DV), slab(h, GLA_DV),
                  pl.BlockSpec((None, seq // GLA_CHUNK, 1, GLA_DK), lambda b, j: (b, 0, 0, j))],
        out_specs=slab(0, GLA_DV),
        out_shape=jax.ShapeDtypeStruct((h, batch, seq, GLA_DV), BF16),
        compiler_params=_params("parallel", "parallel"),
        name="gla",
    )(qk, qk, qk, vr, vr, dec)


def kernel(x, ffn_pre_norm, ffn_pre_w_gate, ffn_pre_w_up, ffn_pre_w_down, mix_norm, ab_w_in, ab_w_out,
           gla_w_in, gla_w_a2, gla_b_a, gla_w_out, ffn_post_norm, ffn_post_w_gate, ffn_post_w_up,
           ffn_post_w_down, final_norm):
    batch, seq, d = x.shape
    depth = ffn_pre_norm.shape[0]
    bf = lambda w: w.astype(BF16)
    gains = lambda g: g.reshape(g.shape[0], 1, d)
    assert RET_HEADS * RET_DV == DIL_WIDTH

    pre = [gains(ffn_pre_norm), bf(ffn_pre_w_gate), bf(ffn_pre_w_up), bf(ffn_pre_w_down)]
    post = [gains(ffn_post_norm), bf(ffn_post_w_gate), bf(ffn_post_w_up), bf(ffn_post_w_down)]
    mix_g = gains(mix_norm)
    ab_in, ab_out = bf(ab_w_in), bf(ab_w_out)
    gla_in = jnp.pad(bf(gla_w_in), ((0, 0), (0, 0), (0, GLA_IN_PADDED - gla_w_in.shape[2])))
    gla_a2 = jnp.pad(bf(gla_w_a2), ((0, 0), (0, LANES - gla_w_a2.shape[1]), (0, 0)))
    gla_ba = gla_b_a.reshape(gla_b_a.shape[0], 1, -1)
    gla_out = bf(gla_w_out)
    tables = _even_tables(seq)

    xt = x.reshape(batch * seq, d)
    for l in range(depth):
        xt = _ffn(xt, [], [_layer(w, l) for w in pre])
        if l % 2 == 0:
            p = _inproj_even(xt, _layer(mix_g, l), _layer(ab_in, l // 2), tables, seq)
            p = p.reshape(EVEN_SLABS, batch, seq, LANES)
            o_r = _retention(p, tables, batch, seq).reshape(RET_HEADS, batch * seq, RET_DV)
            o_d = _dilated(p, batch, seq).reshape(DIL_WIDTH // LANES, batch * seq, LANES)
            mixes = [(o_r, _layer(ab_out, l // 2, rows=DIL_WIDTH, row_block=0)),
                     (o_d, _layer(ab_out, l // 2, rows=DIL_WIDTH, row_block=1))]
        else:
            qk, vr, dec = _inproj_gla(xt, _layer(mix_g, l), _layer(gla_in, l // 2), _layer(gla_a2, l // 2),
                                      _layer(gla_ba, l // 2))
            o = _gla(qk.reshape(3 * GLA_HEADS, batch, seq, GLA_DK), vr.reshape(2 * GLA_HEADS, batch, seq, GLA_DV),
                     dec.reshape(batch, seq // GLA_CHUNK, 1, GLA_KW), batch, seq)
            mixes = [(o.reshape(GLA_HEADS, batch * seq, GLA_DV), _layer(gla_out, l // 2))]
        xt = _ffn(xt, mixes, [_layer(w, l) for w in post],
                  final_g=final_norm.reshape(1, d) if l == depth - 1 else None)
    return xt.reshape(batch, seq, d)
```

```python
import functools

import jax
import jax.numpy as jnp
import numpy as np
from jax import lax
from jax.experimental import pallas as pl
from jax.experimental.pallas import tpu as pltpu

F32 = jnp.float32
BF16 = jnp.bfloat16

LANES = 128
VMEM_LIMIT_BYTES = 56 * 1024 * 1024

D_MODEL = 1024
FFN_HIDDEN = 2816
EPS = 1e-6

RET_HEADS = 4
RET_DK = 128
RET_DV = 128
RET_CHUNK = 128
RET_THETA = 10000.0
RET_WIDTH = RET_HEADS * RET_DK

DIL_HEADS = 8
DIL_DH = 64
DIL_PATTERNS = ((128, 1), (512, 4), (2048, 16))
DIL_BLOCK = 128
DIL_GROUP = 32
ROPE_THETA = 500000.0
ROT_DIM = DIL_DH // 4
DIL_PAD = DIL_BLOCK
DIL_WIDTH = DIL_HEADS * DIL_DH

GLA_HEADS = 4
GLA_DK = (D_MODEL // 2) // GLA_HEADS
GLA_DV = D_MODEL // GLA_HEADS
GLA_LOWRANK = 16
GLA_TAU = 16.0
GLA_CHUNK = 64
GLA_KW = GLA_HEADS * GLA_DK
GLA_VW = GLA_HEADS * GLA_DV

CHUNK_GROUP = 32

EVEN_RQ, EVEN_RQD, EVEN_RK, EVEN_RKD, EVEN_RV, EVEN_RG, EVEN_DQ, EVEN_DK, EVEN_DV = range(0, 36, 4)
EVEN_SLABS = 9 * RET_WIDTH // LANES
GLA_IN_PADDED = 2 * GLA_KW + 2 * GLA_VW + LANES

ROW_TILE = 512
PREP_TILE = 512

NT_DIMS = (((1,), (1,)), ((), ()))
TN_DIMS = (((0,), (0,)), ((), ()))


def _params(*semantics):
    return pltpu.CompilerParams(dimension_semantics=semantics,
                                vmem_limit_bytes=VMEM_LIMIT_BYTES)


def _resident(a):
    zeros = (0,) * a.ndim
    return a, pl.BlockSpec(a.shape, lambda *_: zeros, pipeline_mode=pl.Buffered(1))


def _layer(w, l, rows=None, row_block=0):
    shape = (None, w.shape[1] if rows is None else rows, w.shape[2])
    return w, pl.BlockSpec(shape, lambda *_: (l, row_block, 0), pipeline_mode=pl.Buffered(1))


def _rmsnorm(x, g):
    return x * lax.rsqrt(jnp.mean(x * x, axis=-1, keepdims=True) + EPS) * g


def _norm_parts(x, g):
    return (x * g).astype(BF16), lax.rsqrt(jnp.mean(x * x, axis=-1, keepdims=True) + EPS)


def _head_norm(o):
    mu = jnp.mean(o, axis=-1, keepdims=True)
    c = o - mu
    return c * lax.rsqrt(jnp.mean(c * c, axis=-1, keepdims=True) + EPS)


def _silu(x):
    return x * jax.nn.sigmoid(x)


def _mm(a, b):
    return jnp.dot(a, b, preferred_element_type=F32)


def _mm_nt(a, b):
    return lax.dot_general(a, b, NT_DIMS, preferred_element_type=F32)


def _mm_tn(a, b):
    return lax.dot_general(a, b, TN_DIMS, preferred_element_type=F32)


def _tile_rows(i, n):
    return pl.ds(pl.multiple_of(i * n, n), n)


def _lane_block(j, n=1):
    return slice(j * LANES, (j + n) * LANES)


def _ffn_kernel(*refs, n_mix, final):
    x_ref = refs[0]
    mix_refs = refs[1:1 + 2 * n_mix]
    g_ref, wg_ref, wu_ref, wd_ref = refs[1 + 2 * n_mix:5 + 2 * n_mix]
    o_ref = refs[-1]
    x = x_ref[...]
    for i in range(n_mix):
        mix_ref = mix_refs[2 * i]
        mix = jnp.concatenate([mix_ref[g] for g in range(mix_ref.shape[0])], axis=1)
        x = x + _mm(mix, mix_refs[2 * i + 1][...])
    xg, r = _norm_parts(x, g_ref[...])
    act = (_silu(_mm(xg, wg_ref[...]) * r) * (_mm(xg, wu_ref[...]) * r)).astype(BF16)
    y = x + 0.5 * _mm(act, wd_ref[...])
    if final:
        y = _rmsnorm(y, refs[-2][...])
    o_ref[...] = y


def _ffn(x, mixes, weights, final_g=None):
    t, d = x.shape
    row = lambda w: pl.BlockSpec((ROW_TILE, w), lambda i: (i, 0))
    args, specs = [x], [row(d)]
    for m, (w, spec) in mixes:
        args += [m, w]
        specs += [pl.BlockSpec((m.shape[0], ROW_TILE, m.shape[2]), lambda i: (0, i, 0)), spec]
    for w, spec in weights + ([_resident(final_g)] if final_g is not None else []):
        args.append(w)
        specs.append(spec)
    return pl.pallas_call(
        functools.partial(_ffn_kernel, n_mix=len(mixes), final=final_g is not None),
        grid=(t // ROW_TILE,),
        in_specs=specs,
        out_specs=row(d),
        out_shape=jax.ShapeDtypeStruct((t, d), F32),
        compiler_params=_params("parallel"),
        name="ffn",
    )(*args)


def _inproj_even_kernel(x_ref, g_ref, w_ref, rcos_ref, rsin_ref, qd_ref, kd_ref,
                        dcos_ref, dsina_ref, dsinb_ref, o_ref):
    xg, r = _norm_parts(x_ref[...], g_ref[...])
    proj = lambda first_block, n_blocks: _mm(xg, w_ref[:, _lane_block(first_block, n_blocks)]) * r
    heads = RET_WIDTH // LANES
    half = ROT_DIM // 2

    rcos, rsin = rcos_ref[...], rsin_ref[...]
    ret_rot = lambda x: x * rcos + pltpu.roll(x, RET_DK // 2, 1) * rsin
    q = proj(0, heads)
    for j in range(heads):
        qj = ret_rot(q[:, _lane_block(j)])
        o_ref[EVEN_RQ + j] = qj.astype(BF16)
        o_ref[EVEN_RQD + j] = (qj * qd_ref[:, _lane_block(j)]).astype(BF16)
    k = proj(heads, heads)
    for j in range(heads):
        kj = ret_rot(k[:, _lane_block(j)]) * (RET_DK ** -0.5)
        o_ref[EVEN_RK + j] = kj.astype(BF16)
        o_ref[EVEN_RKD + j] = (kj * kd_ref[:, _lane_block(j)]).astype(BF16)
    vg = proj(2 * heads, 2 * heads)
    for j in range(2 * heads):
        o_ref[EVEN_RV + j] = vg[:, _lane_block(j)].astype(BF16)

    dcos, dsina, dsinb = dcos_ref[...], dsina_ref[...], dsinb_ref[...]
    dil_rot = lambda x: x * dcos + pltpu.roll(x, LANES - half, 1) * dsina + pltpu.roll(x, half, 1) * dsinb
    pairs = DIL_WIDTH // LANES
    dq = proj(4 * heads, pairs)
    for j in range(pairs):
        o_ref[EVEN_DQ + j] = (dil_rot(dq[:, _lane_block(j)]) * (DIL_DH ** -0.5)).astype(BF16)
    dk = proj(4 * heads + pairs, pairs)
    for j in range(pairs):
        o_ref[EVEN_DK + j] = dil_rot(dk[:, _lane_block(j)]).astype(BF16)
    dv = proj(4 * heads + 2 * pairs, pairs)
    for j in range(pairs):
        o_ref[EVEN_DV + j] = dv[:, _lane_block(j)].astype(BF16)


def _even_tables(seq):
    c = RET_CHUNK
    log_g = np.log(1.0 - 2.0 ** (-5.0 - np.arange(RET_HEADS, dtype=np.float64)))
    i = np.arange(c, dtype=np.float64)
    rel = i[:, None] - i[None, :]
    dmask = np.where(rel >= 0, np.exp(log_g[:, None, None] * np.maximum(rel, 0.0)), 0.0)
    spread = lambda a: np.tile(np.repeat(a.T, RET_DK, axis=1), (ROW_TILE // c, 1))
    q_decay = spread(np.exp(log_g[:, None] * (i + 1.0)))
    k_decay = spread(np.exp(log_g[:, None] * (c - 1 - i)))
    chunk_decay = np.exp(log_g * c)[:, None, None] * np.ones((1, 1, RET_DV))
    pos = np.arange(seq, dtype=np.float32)[:, None]

    freq = np.float32(RET_THETA) ** (-np.linspace(0.0, 1.0, RET_DK // 2, dtype=np.float32))
    ang = pos * freq[None, :]
    rcos = np.concatenate([np.cos(ang), np.cos(ang)], axis=1)
    rsin = np.concatenate([-np.sin(ang), np.sin(ang)], axis=1)

    freq = np.float32(ROPE_THETA) ** (-np.arange(0, ROT_DIM, 2, dtype=np.float32) / np.float32(ROT_DIM))
    ang = pos * freq[None, :]
    cos, sin = np.cos(ang), np.sin(ang)
    zeros = np.zeros_like(sin)
    head = lambda a, b, fill: np.concatenate([a, b, np.full((seq, DIL_DH - ROT_DIM), fill, np.float32)], axis=1)
    two = lambda x: np.concatenate([x, x], axis=1)
    dcos, dsina, dsinb = two(head(cos, cos, 1.0)), two(head(-sin, zeros, 0.0)), two(head(zeros, sin, 0.0))
    f = lambda a: np.asarray(a, np.float32)
    return dict(rcos=f(rcos), rsin=f(rsin), q_decay=f(q_decay), k_decay=f(k_decay), dcos=f(dcos),
                dsina=f(dsina), dsinb=f(dsinb), dmask=f(dmask), chunk_decay=f(chunk_decay))


def _inproj_even(x, gain, weight, tables, seq):
    t, d = x.shape
    tiles_per_seq = seq // ROW_TILE
    by_position = lambda a: (a, pl.BlockSpec((ROW_TILE, LANES), lambda i: (i % tiles_per_seq, 0)))
    operands = [(x, pl.BlockSpec((ROW_TILE, d), lambda i: (i, 0))), gain, weight,
                by_position(tables["rcos"]), by_position(tables["rsin"]),
                _resident(tables["q_decay"]), _resident(tables["k_decay"]),
                by_position(tables["dcos"]), by_position(tables["dsina"]), by_position(tables["dsinb"])]
    return pl.pallas_call(
        _inproj_even_kernel,
        grid=(t // ROW_TILE,),
        in_specs=[spec for _, spec in operands],
        out_specs=pl.BlockSpec((EVEN_SLABS, ROW_TILE, LANES), lambda i: (0, i, 0)),
        out_shape=jax.ShapeDtypeStruct((EVEN_SLABS, t, LANES), BF16),
        compiler_params=_params("parallel"),
        name="inproj_even",
    )(*[a for a, _ in operands])


def _retention_kernel(q_ref, qdec_ref, k_ref, kdec_ref, v_ref, g_ref, dm_ref, cd_ref, o_ref):
    c = RET_CHUNK
    seq = q_ref.shape[0]
    dmask = dm_ref[...]
    chunk_decay = cd_ref[...]

    def body(n, state):
        rows = [_tile_rows(n * CHUNK_GROUP + g, c) for g in range(CHUNK_GROUP)]
        vs = [v_ref[r, :] for r in rows]
        scores = [_mm_nt(q_ref[r, :], k_ref[r, :]) for r in rows]
        kvs = [_mm_tn(kdec_ref[r, :], v) for r, v in zip(rows, vs)]
        for r, v, s, kv in zip(rows, vs, scores, kvs):
            lhs = jnp.concatenate([(s * dmask).astype(BF16), qdec_ref[r, :]], axis=1)
            rhs = jnp.concatenate([v, state.astype(BF16)], axis=0)
            o = _head_norm(_mm(lhs, rhs)) * _silu(g_ref[r, :].astype(F32))
            o_ref[r, :] = o.astype(BF16)
            state = state * chunk_decay + kv
        return state

    lax.fori_loop(0, seq // c // CHUNK_GROUP, body, jnp.zeros((RET_DK, RET_DV), F32))


def _retention(p, tables, batch, seq):
    head_block = lambda first: pl.BlockSpec((None, None, seq, RET_DK), lambda b, j: (first + j, b, 0, 0))
    per_head = lambda a: pl.BlockSpec((None,) + a.shape[1:], lambda b, j: (j, 0, 0))
    dmask, chunk_decay = tables["dmask"], tables["chunk_decay"]
    return pl.pallas_call(
        _retention_kernel,
        grid=(batch, RET_HEADS),
        in_specs=[head_block(EVEN_RQ), head_block(EVEN_RQD), head_block(EVEN_RK), head_block(EVEN_RKD),
                  head_block(EVEN_RV), head_block(EVEN_RG), per_head(dmask), per_head(chunk_decay)],
        out_specs=pl.BlockSpec((None, None, seq, RET_DV), lambda b, j: (j, b, 0, 0)),
        out_shape=jax.ShapeDtypeStruct((RET_HEADS, batch, seq, RET_DV), BF16),
        compiler_params=_params("parallel", "parallel"),
        name="retention",
    )(p, p, p, p, p, p, dmask, chunk_decay)


def _dilated_kernel(q_ref, k_ref, v_ref, o_ref, qs_ref, ks_ref, vs_ref, ob_ref, mb_ref, lb_ref):
    seq = q_ref.shape[0]
    c = DIL_BLOCK
    pad = DIL_PAD

    ks_ref[pl.ds(0, pad), :] = jnp.zeros((pad, LANES), F32)
    vs_ref[pl.ds(0, pad), :] = jnp.zeros((pad, LANES), F32)

    def stage(i, carry):
        rows = _tile_rows(i, PREP_TILE)
        dst = pl.ds(pl.multiple_of(pad + i * PREP_TILE, DIL_BLOCK), PREP_TILE)
        qs_ref[rows, :] = q_ref[rows, :].astype(F32)
        ks_ref[dst, :] = k_ref[rows, :].astype(F32)
        vs_ref[dst, :] = v_ref[rows, :].astype(F32)
        return carry

    lax.fori_loop(0, seq // PREP_TILE, stage, 0)

    head0 = lax.broadcasted_iota(jnp.int32, (c, LANES), 1) < DIL_DH
    qi = lax.broadcasted_iota(jnp.int32, (c, 2 * c), 0)
    kj = lax.broadcasted_iota(jnp.int32, (c, 2 * c), 1)
    dist = qi + c - kj
    band = (dist >= 0) & (dist <= c)
    causal = lax.broadcasted_iota(jnp.int32, (c, c), 0) >= lax.broadcasted_iota(jnp.int32, (c, c), 1)

    def rows_of(start, n, dil):
        return pl.ds(start, n, stride=dil) if dil > 1 else pl.ds(start, n)

    plans = []
    for br, (window, dil) in enumerate(DIL_PATTERNS):
        assert window // dil == c
        blocks = seq // dil // c
        run = min(DIL_GROUP, blocks)
        assert blocks == run or c * dil <= pad
        plans.append((br, dil, run, blocks // run, DIL_GROUP // run, dil * blocks // DIL_GROUP))

    def geometry(plan, t):
        _, dil, run, runs, classes, _ = plan
        jg = t % runs
        return [(t // runs) * classes + ci + jg * (run * c * dil) for ci in range(classes)], jg == 0

    def key_rows(plan, q_start):
        _, dil, run, runs, _, _ = plan
        lead = 0 if runs == 1 else 1
        return rows_of(pad + q_start - lead * c * dil, (run + lead) * c, dil), lead

    def tile_keys(strip, u, lead):
        first = u + lead - 1
        return strip[max(first, 0) * c:(first + 2) * c]

    def issue_scores(plan, t):
        _, dil, run, _, _, _ = plan
        starts, _ = geometry(plan, t)
        tiles = []
        for q_start in starts:
            q = qs_ref[rows_of(q_start, run * c, dil), :]
            rows, lead = key_rows(plan, q_start)
            k = ks_ref[rows, :].astype(BF16)
            for u in range(run):
                qu = q[u * c:(u + 1) * c]
                ku = tile_keys(k, u, lead)
                for qh in (jnp.where(head0, qu, 0.0).astype(BF16), jnp.where(head0, 0.0, qu).astype(BF16)):
                    tiles.append(_mm_nt(qh, ku))
        return tiles

    def finish(plan, t, tiles):
        br, dil, run, _, _, _ = plan
        starts, first_run = geometry(plan, t)
        band_first = band & (kj >= jnp.where(first_run, c, 0))
        tiles = iter(tiles)
        for q_start in starts:
            rows, lead = key_rows(plan, q_start)
            v = vs_ref[rows, :]
            head0_kv = lax.broadcasted_iota(jnp.int32, v.shape, 1) < DIL_DH
            v_heads = (jnp.where(head0_kv, v, 1.0).astype(BF16), jnp.where(head0_kv, 1.0, v).astype(BF16))
            for u in range(run):
                valid = band if u > 0 else (band_first if lead else causal)
                res, mx = [], []
                for vh in v_heads:
                    s = jnp.where(valid, next(tiles), -jnp.inf)
                    m = jnp.max(s, axis=-1, keepdims=True)
                    res.append(_mm(jnp.exp(s - m).astype(BF16), tile_keys(vh, u, lead)))
                    mx.append(jnp.broadcast_to(m, (c, LANES)))
                rows = rows_of(q_start + u * (c * dil), c, dil)
                ob_ref[br, rows, :] = jnp.where(head0, res[0], res[1])
                lb_ref[br, rows, :] = jnp.where(head0, res[1], res[0])
                mb_ref[br, rows, :] = jnp.where(head0, mx[0], mx[1])

    for plan in plans:
        def step(t, carry, plan=plan):
            finish(plan, t, issue_scores(plan, t))
            return carry

        lax.fori_loop(0, plan[-1], step, 0)

    def combine(i, carry):
        rows = _tile_rows(i, PREP_TILE)
        n_br = len(DIL_PATTERNS)
        ms = [mb_ref[b, rows, :] for b in range(n_br)]
        m = functools.reduce(jnp.maximum, ms)
        es = [jnp.exp(x - m) for x in ms]
        num = sum(es[b] * ob_ref[b, rows, :] for b in range(n_br))
        den = sum(es[b] * pltpu.roll(lb_ref[b, rows, :], DIL_DH, 1) for b in range(n_br))
        o_ref[rows, :] = (num / den).astype(BF16)
        return carry

    lax.fori_loop(0, seq // PREP_TILE, combine, 0)


def _dilated(p, batch, seq):
    pairs = DIL_WIDTH // LANES
    pair_block = lambda first: pl.BlockSpec((None, None, seq, LANES), lambda b, j: (first + j, b, 0, 0))
    n_br = len(DIL_PATTERNS)
    return pl.pallas_call(
        _dilated_kernel,
        grid=(batch, pairs),
        in_specs=[pair_block(EVEN_DQ), pair_block(EVEN_DK), pair_block(EVEN_DV)],
        out_specs=pl.BlockSpec((None, None, seq, LANES), lambda b, j: (j, b, 0, 0)),
        out_shape=jax.ShapeDtypeStruct((pairs, batch, seq, LANES), BF16),
        scratch_shapes=[pltpu.VMEM((seq, LANES), F32),
                        pltpu.VMEM((seq + DIL_PAD, LANES), F32),
                        pltpu.VMEM((seq + DIL_PAD, LANES), F32),
                        pltpu.VMEM((n_br, seq, LANES), F32),
                        pltpu.VMEM((n_br, seq, LANES), F32),
                        pltpu.VMEM((n_br, seq, LANES), F32)],
        compiler_params=_params("parallel", "parallel"),
        name="dilated",
    )(p, p, p)


def _inproj_gla_kernel(x_ref, g_ref, w_ref, wa_ref, ba_ref, qk_ref, vr_ref, dec_ref):
    c = GLA_CHUNK
    per_tile = ROW_TILE // c
    kw, vw = GLA_KW, GLA_VW
    xg, r = _norm_parts(x_ref[...], g_ref[...])
    proj = lambda lo, hi: _mm(xg, w_ref[:, lo:hi]) * r
    a_low = proj(2 * kw + 2 * vw, GLA_IN_PADDED).astype(BF16)
    z = _mm(a_low, wa_ref[...]) + ba_ref[...]
    q = proj(0, kw) * (GLA_DK ** -0.5)
    k = proj(kw, 2 * kw)
    vr = proj(2 * kw, 2 * kw + 2 * vw)
    for j in range(2 * GLA_HEADS):
        vr_ref[j] = vr[:, j * GLA_DV:(j + 1) * GLA_DV].astype(BF16)
    b = (jnp.minimum(z, 0.0) - jnp.log1p(jnp.exp(-jnp.abs(z)))) * (1.0 / GLA_TAU)
    row_in_chunk = lax.broadcasted_iota(jnp.int32, b.shape, 0) % c
    shift = 1
    while shift < c:
        b = b + jnp.where(row_in_chunk >= shift, pltpu.roll(b, shift, 0), 0.0)
        shift *= 2
    b3 = b.reshape(per_tile, c, kw)
    b_last = b3[:, c - 1:c, :]
    dec_ref[...] = jnp.exp(b_last)
    q_t = q * jnp.exp(b)
    k_t = k * jnp.exp(-b)
    k_l = (k.reshape(per_tile, c, kw) * jnp.exp(b_last - b3)).reshape(ROW_TILE, kw)
    for i, operand in enumerate((q_t, k_t, k_l)):
        for j in range(GLA_HEADS):
            qk_ref[i * GLA_HEADS + j] = operand[:, j * GLA_DK:(j + 1) * GLA_DK].astype(BF16)


def _inproj_gla(x, gain, weight, wa, ba):
    t, d = x.shape
    per_tile = ROW_TILE // GLA_CHUNK
    operands = [(x, pl.BlockSpec((ROW_TILE, d), lambda i: (i, 0))), gain, weight, wa, ba]
    return pl.pallas_call(
        _inproj_gla_kernel,
        grid=(t // ROW_TILE,),
        in_specs=[spec for _, spec in operands],
        out_specs=[pl.BlockSpec((3 * GLA_HEADS, ROW_TILE, GLA_DK), lambda i: (0, i, 0)),
                   pl.BlockSpec((2 * GLA_HEADS, ROW_TILE, GLA_DV), lambda i: (0, i, 0)),
                   pl.BlockSpec((per_tile, 1, GLA_KW), lambda i: (i, 0, 0))],
        out_shape=[jax.ShapeDtypeStruct((3 * GLA_HEADS, t, GLA_DK), BF16),
                   jax.ShapeDtypeStruct((2 * GLA_HEADS, t, GLA_DV), BF16),
                   jax.ShapeDtypeStruct((t // GLA_CHUNK, 1, GLA_KW), F32)],
        compiler_params=_params("parallel"),
        name="inproj_gla",
    )(*[a for a, _ in operands])


def _gla_kernel(qt_ref, kt_ref, kl_ref, v_ref, r_ref, dec_ref, o_ref):
    c = GLA_CHUNK
    seq = qt_ref.shape[0]
    causal = lax.broadcasted_iota(jnp.int32, (c, c), 0) >= lax.broadcasted_iota(jnp.int32, (c, c), 1)

    def body(n, state):
        chunks = [n * CHUNK_GROUP + g for g in range(CHUNK_GROUP)]
        rows = [_tile_rows(j, c) for j in chunks]
        vs = [v_ref[r, :] for r in rows]
        qts = [qt_ref[r, :] for r in rows]
        scores = [_mm_nt(q_t, kt_ref[r, :]) for r, q_t in zip(rows, qts)]
        kvs = [_mm_tn(v, kl_ref[r, :]) for r, v in zip(rows, vs)]
        for j, r, v, q_t, s, kv in zip(chunks, rows, vs, qts, scores, kvs):
            o = _mm(jnp.where(causal, s, 0.0).astype(BF16), v) + _mm_nt(q_t, state.astype(BF16))
            o = _head_norm(o) * _silu(r_ref[r, :].astype(F32))
            o_ref[r, :] = o.astype(BF16)
            state = state * dec_ref[j] + kv
        return state

    lax.fori_loop(0, seq // c // CHUNK_GROUP, body, jnp.zeros((GLA_DV, GLA_DK), F32))


def _gla(qk, vr, dec, batch, seq):
    h = GLA_HEADS
    slab = lambda first, width: pl.BlockSpec((None, None, seq, width), lambda b, j: (first + j, b, 0, 0))
    return pl.pallas_call(
        _gla_kernel,
        grid=(batch, h),
        in_specs=[slab(0, GLA_DK), slab(h, GLA_DK), slab(2 * h, GLA_DK), slab(0, GLA_DV), slab(h, GLA_DV),
                  pl.BlockSpec((None, seq // GLA_CHUNK, 1, GLA_DK), lambda b, j: (b, 0, 0, j))],
        out_specs=slab(0, GLA_DV),
        out_shape=jax.ShapeDtypeStruct((h, batch, seq, GLA_DV), BF16),
        compiler_params=_params("parallel", "parallel"),
        name="gla",
    )(qk, qk, qk, vr, vr, dec)


def kernel(x, ffn_pre_norm, ffn_pre_w_gate, ffn_pre_w_up, ffn_pre_w_down, mix_norm, ab_w_in, ab_w_out,
           gla_w_in, gla_w_a2, gla_b_a, gla_w_out, ffn_post_norm, ffn_post_w_gate, ffn_post_w_up,
           ffn_post_w_down, final_norm):
    batch, seq, d = x.shape
    depth = ffn_pre_norm.shape[0]
    bf = lambda w: w.astype(BF16)
    gains = lambda g: g.reshape(g.shape[0], 1, d)
    assert RET_HEADS * RET_DV == DIL_WIDTH

    pre = [gains(ffn_pre_norm), bf(ffn_pre_w_gate), bf(ffn_pre_w_up), bf(ffn_pre_w_down)]
    post = [gains(ffn_post_norm), bf(ffn_post_w_gate), bf(ffn_post_w_up), bf(ffn_post_w_down)]
    mix_g = gains(mix_norm)
    ab_in, ab_out = bf(ab_w_in), bf(ab_w_out)
    gla_in = jnp.pad(bf(gla_w_in), ((0, 0), (0, 0), (0, GLA_IN_PADDED - gla_w_in.shape[2])))
    gla_a2 = jnp.pad(bf(gla_w_a2), ((0, 0), (0, LANES - gla_w_a2.shape[1]), (0, 0)))
    gla_ba = gla_b_a.reshape(gla_b_a.shape[0], 1, -1)
    gla_out = bf(gla_w_out)
    tables = _even_tables(seq)

    xt = x.reshape(batch * seq, d)
    for l in range(depth):
        xt = _ffn(xt, [], [_layer(w, l) for w in pre])
        if l % 2 == 0:
            p = _inproj_even(xt, _layer(mix_g, l), _layer(ab_in, l // 2), tables, seq)
            p = p.reshape(EVEN_SLABS, batch, seq, LANES)
            o_r = _retention(p, tables, batch, seq).reshape(RET_HEADS, batch * seq, RET_DV)
            o_d = _dilated(p, batch, seq).reshape(DIL_WIDTH // LANES, batch * seq, LANES)
            mixes = [(o_r, _layer(ab_out, l // 2, rows=DIL_WIDTH, row_block=0)),
                     (o_d, _layer(ab_out, l // 2, rows=DIL_WIDTH, row_block=1))]
        else:
            qk, vr, dec = _inproj_gla(xt, _layer(mix_g, l), _layer(gla_in, l // 2), _layer(gla_a2, l // 2),
                                      _layer(gla_ba, l // 2))
            o = _gla(qk.reshape(3 * GLA_HEADS, batch, seq, GLA_DK), vr.reshape(2 * GLA_HEADS, batch, seq, GLA_DV),
                     dec.reshape(batch, seq // GLA_CHUNK, 1, GLA_KW), batch, seq)
            mixes = [(o.reshape(GLA_HEADS, batch * seq, GLA_DV), _layer(gla_out, l // 2))]
        xt = _ffn(xt, mixes, [_layer(w, l) for w in post],
                  final_g=final_norm.reshape(1, d) if l == depth - 1 else None)
    return xt.reshape(batch, seq, d)
```

```python
import functools

import jax
import jax.numpy as jnp
import numpy as np
from jax import lax
from jax.experimental import pallas as pl
from jax.experimental.pallas import tpu as pltpu

F32 = jnp.float32
BF16 = jnp.bfloat16

LANES = 128
VMEM_LIMIT_BYTES = 56 * 1024 * 1024

D_MODEL = 1024
FFN_HIDDEN = 2816
EPS = 1e-6

RET_HEADS = 4
RET_DK = 128
RET_DV = 128
RET_CHUNK = 128
RET_THETA = 10000.0
RET_WIDTH = RET_HEADS * RET_DK

DIL_HEADS = 8
DIL_DH = 64
DIL_PATTERNS = ((128, 1), (512, 4), (2048, 16))
DIL_BLOCK = 128
ROPE_THETA = 500000.0
ROT_DIM = DIL_DH // 4
DIL_WIDTH = DIL_HEADS * DIL_DH

GLA_HEADS = 4
GLA_DK = (D_MODEL // 2) // GLA_HEADS
GLA_DV = D_MODEL // GLA_HEADS
GLA_LOWRANK = 16
GLA_TAU = 16.0
GLA_CHUNK = 64
GLA_KW = GLA_HEADS * GLA_DK
GLA_VW = GLA_HEADS * GLA_DV

CHUNK_GROUP = 32

EVEN_RQ, EVEN_RQD, EVEN_RK, EVEN_RKD, EVEN_RV, EVEN_RG, EVEN_DQ, EVEN_DK, EVEN_DV = range(0, 36, 4)
EVEN_SLABS = 9 * RET_WIDTH // LANES
GLA_IN_PADDED = 2 * GLA_KW + 2 * GLA_VW + LANES

ROW_TILE = 512
PREP_TILE = 512
COMBINE_TILE = 1024

NT_DIMS = (((1,), (1,)), ((), ()))
TN_DIMS = (((0,), (0,)), ((), ()))


def _params(*semantics):
    return pltpu.CompilerParams(dimension_semantics=semantics,
                                vmem_limit_bytes=VMEM_LIMIT_BYTES)


def _resident(a):
    zeros = (0,) * a.ndim
    return a, pl.BlockSpec(a.shape, lambda *_: zeros, pipeline_mode=pl.Buffered(1))


def _layer(w, l, rows=None, row_block=0):
    shape = (None, w.shape[1] if rows is None else rows, w.shape[2])
    return w, pl.BlockSpec(shape, lambda *_: (l, row_block, 0), pipeline_mode=pl.Buffered(1))


def _rmsnorm(x, g):
    return x * lax.rsqrt(jnp.mean(x * x, axis=-1, keepdims=True) + EPS) * g


def _norm_parts(x, g):
    return (x * g).astype(BF16), lax.rsqrt(jnp.mean(x * x, axis=-1, keepdims=True) + EPS)


def _head_norm(o):
    mu = jnp.mean(o, axis=-1, keepdims=True)
    c = o - mu
    return c * lax.rsqrt(jnp.mean(c * c, axis=-1, keepdims=True) + EPS)


def _silu(x):
    return x * jax.nn.sigmoid(x)


def _mm(a, b):
    return jnp.dot(a, b, preferred_element_type=F32)


def _mm_nt(a, b):
    return lax.dot_general(a, b, NT_DIMS, preferred_element_type=F32)


def _mm_tn(a, b):
    return lax.dot_general(a, b, TN_DIMS, preferred_element_type=F32)


def _tile_rows(i, n):
    return pl.ds(pl.multiple_of(i * n, n), n)


def _lane_block(j, n=1):
    return slice(j * LANES, (j + n) * LANES)


def _ffn_kernel(*refs, n_mix, final):
    x_ref = refs[0]
    mix_refs = refs[1:1 + 2 * n_mix]
    g_ref, wg_ref, wu_ref, wd_ref = refs[1 + 2 * n_mix:5 + 2 * n_mix]
    o_ref = refs[-1]
    x = x_ref[...]
    for i in range(n_mix):
        mix_ref = mix_refs[2 * i]
        mix = jnp.concatenate([mix_ref[g] for g in range(mix_ref.shape[0])], axis=1)
        x = x + _mm(mix, mix_refs[2 * i + 1][...])
    xg, r = _norm_parts(x, g_ref[...])
    act = (_silu(_mm(xg, wg_ref[...]) * r) * (_mm(xg, wu_ref[...]) * r)).astype(BF16)
    y = x + 0.5 * _mm(act, wd_ref[...])
    if final:
        y = _rmsnorm(y, refs[-2][...])
    o_ref[...] = y


def _ffn(x, mixes, weights, final_g=None):
    t, d = x.shape
    row = lambda w: pl.BlockSpec((ROW_TILE, w), lambda i: (i, 0))
    args, specs = [x], [row(d)]
    for m, (w, spec) in mixes:
        args += [m, w]
        specs += [pl.BlockSpec((m.shape[0], ROW_TILE, m.shape[2]), lambda i: (0, i, 0)), spec]
    for w, spec in weights + ([_resident(final_g)] if final_g is not None else []):
        args.append(w)
        specs.append(spec)
    return pl.pallas_call(
        functools.partial(_ffn_kernel, n_mix=len(mixes), final=final_g is not None),
        grid=(t // ROW_TILE,),
        in_specs=specs,
        out_specs=row(d),
        out_shape=jax.ShapeDtypeStruct((t, d), F32),
        compiler_params=_params("parallel"),
        name="ffn",
    )(*args)


def _inproj_even_kernel(x_ref, g_ref, w_ref, rcos_ref, rsin_ref, qd_ref, kd_ref,
                        dcos_ref, dsina_ref, dsinb_ref, o_ref):
    xg, r = _norm_parts(x_ref[...], g_ref[...])
    proj = lambda first_block, n_blocks: _mm(xg, w_ref[:, _lane_block(first_block, n_blocks)]) * r
    heads = RET_WIDTH // LANES
    half = ROT_DIM // 2

    rcos, rsin = rcos_ref[...], rsin_ref[...]
    ret_rot = lambda x: x * rcos + pltpu.roll(x, RET_DK // 2, 1) * rsin
    q = proj(0, heads)
    for j in range(heads):
        qj = ret_rot(q[:, _lane_block(j)])
        o_ref[EVEN_RQ + j] = qj.astype(BF16)
        o_ref[EVEN_RQD + j] = (qj * qd_ref[:, _lane_block(j)]).astype(BF16)
    k = proj(heads, heads)
    for j in range(heads):
        kj = ret_rot(k[:, _lane_block(j)]) * (RET_DK ** -0.5)
        o_ref[EVEN_RK + j] = kj.astype(BF16)
        o_ref[EVEN_RKD + j] = (kj * kd_ref[:, _lane_block(j)]).astype(BF16)
    vg = proj(2 * heads, 2 * heads)
    for j in range(2 * heads):
        o_ref[EVEN_RV + j] = vg[:, _lane_block(j)].astype(BF16)

    dcos, dsina, dsinb = dcos_ref[...], dsina_ref[...], dsinb_ref[...]
    dil_rot = lambda x: x * dcos + pltpu.roll(x, LANES - half, 1) * dsina + pltpu.roll(x, half, 1) * dsinb
    pairs = DIL_WIDTH // LANES
    dq = proj(4 * heads, pairs)
    for j in range(pairs):
        o_ref[EVEN_DQ + j] = (dil_rot(dq[:, _lane_block(j)]) * (DIL_DH ** -0.5)).astype(BF16)
    dk = proj(4 * heads + pairs, pairs)
    for j in range(pairs):
        o_ref[EVEN_DK + j] = dil_rot(dk[:, _lane_block(j)]).astype(BF16)
    dv = proj(4 * heads + 2 * pairs, pairs)
    for j in range(pairs):
        o_ref[EVEN_DV + j] = dv[:, _lane_block(j)].astype(BF16)


def _even_tables(seq):
    c = RET_CHUNK
    log_g = np.log(1.0 - 2.0 ** (-5.0 - np.arange(RET_HEADS, dtype=np.float64)))
    i = np.arange(c, dtype=np.float64)
    rel = i[:, None] - i[None, :]
    dmask = np.where(rel >= 0, np.exp(log_g[:, None, None] * np.maximum(rel, 0.0)), 0.0)
    spread = lambda a: np.tile(np.repeat(a.T, RET_DK, axis=1), (ROW_TILE // c, 1))
    q_decay = spread(np.exp(log_g[:, None] * (i + 1.0)))
    k_decay = spread(np.exp(log_g[:, None] * (c - 1 - i)))
    chunk_decay = np.exp(log_g * c)[:, None, None] * np.ones((1, 1, RET_DV))
    pos = np.arange(seq, dtype=np.float32)[:, None]

    freq = np.float32(RET_THETA) ** (-np.linspace(0.0, 1.0, RET_DK // 2, dtype=np.float32))
    ang = pos * freq[None, :]
    rcos = np.concatenate([np.cos(ang), np.cos(ang)], axis=1)
    rsin = np.concatenate([-np.sin(ang), np.sin(ang)], axis=1)

    freq = np.float32(ROPE_THETA) ** (-np.arange(0, ROT_DIM, 2, dtype=np.float32) / np.float32(ROT_DIM))
    ang = pos * freq[None, :]
    cos, sin = np.cos(ang), np.sin(ang)
    zeros = np.zeros_like(sin)
    head = lambda a, b, fill: np.concatenate([a, b, np.full((seq, DIL_DH - ROT_DIM), fill, np.float32)], axis=1)
    two = lambda x: np.concatenate([x, x], axis=1)
    dcos, dsina, dsinb = two(head(cos, cos, 1.0)), two(head(-sin, zeros, 0.0)), two(head(zeros, sin, 0.0))
    f = lambda a: np.asarray(a, np.float32)
    return dict(rcos=f(rcos), rsin=f(rsin), q_decay=f(q_decay), k_decay=f(k_decay), dcos=f(dcos),
                dsina=f(dsina), dsinb=f(dsinb), dmask=f(dmask), chunk_decay=f(chunk_decay))


def _inproj_even(x, gain, weight, tables, seq):
    t, d = x.shape
    tiles_per_seq = seq // ROW_TILE
    by_position = lambda a: (a, pl.BlockSpec((ROW_TILE, LANES), lambda i: (i % tiles_per_seq, 0)))
    operands = [(x, pl.BlockSpec((ROW_TILE, d), lambda i: (i, 0))), gain, weight,
                by_position(tables["rcos"]), by_position(tables["rsin"]),
                _resident(tables["q_decay"]), _resident(tables["k_decay"]),
                by_position(tables["dcos"]), by_position(tables["dsina"]), by_position(tables["dsinb"])]
    return pl.pallas_call(
        _inproj_even_kernel,
        grid=(t // ROW_TILE,),
        in_specs=[spec for _, spec in operands],
        out_specs=pl.BlockSpec((EVEN_SLABS, ROW_TILE, LANES), lambda i: (0, i, 0)),
        out_shape=jax.ShapeDtypeStruct((EVEN_SLABS, t, LANES), BF16),
        compiler_params=_params("parallel"),
        name="inproj_even",
    )(*[a for a, _ in operands])


def _retention_kernel(q_ref, qdec_ref, k_ref, kdec_ref, v_ref, g_ref, dm_ref, cd_ref, o_ref):
    c = RET_CHUNK
    seq = q_ref.shape[0]
    dmask = dm_ref[...]
    chunk_decay = cd_ref[...]

    def body(n, state):
        rows = [_tile_rows(n * CHUNK_GROUP + g, c) for g in range(CHUNK_GROUP)]
        vs = [v_ref[r, :] for r in rows]
        scores = [_mm_nt(q_ref[r, :], k_ref[r, :]) for r in rows]
        kvs = [_mm_tn(kdec_ref[r, :], v) for r, v in zip(rows, vs)]
        for r, v, s, kv in zip(rows, vs, scores, kvs):
            lhs = jnp.concatenate([(s * dmask).astype(BF16), qdec_ref[r, :]], axis=1)
            rhs = jnp.concatenate([v, state.astype(BF16)], axis=0)
            o = _head_norm(_mm(lhs, rhs)) * _silu(g_ref[r, :].astype(F32))
            o_ref[r, :] = o.astype(BF16)
            state = state * chunk_decay + kv
        return state

    lax.fori_loop(0, seq // c // CHUNK_GROUP, body, jnp.zeros((RET_DK, RET_DV), F32))


def _retention(p, tables, batch, seq):
    head_block = lambda first: pl.BlockSpec((None, None, seq, RET_DK), lambda b, j: (first + j, b, 0, 0))
    per_head = lambda a: pl.BlockSpec((None,) + a.shape[1:], lambda b, j: (j, 0, 0))
    dmask, chunk_decay = tables["dmask"], tables["chunk_decay"]
    return pl.pallas_call(
        _retention_kernel,
        grid=(batch, RET_HEADS),
        in_specs=[head_block(EVEN_RQ), head_block(EVEN_RQD), head_block(EVEN_RK), head_block(EVEN_RKD),
                  head_block(EVEN_RV), head_block(EVEN_RG), per_head(dmask), per_head(chunk_decay)],
        out_specs=pl.BlockSpec((None, None, seq, RET_DV), lambda b, j: (j, b, 0, 0)),
        out_shape=jax.ShapeDtypeStruct((RET_HEADS, batch, seq, RET_DV), BF16),
        compiler_params=_params("parallel", "parallel"),
        name="retention",
    )(p, p, p, p, p, p, dmask, chunk_decay)


def _dilated_kernel(q_ref, k_ref, v_ref, o_ref, qs_ref, ks_ref, vs_ref, ob_ref, mb_ref, lb_ref):
    seq = q_ref.shape[0]
    c = DIL_BLOCK

    def stage(i, carry):
        rows = _tile_rows(i, PREP_TILE)
        qs_ref[rows, :] = q_ref[rows, :].astype(F32)
        ks_ref[rows, :] = k_ref[rows, :].astype(F32)
        vs_ref[rows, :] = v_ref[rows, :].astype(F32)
        return carry

    lax.fori_loop(0, seq // PREP_TILE, stage, 0)

    head0 = lax.broadcasted_iota(jnp.int32, (c, LANES), 1) < DIL_DH
    dist = lax.broadcasted_iota(jnp.int32, (c, 2 * c), 0) + c - lax.broadcasted_iota(jnp.int32, (c, 2 * c), 1)
    band = (dist >= 0) & (dist <= c)
    causal = lax.broadcasted_iota(jnp.int32, (c, c), 0) >= lax.broadcasted_iota(jnp.int32, (c, c), 1)

    def class_rows(r, dil):
        return pl.ds(r, seq // dil, stride=dil) if dil > 1 else pl.ds(0, seq)

    def tile_keys(strip, u):
        return strip[max(u - 1, 0) * c:(u + 1) * c]

    def branch(dil):
        blocks = seq // dil // c
        scores = []
        for r in range(dil):
            q = qs_ref[class_rows(r, dil), :]
            k = ks_ref[class_rows(r, dil), :].astype(BF16)
            for u in range(blocks):
                qu = q[u * c:(u + 1) * c]
                for qh in (jnp.where(head0, qu, 0.0).astype(BF16), jnp.where(head0, 0.0, qu).astype(BF16)):
                    scores.append(_mm_nt(qh, tile_keys(k, u)))
        scores = iter(scores)
        for r in range(dil):
            v = vs_ref[class_rows(r, dil), :]
            head0_kv = lax.broadcasted_iota(jnp.int32, v.shape, 1) < DIL_DH
            v_heads = (jnp.where(head0_kv, v, 1.0).astype(BF16), jnp.where(head0_kv, 1.0, v).astype(BF16))
            for u in range(blocks):
                valid = band if u > 0 else causal
                res, mx = [], []
                for vh in v_heads:
                    s = jnp.where(valid, next(scores), -jnp.inf)
                    m = jnp.max(s, axis=-1, keepdims=True)
                    res.append(_mm(jnp.exp(s - m).astype(BF16), tile_keys(vh, u)))
                    mx.append(jnp.broadcast_to(m, (c, LANES)))
                start = r + u * c * dil
                rows = pl.ds(start, c, stride=dil) if dil > 1 else pl.ds(start, c)
                yield (rows, jnp.where(head0, res[0], res[1]), jnp.where(head0, mx[0], mx[1]),
                       jnp.where(head0, res[1], res[0]))

    n_br = len(DIL_PATTERNS)
    for br, (window, dil) in enumerate(DIL_PATTERNS):
        assert window // dil == c
        for rows, o, m, l in branch(dil):
            ob_ref[br, rows, :] = o
            mb_ref[br, rows, :] = m
            lb_ref[br, rows, :] = l

    def combine(i, carry):
        rows = _tile_rows(i, COMBINE_TILE)
        ms = [mb_ref[b, rows, :] for b in range(n_br)]
        m = functools.reduce(jnp.maximum, ms)
        es = [jnp.exp(x - m) for x in ms]
        num = sum(es[b] * ob_ref[b, rows, :] for b in range(n_br))
        den = sum(es[b] * pltpu.roll(lb_ref[b, rows, :], DIL_DH, 1) for b in range(n_br))
        o_ref[rows, :] = (num / den).astype(BF16)
        return carry

    lax.fori_loop(0, seq // COMBINE_TILE, combine, 0)


def _dilated(p, batch, seq):
    pairs = DIL_WIDTH // LANES
    pair_block = lambda first: pl.BlockSpec((None, None, seq, LANES), lambda b, j: (first + j, b, 0, 0))
    n_br = len(DIL_PATTERNS)
    return pl.pallas_call(
        _dilated_kernel,
        grid=(batch, pairs),
        in_specs=[pair_block(EVEN_DQ), pair_block(EVEN_DK), pair_block(EVEN_DV)],
        out_specs=pl.BlockSpec((None, None, seq, LANES), lambda b, j: (j, b, 0, 0)),
        out_shape=jax.ShapeDtypeStruct((pairs, batch, seq, LANES), BF16),
        scratch_shapes=[pltpu.VMEM((seq, LANES), F32)] * 3 + [pltpu.VMEM((n_br, seq, LANES), F32)] * 3,
        compiler_params=_params("parallel", "parallel"),
        name="dilated",
    )(p, p, p)


def _inproj_gla_kernel(x_ref, g_ref, w_ref, wa_ref, ba_ref, qk_ref, vr_ref, dec_ref):
    c = GLA_CHUNK
    per_tile = ROW_TILE // c
    kw, vw = GLA_KW, GLA_VW
    xg, r = _norm_parts(x_ref[...], g_ref[...])
    proj = lambda lo, hi: _mm(xg, w_ref[:, lo:hi]) * r
    a_low = proj(2 * kw + 2 * vw, GLA_IN_PADDED).astype(BF16)
    z = _mm(a_low, wa_ref[...]) + ba_ref[...]
    slabs = []
    for j in range(2 * GLA_HEADS):
        slabs.append(proj(2 * kw + j * GLA_DV, 2 * kw + (j + 1) * GLA_DV))
        vr_ref[j] = slabs[j].astype(BF16)
    q = proj(0, kw) * (GLA_DK ** -0.5)
    k = proj(kw, 2 * kw)
    row = lax.broadcasted_iota(jnp.int32, (c, kw), 0)
    for j in range(per_tile):
        rows = slice(j * c, (j + 1) * c)
        anchor = 0.0 * slabs[j % len(slabs)][rows, :LANES]
        zj = z[rows] + jnp.concatenate([anchor] * (kw // LANES), axis=1)
        b = (jnp.minimum(zj, 0.0) - jnp.log1p(jnp.exp(-jnp.abs(zj)))) * (1.0 / GLA_TAU)
        shift = 1
        while shift < c:
            b = b + jnp.where(row >= shift, pltpu.roll(b, shift, 0), 0.0)
            shift *= 2
        b_last = b[c - 1:c, :]
        dec_ref[j] = jnp.exp(b_last)
        operands = (q[rows] * jnp.exp(b), k[rows] * jnp.exp(-b), k[rows] * jnp.exp(b_last - b))
        for i, operand in enumerate(operands):
            for h in range(GLA_HEADS):
                qk_ref[i * GLA_HEADS + h, rows, :] = operand[:, h * GLA_DK:(h + 1) * GLA_DK].astype(BF16)


def _inproj_gla(x, gain, weight, wa, ba):
    t, d = x.shape
    per_tile = ROW_TILE // GLA_CHUNK
    operands = [(x, pl.BlockSpec((ROW_TILE, d), lambda i: (i, 0))), gain, weight, wa, ba]
    return pl.pallas_call(
        _inproj_gla_kernel,
        grid=(t // ROW_TILE,),
        in_specs=[spec for _, spec in operands],
        out_specs=[pl.BlockSpec((3 * GLA_HEADS, ROW_TILE, GLA_DK), lambda i: (0, i, 0)),
                   pl.BlockSpec((2 * GLA_HEADS, ROW_TILE, GLA_DV), lambda i: (0, i, 0)),
                   pl.BlockSpec((per_tile, 1, GLA_KW), lambda i: (i, 0, 0))],
        out_shape=[jax.ShapeDtypeStruct((3 * GLA_HEADS, t, GLA_DK), BF16),
                   jax.ShapeDtypeStruct((2 * GLA_HEADS, t, GLA_DV), BF16),
                   jax.ShapeDtypeStruct((t // GLA_CHUNK, 1, GLA_KW), F32)],
        compiler_params=_params("parallel"),
        name="inproj_gla",
    )(*[a for a, _ in operands])


def _gla_kernel(qt_ref, kt_ref, kl_ref, v_ref, r_ref, dec_ref, o_ref):
    c = GLA_CHUNK
    seq = qt_ref.shape[0]
    causal = lax.broadcasted_iota(jnp.int32, (c, c), 0) >= lax.broadcasted_iota(jnp.int32, (c, c), 1)

    def body(n, state):
        chunks = [n * CHUNK_GROUP + g for g in range(CHUNK_GROUP)]
        rows = [_tile_rows(j, c) for j in chunks]
        vs = [v_ref[r, :] for r in rows]
        qts = [qt_ref[r, :] for r in rows]
        scores = [_mm_nt(q_t, kt_ref[r, :]) for r, q_t in zip(rows, qts)]
        kvs = [_mm_tn(v, kl_ref[r, :]) for r, v in zip(rows, vs)]
        for j, r, v, q_t, s, kv in zip(chunks, rows, vs, qts, scores, kvs):
            o = _mm(jnp.where(causal, s, 0.0).astype(BF16), v) + _mm_nt(q_t, state.astype(BF16))
            o = _head_norm(o) * _silu(r_ref[r, :].astype(F32))
            o_ref[r, :] = o.astype(BF16)
            state = state * dec_ref[j] + kv
        return state

    lax.fori_loop(0, seq // c // CHUNK_GROUP, body, jnp.zeros((GLA_DV, GLA_DK), F32))


def _gla(qk, vr, dec, batch, seq):
    h = GLA_HEADS
    slab = lambda first, width: pl.BlockSpec((None, None, seq, width), lambda b, j: (first + j, b, 0, 0))
    return pl.pallas_call(
        _gla_kernel,
        grid=(batch, h),
        in_specs=[slab(0, GLA_DK), slab(h, GLA_DK), slab(2 * h, GLA_DK), slab(0, GLA_DV), slab(h, GLA_DV),
                  pl.BlockSpec((None, seq // GLA_CHUNK, 1, GLA_DK), lambda b, j: (b, 0, 0, j))],
        out_specs=slab(0, GLA_DV),
        out_shape=jax.ShapeDtypeStruct((h, batch, seq, GLA_DV), BF16),
        compiler_params=_params("parallel", "parallel"),
        name="gla",
    )(qk, qk, qk, vr, vr, dec)


def kernel(x, ffn_pre_norm, ffn_pre_w_gate, ffn_pre_w_up, ffn_pre_w_down, mix_norm, ab_w_in, ab_w_out,
           gla_w_in, gla_w_a2, gla_b_a, gla_w_out, ffn_post_norm, ffn_post_w_gate, ffn_post_w_up,
           ffn_post_w_down, final_norm):
    batch, seq, d = x.shape
    depth = ffn_pre_norm.shape[0]
    bf = lambda w: w.astype(BF16)
    gains = lambda g: g.reshape(g.shape[0], 1, d)
    assert RET_HEADS * RET_DV == DIL_WIDTH

    pre = [gains(ffn_pre_norm), bf(ffn_pre_w_gate), bf(ffn_pre_w_up), bf(ffn_pre_w_down)]
    post = [gains(ffn_post_norm), bf(ffn_post_w_gate), bf(ffn_post_w_up), bf(ffn_post_w_down)]
    mix_g = gains(mix_norm)
    ab_in, ab_out = bf(ab_w_in), bf(ab_w_out)
    gla_in = jnp.pad(bf(gla_w_in), ((0, 0), (0, 0), (0, GLA_IN_PADDED - gla_w_in.shape[2])))
    gla_a2 = jnp.pad(bf(gla_w_a2), ((0, 0), (0, LANES - gla_w_a2.shape[1]), (0, 0)))
    gla_ba = gla_b_a.reshape(gla_b_a.shape[0], 1, -1)
    gla_out = bf(gla_w_out)
    tables = _even_tables(seq)

    xt = x.reshape(batch * seq, d)
    for l in range(depth):
        xt = _ffn(xt, [], [_layer(w, l) for w in pre])
        if l % 2 == 0:
            p = _inproj_even(xt, _layer(mix_g, l), _layer(ab_in, l // 2), tables, seq)
            p = p.reshape(EVEN_SLABS, batch, seq, LANES)
            o_r = _retention(p, tables, batch, seq).reshape(RET_HEADS, batch * seq, RET_DV)
            o_d = _dilated(p, batch, seq).reshape(DIL_WIDTH // LANES, batch * seq, LANES)
            mixes = [(o_r, _layer(ab_out, l // 2, rows=DIL_WIDTH, row_block=0)),
                     (o_d, _layer(ab_out, l // 2, rows=DIL_WIDTH, row_block=1))]
        else:
            qk, vr, dec = _inproj_gla(xt, _layer(mix_g, l), _layer(gla_in, l // 2), _layer(gla_a2, l // 2),
                                      _layer(gla_ba, l // 2))
            o = _gla(qk.reshape(3 * GLA_HEADS, batch, seq, GLA_DK), vr.reshape(2 * GLA_HEADS, batch, seq, GLA_DV),
                     dec.reshape(batch, seq // GLA_CHUNK, 1, GLA_KW), batch, seq)
            mixes = [(o.reshape(GLA_HEADS, batch * seq, GLA_DV), _layer(gla_out, l // 2))]
        xt = _ffn(xt, mixes, [_layer(w, l) for w in post],
                  final_g=final_norm.reshape(1, d) if l == depth - 1 else None)
    return xt.reshape(batch, seq, d)
```

```python
import functools

import jax
import jax.numpy as jnp
import numpy as np
from jax import lax
from jax.experimental import pallas as pl
from jax.experimental.pallas import tpu as pltpu

F32 = jnp.float32
BF16 = jnp.bfloat16

LANES = 128
VMEM_LIMIT_BYTES = 56 * 1024 * 1024

D_MODEL = 1024
FFN_HIDDEN = 2816
EPS = 1e-6

RET_HEADS = 4
RET_DK = 128
RET_DV = 128
RET_CHUNK = 128
RET_THETA = 10000.0
RET_WIDTH = RET_HEADS * RET_DK

DIL_HEADS = 8
DIL_DH = 64
DIL_PATTERNS = ((128, 1), (512, 4), (2048, 16))
DIL_BLOCK = 128
DIL_LOOKAHEAD = 3
ROPE_THETA = 500000.0
ROT_DIM = DIL_DH // 4
DIL_WIDTH = DIL_HEADS * DIL_DH

GLA_HEADS = 4
GLA_DK = (D_MODEL // 2) // GLA_HEADS
GLA_DV = D_MODEL // GLA_HEADS
GLA_LOWRANK = 16
GLA_TAU = 16.0
GLA_CHUNK = 64
GLA_KW = GLA_HEADS * GLA_DK
GLA_VW = GLA_HEADS * GLA_DV

CHUNK_GROUP = 32

EVEN_RQ, EVEN_RQD, EVEN_RK, EVEN_RKD, EVEN_RV, EVEN_RG, EVEN_DQ, EVEN_DK, EVEN_DV = range(0, 36, 4)
EVEN_SLABS = 9 * RET_WIDTH // LANES
GLA_IN_PADDED = 2 * GLA_KW + 2 * GLA_VW + LANES

ROW_TILE = 512
PREP_TILE = 512
COMBINE_TILE = 1024

NT_DIMS = (((1,), (1,)), ((), ()))
TN_DIMS = (((0,), (0,)), ((), ()))


def _params(*semantics):
    return pltpu.CompilerParams(dimension_semantics=semantics,
                                vmem_limit_bytes=VMEM_LIMIT_BYTES)


def _resident(a):
    zeros = (0,) * a.ndim
    return a, pl.BlockSpec(a.shape, lambda *_: zeros, pipeline_mode=pl.Buffered(1))


def _layer(w, l, rows=None, row_block=0):
    shape = (None, w.shape[1] if rows is None else rows, w.shape[2])
    return w, pl.BlockSpec(shape, lambda *_: (l, row_block, 0), pipeline_mode=pl.Buffered(1))


def _rmsnorm(x, g):
    return x * lax.rsqrt(jnp.mean(x * x, axis=-1, keepdims=True) + EPS) * g


def _norm_parts(x, g):
    return (x * g).astype(BF16), lax.rsqrt(jnp.mean(x * x, axis=-1, keepdims=True) + EPS)


def _head_norm(o):
    mu = jnp.mean(o, axis=-1, keepdims=True)
    c = o - mu
    return c * lax.rsqrt(jnp.mean(c * c, axis=-1, keepdims=True) + EPS)


def _silu(x):
    return x * jax.nn.sigmoid(x)


def _mm(a, b):
    return jnp.dot(a, b, preferred_element_type=F32)


def _mm_nt(a, b):
    return lax.dot_general(a, b, NT_DIMS, preferred_element_type=F32)


def _mm_tn(a, b):
    return lax.dot_general(a, b, TN_DIMS, preferred_element_type=F32)


def _tile_rows(i, n):
    return pl.ds(pl.multiple_of(i * n, n), n)


def _lane_block(j, n=1):
    return slice(j * LANES, (j + n) * LANES)


def _ffn_kernel(*refs, n_mix, final):
    x_ref = refs[0]
    mix_refs = refs[1:1 + 2 * n_mix]
    g_ref, wg_ref, wu_ref, wd_ref = refs[1 + 2 * n_mix:5 + 2 * n_mix]
    o_ref = refs[-1]
    x = x_ref[...]
    for i in range(n_mix):
        mix_ref = mix_refs[2 * i]
        mix = jnp.concatenate([mix_ref[g] for g in range(mix_ref.shape[0])], axis=1)
        x = x + _mm(mix, mix_refs[2 * i + 1][...])
    xg, r = _norm_parts(x, g_ref[...])
    act = (_silu(_mm(xg, wg_ref[...]) * r) * (_mm(xg, wu_ref[...]) * r)).astype(BF16)
    y = x + 0.5 * _mm(act, wd_ref[...])
    if final:
        y = _rmsnorm(y, refs[-2][...])
    o_ref[...] = y


def _ffn(x, mixes, weights, final_g=None):
    t, d = x.shape
    row = lambda w: pl.BlockSpec((ROW_TILE, w), lambda i: (i, 0))
    args, specs = [x], [row(d)]
    for m, (w, spec) in mixes:
        args += [m, w]
        specs += [pl.BlockSpec((m.shape[0], ROW_TILE, m.shape[2]), lambda i: (0, i, 0)), spec]
    for w, spec in weights + ([_resident(final_g)] if final_g is not None else []):
        args.append(w)
        specs.append(spec)
    return pl.pallas_call(
        functools.partial(_ffn_kernel, n_mix=len(mixes), final=final_g is not None),
        grid=(t // ROW_TILE,),
        in_specs=specs,
        out_specs=row(d),
        out_shape=jax.ShapeDtypeStruct((t, d), F32),
        compiler_params=_params("parallel"),
        name="ffn",
    )(*args)


def _inproj_even_kernel(x_ref, g_ref, w_ref, rcos_ref, rsin_ref, qd_ref, kd_ref,
                        dcos_ref, dsina_ref, dsinb_ref, o_ref):
    xg, r = _norm_parts(x_ref[...], g_ref[...])
    proj = lambda first_block, n_blocks: _mm(xg, w_ref[:, _lane_block(first_block, n_blocks)]) * r
    heads = RET_WIDTH // LANES
    half = ROT_DIM // 2

    rcos, rsin = rcos_ref[...], rsin_ref[...]
    ret_rot = lambda x: x * rcos + pltpu.roll(x, RET_DK // 2, 1) * rsin
    q = proj(0, heads)
    for j in range(heads):
        qj = ret_rot(q[:, _lane_block(j)])
        o_ref[EVEN_RQ + j] = qj.astype(BF16)
        o_ref[EVEN_RQD + j] = (qj * qd_ref[:, _lane_block(j)]).astype(BF16)
    k = proj(heads, heads)
    for j in range(heads):
        kj = ret_rot(k[:, _lane_block(j)]) * (RET_DK ** -0.5)
        o_ref[EVEN_RK + j] = kj.astype(BF16)
        o_ref[EVEN_RKD + j] = (kj * kd_ref[:, _lane_block(j)]).astype(BF16)
    vg = proj(2 * heads, 2 * heads)
    for j in range(2 * heads):
        o_ref[EVEN_RV + j] = vg[:, _lane_block(j)].astype(BF16)

    dcos, dsina, dsinb = dcos_ref[...], dsina_ref[...], dsinb_ref[...]
    dil_rot = lambda x: x * dcos + pltpu.roll(x, LANES - half, 1) * dsina + pltpu.roll(x, half, 1) * dsinb
    pairs = DIL_WIDTH // LANES
    dq = proj(4 * heads, pairs)
    for j in range(pairs):
        o_ref[EVEN_DQ + j] = (dil_rot(dq[:, _lane_block(j)]) * (DIL_DH ** -0.5)).astype(BF16)
    dk = proj(4 * heads + pairs, pairs)
    for j in range(pairs):
        o_ref[EVEN_DK + j] = dil_rot(dk[:, _lane_block(j)]).astype(BF16)
    dv = proj(4 * heads + 2 * pairs, pairs)
    for j in range(pairs):
        o_ref[EVEN_DV + j] = dv[:, _lane_block(j)].astype(BF16)


def _even_tables(seq):
    c = RET_CHUNK
    log_g = np.log(1.0 - 2.0 ** (-5.0 - np.arange(RET_HEADS, dtype=np.float64)))
    i = np.arange(c, dtype=np.float64)
    rel = i[:, None] - i[None, :]
    dmask = np.where(rel >= 0, np.exp(log_g[:, None, None] * np.maximum(rel, 0.0)), 0.0)
    spread = lambda a: np.tile(np.repeat(a.T, RET_DK, axis=1), (ROW_TILE // c, 1))
    q_decay = spread(np.exp(log_g[:, None] * (i + 1.0)))
    k_decay = spread(np.exp(log_g[:, None] * (c - 1 - i)))
    chunk_decay = np.exp(log_g * c)[:, None, None] * np.ones((1, 1, RET_DV))
    pos = np.arange(seq, dtype=np.float32)[:, None]

    freq = np.float32(RET_THETA) ** (-np.linspace(0.0, 1.0, RET_DK // 2, dtype=np.float32))
    ang = pos * freq[None, :]
    rcos = np.concatenate([np.cos(ang), np.cos(ang)], axis=1)
    rsin = np.concatenate([-np.sin(ang), np.sin(ang)], axis=1)

    freq = np.float32(ROPE_THETA) ** (-np.arange(0, ROT_DIM, 2, dtype=np.float32) / np.float32(ROT_DIM))
    ang = pos * freq[None, :]
    cos, sin = np.cos(ang), np.sin(ang)
    zeros = np.zeros_like(sin)
    head = lambda a, b, fill: np.concatenate([a, b, np.full((seq, DIL_DH - ROT_DIM), fill, np.float32)], axis=1)
    two = lambda x: np.concatenate([x, x], axis=1)
    dcos, dsina, dsinb = two(head(cos, cos, 1.0)), two(head(-sin, zeros, 0.0)), two(head(zeros, sin, 0.0))
    f = lambda a: np.asarray(a, np.float32)
    return dict(rcos=f(rcos), rsin=f(rsin), q_decay=f(q_decay), k_decay=f(k_decay), dcos=f(dcos),
                dsina=f(dsina), dsinb=f(dsinb), dmask=f(dmask), chunk_decay=f(chunk_decay))


def _inproj_even(x, gain, weight, tables, seq):
    t, d = x.shape
    tiles_per_seq = seq // ROW_TILE
    by_position = lambda a: (a, pl.BlockSpec((ROW_TILE, LANES), lambda i: (i % tiles_per_seq, 0)))
    operands = [(x, pl.BlockSpec((ROW_TILE, d), lambda i: (i, 0))), gain, weight,
                by_position(tables["rcos"]), by_position(tables["rsin"]),
                _resident(tables["q_decay"]), _resident(tables["k_decay"]),
                by_position(tables["dcos"]), by_position(tables["dsina"]), by_position(tables["dsinb"])]
    return pl.pallas_call(
        _inproj_even_kernel,
        grid=(t // ROW_TILE,),
        in_specs=[spec for _, spec in operands],
        out_specs=pl.BlockSpec((EVEN_SLABS, ROW_TILE, LANES), lambda i: (0, i, 0)),
        out_shape=jax.ShapeDtypeStruct((EVEN_SLABS, t, LANES), BF16),
        compiler_params=_params("parallel"),
        name="inproj_even",
    )(*[a for a, _ in operands])


def _retention_kernel(q_ref, qdec_ref, k_ref, kdec_ref, v_ref, g_ref, dm_ref, cd_ref, o_ref):
    c = RET_CHUNK
    seq = q_ref.shape[0]
    dmask = dm_ref[...]
    chunk_decay = cd_ref[...]

    def body(n, state):
        rows = [_tile_rows(n * CHUNK_GROUP + g, c) for g in range(CHUNK_GROUP)]
        vs = [v_ref[r, :] for r in rows]
        scores = [_mm_nt(q_ref[r, :], k_ref[r, :]) for r in rows]
        kvs = [_mm_tn(kdec_ref[r, :], v) for r, v in zip(rows, vs)]
        for r, v, s, kv in zip(rows, vs, scores, kvs):
            lhs = jnp.concatenate([(s * dmask).astype(BF16), qdec_ref[r, :]], axis=1)
            rhs = jnp.concatenate([v, state.astype(BF16)], axis=0)
            o = _head_norm(_mm(lhs, rhs)) * _silu(g_ref[r, :].astype(F32))
            o_ref[r, :] = o.astype(BF16)
            state = state * chunk_decay + kv
        return state

    lax.fori_loop(0, seq // c // CHUNK_GROUP, body, jnp.zeros((RET_DK, RET_DV), F32))


def _retention(p, tables, batch, seq):
    head_block = lambda first: pl.BlockSpec((None, None, seq, RET_DK), lambda b, j: (first + j, b, 0, 0))
    per_head = lambda a: pl.BlockSpec((None,) + a.shape[1:], lambda b, j: (j, 0, 0))
    dmask, chunk_decay = tables["dmask"], tables["chunk_decay"]
    return pl.pallas_call(
        _retention_kernel,
        grid=(batch, RET_HEADS),
        in_specs=[head_block(EVEN_RQ), head_block(EVEN_RQD), head_block(EVEN_RK), head_block(EVEN_RKD),
                  head_block(EVEN_RV), head_block(EVEN_RG), per_head(dmask), per_head(chunk_decay)],
        out_specs=pl.BlockSpec((None, None, seq, RET_DV), lambda b, j: (j, b, 0, 0)),
        out_shape=jax.ShapeDtypeStruct((RET_HEADS, batch, seq, RET_DV), BF16),
        compiler_params=_params("parallel", "parallel"),
        name="retention",
    )(p, p, p, p, p, p, dmask, chunk_decay)


def _dilated_kernel(q_ref, k_ref, v_ref, o_ref, qs_ref, ks_ref, vs_ref, ob_ref, mb_ref, lb_ref):
    seq = q_ref.shape[0]
    c = DIL_BLOCK

    def stage(i, carry):
        rows = _tile_rows(i, PREP_TILE)
        qs_ref[rows, :] = q_ref[rows, :].astype(F32)
        ks_ref[rows, :] = k_ref[rows, :].astype(F32)
        vs_ref[rows, :] = v_ref[rows, :].astype(F32)
        return carry

    lax.fori_loop(0, seq // PREP_TILE, stage, 0)

    head0 = lax.broadcasted_iota(jnp.int32, (c, LANES), 1) < DIL_DH
    dist = lax.broadcasted_iota(jnp.int32, (c, 2 * c), 0) + c - lax.broadcasted_iota(jnp.int32, (c, 2 * c), 1)
    band = (dist >= 0) & (dist <= c)
    causal = lax.broadcasted_iota(jnp.int32, (c, c), 0) >= lax.broadcasted_iota(jnp.int32, (c, c), 1)

    def class_rows(r, dil):
        return pl.ds(r, seq // dil, stride=dil) if dil > 1 else pl.ds(0, seq)

    def tile_keys(strip, u):
        return strip[max(u - 1, 0) * c:(u + 1) * c]

    def branch(dil):
        blocks = seq // dil // c
        strips = {}

        def strip(ref, r, prepare):
            if (id(ref), r) not in strips:
                strips[id(ref), r] = prepare(ref[class_rows(r, dil), :])
            return strips[id(ref), r]

        def by_head(x, other):
            first = lax.broadcasted_iota(jnp.int32, x.shape, 1) < DIL_DH
            return jnp.where(first, x, other).astype(BF16), jnp.where(first, other, x).astype(BF16)

        def issue(r, u):
            qu = strip(qs_ref, r, lambda q: q)[u * c:(u + 1) * c]
            ku = tile_keys(strip(ks_ref, r, lambda k: k.astype(BF16)), u)
            return [_mm_nt(qh, ku) for qh in by_head(qu, 0.0)]

        def finish(r, u, scores):
            valid = band if u > 0 else causal
            res, mx = [], []
            for s, vh in zip(scores, strip(vs_ref, r, lambda v: by_head(v, 1.0))):
                s = jnp.where(valid, s, -jnp.inf)
                m = jnp.max(s, axis=-1, keepdims=True)
                res.append(_mm(jnp.exp(s - m).astype(BF16), tile_keys(vh, u)))
                mx.append(jnp.broadcast_to(m, (c, LANES)))
            start = r + u * c * dil
            rows = pl.ds(start, c, stride=dil) if dil > 1 else pl.ds(start, c)
            return (rows, jnp.where(head0, res[0], res[1]), jnp.where(head0, mx[0], mx[1]),
                    jnp.where(head0, res[1], res[0]))

        tiles = [(r, u) for r in range(dil) for u in range(blocks)]
        issued = []
        for i, tile in enumerate(tiles):
            issued.append(issue(*tile))
            if i >= DIL_LOOKAHEAD:
                yield finish(*tiles[i - DIL_LOOKAHEAD], issued[i - DIL_LOOKAHEAD])
        for i in range(max(len(tiles) - DIL_LOOKAHEAD, 0), len(tiles)):
            yield finish(*tiles[i], issued[i])

    n_br = len(DIL_PATTERNS)
    for br, (window, dil) in enumerate(DIL_PATTERNS):
        assert window // dil == c
        for rows, o, m, l in branch(dil):
            ob_ref[br, rows, :] = o
            mb_ref[br, rows, :] = m
            lb_ref[br, rows, :] = l

    def combine(i, carry):
        rows = _tile_rows(i, COMBINE_TILE)
        ms = [mb_ref[b, rows, :] for b in range(n_br)]
        m = functools.reduce(jnp.maximum, ms)
        es = [jnp.exp(x - m) for x in ms]
        num = sum(es[b] * ob_ref[b, rows, :] for b in range(n_br))
        den = sum(es[b] * pltpu.roll(lb_ref[b, rows, :], DIL_DH, 1) for b in range(n_br))
        o_ref[rows, :] = (num / den).astype(BF16)
        return carry

    lax.fori_loop(0, seq // COMBINE_TILE, combine, 0)


def _dilated(p, batch, seq):
    pairs = DIL_WIDTH // LANES
    pair_block = lambda first: pl.BlockSpec((None, None, seq, LANES), lambda b, j: (first + j, b, 0, 0))
    n_br = len(DIL_PATTERNS)
    return pl.pallas_call(
        _dilated_kernel,
        grid=(batch, pairs),
        in_specs=[pair_block(EVEN_DQ), pair_block(EVEN_DK), pair_block(EVEN_DV)],
        out_specs=pl.BlockSpec((None, None, seq, LANES), lambda b, j: (j, b, 0, 0)),
        out_shape=jax.ShapeDtypeStruct((pairs, batch, seq, LANES), BF16),
        scratch_shapes=[pltpu.VMEM((seq, LANES), F32)] * 3 + [pltpu.VMEM((n_br, seq, LANES), F32)] * 3,
        compiler_params=_params("parallel", "parallel"),
        name="dilated",
    )(p, p, p)


def _inproj_gla_kernel(x_ref, g_ref, w_ref, wa_ref, ba_ref, qk_ref, vr_ref, dec_ref):
    c = GLA_CHUNK
    per_tile = ROW_TILE // c
    kw, vw = GLA_KW, GLA_VW
    xg, r = _norm_parts(x_ref[...], g_ref[...])
    proj = lambda lo, hi: _mm(xg, w_ref[:, lo:hi]) * r
    a_low = proj(2 * kw + 2 * vw, GLA_IN_PADDED).astype(BF16)
    z = _mm(a_low, wa_ref[...]) + ba_ref[...]
    slabs = []
    for j in range(2 * GLA_HEADS):
        slabs.append(proj(2 * kw + j * GLA_DV, 2 * kw + (j + 1) * GLA_DV))
        vr_ref[j] = slabs[j].astype(BF16)
    q = proj(0, kw) * (GLA_DK ** -0.5)
    k = proj(kw, 2 * kw)
    row = lax.broadcasted_iota(jnp.int32, (c, kw), 0)
    for j in range(per_tile):
        rows = slice(j * c, (j + 1) * c)
        anchor = 0.0 * slabs[j % len(slabs)][rows, :LANES]
        zj = z[rows] + jnp.concatenate([anchor] * (kw // LANES), axis=1)
        b = (jnp.minimum(zj, 0.0) - jnp.log1p(jnp.exp(-jnp.abs(zj)))) * (1.0 / GLA_TAU)
        shift = 1
        while shift < c:
            b = b + jnp.where(row >= shift, pltpu.roll(b, shift, 0), 0.0)
            shift *= 2
        b_last = b[c - 1:c, :]
        dec_ref[j] = jnp.exp(b_last)
        operands = (q[rows] * jnp.exp(b), k[rows] * jnp.exp(-b), k[rows] * jnp.exp(b_last - b))
        for i, operand in enumerate(operands):
            for h in range(GLA_HEADS):
                qk_ref[i * GLA_HEADS + h, rows, :] = operand[:, h * GLA_DK:(h + 1) * GLA_DK].astype(BF16)


def _inproj_gla(x, gain, weight, wa, ba):
    t, d = x.shape
    per_tile = ROW_TILE // GLA_CHUNK
    operands = [(x, pl.BlockSpec((ROW_TILE, d), lambda i: (i, 0))), gain, weight, wa, ba]
    return pl.pallas_call(
        _inproj_gla_kernel,
        grid=(t // ROW_TILE,),
        in_specs=[spec for _, spec in operands],
        out_specs=[pl.BlockSpec((3 * GLA_HEADS, ROW_TILE, GLA_DK), lambda i: (0, i, 0)),
                   pl.BlockSpec((2 * GLA_HEADS, ROW_TILE, GLA_DV), lambda i: (0, i, 0)),
                   pl.BlockSpec((per_tile, 1, GLA_KW), lambda i: (i, 0, 0))],
        out_shape=[jax.ShapeDtypeStruct((3 * GLA_HEADS, t, GLA_DK), BF16),
                   jax.ShapeDtypeStruct((2 * GLA_HEADS, t, GLA_DV), BF16),
                   jax.ShapeDtypeStruct((t // GLA_CHUNK, 1, GLA_KW), F32)],
        compiler_params=_params("parallel"),
        name="inproj_gla",
    )(*[a for a, _ in operands])


def _gla_kernel(qt_ref, kt_ref, kl_ref, v_ref, r_ref, dec_ref, o_ref):
    c = GLA_CHUNK
    seq = qt_ref.shape[0]
    causal = lax.broadcasted_iota(jnp.int32, (c, c), 0) >= lax.broadcasted_iota(jnp.int32, (c, c), 1)

    def body(n, state):
        chunks = [n * CHUNK_GROUP + g for g in range(CHUNK_GROUP)]
        rows = [_tile_rows(j, c) for j in chunks]
        vs = [v_ref[r, :] for r in rows]
        qts = [qt_ref[r, :] for r in rows]
        scores = [_mm_nt(q_t, kt_ref[r, :]) for r, q_t in zip(rows, qts)]
        kvs = [_mm_tn(v, kl_ref[r, :]) for r, v in zip(rows, vs)]
        for j, r, v, q_t, s, kv in zip(chunks, rows, vs, qts, scores, kvs):
            o = _mm(jnp.where(causal, s, 0.0).astype(BF16), v) + _mm_nt(q_t, state.astype(BF16))
            o = _head_norm(o) * _silu(r_ref[r, :].astype(F32))
            o_ref[r, :] = o.astype(BF16)
            state = state * dec_ref[j] + kv
        return state

    lax.fori_loop(0, seq // c // CHUNK_GROUP, body, jnp.zeros((GLA_DV, GLA_DK), F32))


def _gla(qk, vr, dec, batch, seq):
    h = GLA_HEADS
    slab = lambda first, width: pl.BlockSpec((None, None, seq, width), lambda b, j: (first + j, b, 0, 0))
    return pl.pallas_call(
        _gla_kernel,
        grid=(batch, h),
        in_specs=[slab(0, GLA_DK), slab(h, GLA_DK), slab(2 * h, GLA_DK), slab(0, GLA_DV), slab(h, GLA_DV),
                  pl.BlockSpec((None, seq // GLA_CHUNK, 1, GLA_DK), lambda b, j: (b, 0, 0, j))],
        out_specs=slab(0, GLA_DV),
        out_shape=jax.ShapeDtypeStruct((h, batch, seq, GLA_DV), BF16),
        compiler_params=_params("parallel", "parallel"),
        name="gla",
    )(qk, qk, qk, vr, vr, dec)


def kernel(x, ffn_pre_norm, ffn_pre_w_gate, ffn_pre_w_up, ffn_pre_w_down, mix_norm, ab_w_in, ab_w_out,
           gla_w_in, gla_w_a2, gla_b_a, gla_w_out, ffn_post_norm, ffn_post_w_gate, ffn_post_w_up,
           ffn_post_w_down, final_norm):
    batch, seq, d = x.shape
    depth = ffn_pre_norm.shape[0]
    bf = lambda w: w.astype(BF16)
    gains = lambda g: g.reshape(g.shape[0], 1, d)
    assert RET_HEADS * RET_DV == DIL_WIDTH

    pre = [gains(ffn_pre_norm), bf(ffn_pre_w_gate), bf(ffn_pre_w_up), bf(ffn_pre_w_down)]
    post = [gains(ffn_post_norm), bf(ffn_post_w_gate), bf(ffn_post_w_up), bf(ffn_post_w_down)]
    mix_g = gains(mix_norm)
    ab_in, ab_out = bf(ab_w_in), bf(ab_w_out)
    gla_in = jnp.pad(bf(gla_w_in), ((0, 0), (0, 0), (0, GLA_IN_PADDED - gla_w_in.shape[2])))
    gla_a2 = jnp.pad(bf(gla_w_a2), ((0, 0), (0, LANES - gla_w_a2.shape[1]), (0, 0)))
    gla_ba = gla_b_a.reshape(gla_b_a.shape[0], 1, -1)
    gla_out = bf(gla_w_out)
    tables = _even_tables(seq)

    xt = x.reshape(batch * seq, d)
    for l in range(depth):
        xt = _ffn(xt, [], [_layer(w, l) for w in pre])
        if l % 2 == 0:
            p = _inproj_even(xt, _layer(mix_g, l), _layer(ab_in, l // 2), tables, seq)
            p = p.reshape(EVEN_SLABS, batch, seq, LANES)
            o_r = _retention(p, tables, batch, seq).reshape(RET_HEADS, batch * seq, RET_DV)
            o_d = _dilated(p, batch, seq).reshape(DIL_WIDTH // LANES, batch * seq, LANES)
            mixes = [(o_r, _layer(ab_out, l // 2, rows=DIL_WIDTH, row_block=0)),
                     (o_d, _layer(ab_out, l // 2, rows=DIL_WIDTH, row_block=1))]
        else:
            qk, vr, dec = _inproj_gla(xt, _layer(mix_g, l), _layer(gla_in, l // 2), _layer(gla_a2, l // 2),
                                      _layer(gla_ba, l // 2))
            o = _gla(qk.reshape(3 * GLA_HEADS, batch, seq, GLA_DK), vr.reshape(2 * GLA_HEADS, batch, seq, GLA_DV),
                     dec.reshape(batch, seq // GLA_CHUNK, 1, GLA_KW), batch, seq)
            mixes = [(o.reshape(GLA_HEADS, batch * seq, GLA_DV), _layer(gla_out, l // 2))]
        xt = _ffn(xt, mixes, [_layer(w, l) for w in post],
                  final_g=final_norm.reshape(1, d) if l == depth - 1 else None)
    return xt.reshape(batch, seq, d)
```

```python
import functools

import jax
import jax.numpy as jnp
import numpy as np
from jax import lax
from jax.experimental import pallas as pl
from jax.experimental.pallas import tpu as pltpu

F32 = jnp.float32
BF16 = jnp.bfloat16

LANES = 128
VMEM_LIMIT_BYTES = 56 * 1024 * 1024

D_MODEL = 1024
FFN_HIDDEN = 2816
EPS = 1e-6

RET_HEADS = 4
RET_DK = 128
RET_DV = 128
RET_CHUNK = 128
RET_THETA = 10000.0
RET_WIDTH = RET_HEADS * RET_DK

DIL_HEADS = 8
DIL_DH = 64
DIL_PATTERNS = ((128, 1), (512, 4), (2048, 16))
DIL_BLOCK = 128
DIL_LOOKAHEAD = 3
ROPE_THETA = 500000.0
ROT_DIM = DIL_DH // 4
DIL_WIDTH = DIL_HEADS * DIL_DH

GLA_HEADS = 4
GLA_DK = (D_MODEL // 2) // GLA_HEADS
GLA_DV = D_MODEL // GLA_HEADS
GLA_LOWRANK = 16
GLA_TAU = 16.0
GLA_CHUNK = 64
GLA_KW = GLA_HEADS * GLA_DK
GLA_VW = GLA_HEADS * GLA_DV

CHUNK_GROUP = 32
CHUNK_LOOKAHEAD = 16

EVEN_RQ, EVEN_RQD, EVEN_RK, EVEN_RKD, EVEN_RV, EVEN_RG, EVEN_DQ, EVEN_DK, EVEN_DV = range(0, 36, 4)
EVEN_SLABS = 9 * RET_WIDTH // LANES
GLA_IN_PADDED = 2 * GLA_KW + 2 * GLA_VW + LANES

ROW_TILE = 512
PREP_TILE = 512
COMBINE_TILE = 1024

NT_DIMS = (((1,), (1,)), ((), ()))
TN_DIMS = (((0,), (0,)), ((), ()))


def _params(*semantics):
    return pltpu.CompilerParams(dimension_semantics=semantics,
                                vmem_limit_bytes=VMEM_LIMIT_BYTES)


def _resident(a):
    zeros = (0,) * a.ndim
    return a, pl.BlockSpec(a.shape, lambda *_: zeros, pipeline_mode=pl.Buffered(1))


def _layer(w, l, rows=None, row_block=0):
    shape = (None, w.shape[1] if rows is None else rows, w.shape[2])
    return w, pl.BlockSpec(shape, lambda *_: (l, row_block, 0), pipeline_mode=pl.Buffered(1))


def _rmsnorm(x, g):
    return x * lax.rsqrt(jnp.mean(x * x, axis=-1, keepdims=True) + EPS) * g


def _norm_parts(x, g):
    return (x * g).astype(BF16), lax.rsqrt(jnp.mean(x * x, axis=-1, keepdims=True) + EPS)


def _head_norm(o):
    mu = jnp.mean(o, axis=-1, keepdims=True)
    c = o - mu
    return c * lax.rsqrt(jnp.mean(c * c, axis=-1, keepdims=True) + EPS)


def _silu(x):
    return x * jax.nn.sigmoid(x)


def _mm(a, b):
    return jnp.dot(a, b, preferred_element_type=F32)


def _mm_nt(a, b):
    return lax.dot_general(a, b, NT_DIMS, preferred_element_type=F32)


def _mm_tn(a, b):
    return lax.dot_general(a, b, TN_DIMS, preferred_element_type=F32)


def _tile_rows(i, n):
    return pl.ds(pl.multiple_of(i * n, n), n)


def _lane_block(j, n=1):
    return slice(j * LANES, (j + n) * LANES)


def _ffn_kernel(*refs, n_mix, final):
    x_ref = refs[0]
    mix_refs = refs[1:1 + 2 * n_mix]
    g_ref, wg_ref, wu_ref, wd_ref = refs[1 + 2 * n_mix:5 + 2 * n_mix]
    o_ref = refs[-1]
    x = x_ref[...]
    for i in range(n_mix):
        mix_ref = mix_refs[2 * i]
        mix = jnp.concatenate([mix_ref[g] for g in range(mix_ref.shape[0])], axis=1)
        x = x + _mm(mix, mix_refs[2 * i + 1][...])
    xg, r = _norm_parts(x, g_ref[...])
    act = (_silu(_mm(xg, wg_ref[...]) * r) * (_mm(xg, wu_ref[...]) * r)).astype(BF16)
    y = x + 0.5 * _mm(act, wd_ref[...])
    if final:
        y = _rmsnorm(y, refs[-2][...])
    o_ref[...] = y


def _ffn(x, mixes, weights, final_g=None):
    t, d = x.shape
    row = lambda w: pl.BlockSpec((ROW_TILE, w), lambda i: (i, 0))
    args, specs = [x], [row(d)]
    for m, (w, spec) in mixes:
        args += [m, w]
        specs += [pl.BlockSpec((m.shape[0], ROW_TILE, m.shape[2]), lambda i: (0, i, 0)), spec]
    for w, spec in weights + ([_resident(final_g)] if final_g is not None else []):
        args.append(w)
        specs.append(spec)
    return pl.pallas_call(
        functools.partial(_ffn_kernel, n_mix=len(mixes), final=final_g is not None),
        grid=(t // ROW_TILE,),
        in_specs=specs,
        out_specs=row(d),
        out_shape=jax.ShapeDtypeStruct((t, d), F32),
        compiler_params=_params("parallel"),
        name="ffn",
    )(*args)


def _inproj_even_kernel(x_ref, g_ref, w_ref, rcos_ref, rsin_ref, qd_ref, kd_ref,
                        dcos_ref, dsina_ref, dsinb_ref, o_ref):
    xg, r = _norm_parts(x_ref[...], g_ref[...])
    proj = lambda first_block, n_blocks: _mm(xg, w_ref[:, _lane_block(first_block, n_blocks)]) * r
    heads = RET_WIDTH // LANES
    half = ROT_DIM // 2

    rcos, rsin = rcos_ref[...], rsin_ref[...]
    ret_rot = lambda x: x * rcos + pltpu.roll(x, RET_DK // 2, 1) * rsin
    q = proj(0, heads)
    for j in range(heads):
        qj = ret_rot(q[:, _lane_block(j)])
        o_ref[EVEN_RQ + j] = qj.astype(BF16)
        o_ref[EVEN_RQD + j] = (qj * qd_ref[:, _lane_block(j)]).astype(BF16)
    k = proj(heads, heads)
    for j in range(heads):
        kj = ret_rot(k[:, _lane_block(j)]) * (RET_DK ** -0.5)
        o_ref[EVEN_RK + j] = kj.astype(BF16)
        o_ref[EVEN_RKD + j] = (kj * kd_ref[:, _lane_block(j)]).astype(BF16)
    vg = proj(2 * heads, 2 * heads)
    for j in range(2 * heads):
        o_ref[EVEN_RV + j] = vg[:, _lane_block(j)].astype(BF16)

    dcos, dsina, dsinb = dcos_ref[...], dsina_ref[...], dsinb_ref[...]
    dil_rot = lambda x: x * dcos + pltpu.roll(x, LANES - half, 1) * dsina + pltpu.roll(x, half, 1) * dsinb
    pairs = DIL_WIDTH // LANES
    dq = proj(4 * heads, pairs)
    for j in range(pairs):
        o_ref[EVEN_DQ + j] = (dil_rot(dq[:, _lane_block(j)]) * (DIL_DH ** -0.5)).astype(BF16)
    dk = proj(4 * heads + pairs, pairs)
    for j in range(pairs):
        o_ref[EVEN_DK + j] = dil_rot(dk[:, _lane_block(j)]).astype(BF16)
    dv = proj(4 * heads + 2 * pairs, pairs)
    for j in range(pairs):
        o_ref[EVEN_DV + j] = dv[:, _lane_block(j)].astype(BF16)


def _even_tables(seq):
    c = RET_CHUNK
    log_g = np.log(1.0 - 2.0 ** (-5.0 - np.arange(RET_HEADS, dtype=np.float64)))
    i = np.arange(c, dtype=np.float64)
    rel = i[:, None] - i[None, :]
    dmask = np.where(rel >= 0, np.exp(log_g[:, None, None] * np.maximum(rel, 0.0)), 0.0)
    spread = lambda a: np.tile(np.repeat(a.T, RET_DK, axis=1), (ROW_TILE // c, 1))
    q_decay = spread(np.exp(log_g[:, None] * (i + 1.0)))
    k_decay = spread(np.exp(log_g[:, None] * (c - 1 - i)))
    chunk_decay = np.exp(log_g * c)[:, None, None] * np.ones((1, 1, RET_DV))
    pos = np.arange(seq, dtype=np.float32)[:, None]

    freq = np.float32(RET_THETA) ** (-np.linspace(0.0, 1.0, RET_DK // 2, dtype=np.float32))
    ang = pos * freq[None, :]
    rcos = np.concatenate([np.cos(ang), np.cos(ang)], axis=1)
    rsin = np.concatenate([-np.sin(ang), np.sin(ang)], axis=1)

    freq = np.float32(ROPE_THETA) ** (-np.arange(0, ROT_DIM, 2, dtype=np.float32) / np.float32(ROT_DIM))
    ang = pos * freq[None, :]
    cos, sin = np.cos(ang), np.sin(ang)
    zeros = np.zeros_like(sin)
    head = lambda a, b, fill: np.concatenate([a, b, np.full((seq, DIL_DH - ROT_DIM), fill, np.float32)], axis=1)
    two = lambda x: np.concatenate([x, x], axis=1)
    dcos, dsina, dsinb = two(head(cos, cos, 1.0)), two(head(-sin, zeros, 0.0)), two(head(zeros, sin, 0.0))
    f = lambda a: np.asarray(a, np.float32)
    return dict(rcos=f(rcos), rsin=f(rsin), q_decay=f(q_decay), k_decay=f(k_decay), dcos=f(dcos),
                dsina=f(dsina), dsinb=f(dsinb), dmask=f(dmask), chunk_decay=f(chunk_decay))


def _inproj_even(x, gain, weight, tables, seq):
    t, d = x.shape
    tiles_per_seq = seq // ROW_TILE
    by_position = lambda a: (a, pl.BlockSpec((ROW_TILE, LANES), lambda i: (i % tiles_per_seq, 0)))
    operands = [(x, pl.BlockSpec((ROW_TILE, d), lambda i: (i, 0))), gain, weight,
                by_position(tables["rcos"]), by_position(tables["rsin"]),
                _resident(tables["q_decay"]), _resident(tables["k_decay"]),
                by_position(tables["dcos"]), by_position(tables["dsina"]), by_position(tables["dsinb"])]
    return pl.pallas_call(
        _inproj_even_kernel,
        grid=(t // ROW_TILE,),
        in_specs=[spec for _, spec in operands],
        out_specs=pl.BlockSpec((EVEN_SLABS, ROW_TILE, LANES), lambda i: (0, i, 0)),
        out_shape=jax.ShapeDtypeStruct((EVEN_SLABS, t, LANES), BF16),
        compiler_params=_params("parallel"),
        name="inproj_even",
    )(*[a for a, _ in operands])


def _retention_kernel(q_ref, qdec_ref, k_ref, kdec_ref, v_ref, g_ref, dm_ref, cd_ref, o_ref):
    c = RET_CHUNK
    seq = q_ref.shape[0]
    dmask = dm_ref[...]
    chunk_decay = cd_ref[...]

    def body(n, state):
        rows = [_tile_rows(n * CHUNK_GROUP + g, c) for g in range(CHUNK_GROUP)]

        def issue(r):
            v = v_ref[r, :]
            return v, _mm_nt(q_ref[r, :], k_ref[r, :]), _mm_tn(kdec_ref[r, :], v)

        def advance(r, v, s, kv, state):
            lhs = jnp.concatenate([(s * dmask).astype(BF16), qdec_ref[r, :]], axis=1)
            rhs = jnp.concatenate([v, state.astype(BF16)], axis=0)
            o = _head_norm(_mm(lhs, rhs)) * _silu(g_ref[r, :].astype(F32))
            o_ref[r, :] = o.astype(BF16)
            return state * chunk_decay + kv

        issued = []
        for g, r in enumerate(rows):
            issued.append(issue(r))
            if g >= CHUNK_LOOKAHEAD:
                state = advance(rows[g - CHUNK_LOOKAHEAD], *issued[g - CHUNK_LOOKAHEAD], state)
        for g in range(max(len(rows) - CHUNK_LOOKAHEAD, 0), len(rows)):
            state = advance(rows[g], *issued[g], state)
        return state

    lax.fori_loop(0, seq // c // CHUNK_GROUP, body, jnp.zeros((RET_DK, RET_DV), F32))


def _retention(p, tables, batch, seq):
    head_block = lambda first: pl.BlockSpec((None, None, seq, RET_DK), lambda b, j: (first + j, b, 0, 0))
    per_head = lambda a: pl.BlockSpec((None,) + a.shape[1:], lambda b, j: (j, 0, 0))
    dmask, chunk_decay = tables["dmask"], tables["chunk_decay"]
    return pl.pallas_call(
        _retention_kernel,
        grid=(batch, RET_HEADS),
        in_specs=[head_block(EVEN_RQ), head_block(EVEN_RQD), head_block(EVEN_RK), head_block(EVEN_RKD),
                  head_block(EVEN_RV), head_block(EVEN_RG), per_head(dmask), per_head(chunk_decay)],
        out_specs=pl.BlockSpec((None, None, seq, RET_DV), lambda b, j: (j, b, 0, 0)),
        out_shape=jax.ShapeDtypeStruct((RET_HEADS, batch, seq, RET_DV), BF16),
        compiler_params=_params("parallel", "parallel"),
        name="retention",
    )(p, p, p, p, p, p, dmask, chunk_decay)


def _dilated_kernel(q_ref, k_ref, v_ref, o_ref, qs_ref, ks_ref, vs_ref, ob_ref, mb_ref, lb_ref):
    seq = q_ref.shape[0]
    c = DIL_BLOCK

    def stage(i, carry):
        rows = _tile_rows(i, PREP_TILE)
        qs_ref[rows, :] = q_ref[rows, :].astype(F32)
        ks_ref[rows, :] = k_ref[rows, :].astype(F32)
        vs_ref[rows, :] = v_ref[rows, :].astype(F32)
        return carry

    lax.fori_loop(0, seq // PREP_TILE, stage, 0)

    head0 = lax.broadcasted_iota(jnp.int32, (c, LANES), 1) < DIL_DH
    dist = lax.broadcasted_iota(jnp.int32, (c, 2 * c), 0) + c - lax.broadcasted_iota(jnp.int32, (c, 2 * c), 1)
    band = (dist >= 0) & (dist <= c)
    causal = lax.broadcasted_iota(jnp.int32, (c, c), 0) >= lax.broadcasted_iota(jnp.int32, (c, c), 1)

    def class_rows(r, dil):
        return pl.ds(r, seq // dil, stride=dil) if dil > 1 else pl.ds(0, seq)

    def tile_keys(strip, u):
        return strip[max(u - 1, 0) * c:(u + 1) * c]

    def branch(dil):
        blocks = seq // dil // c
        strips = {}

        def strip(ref, r, prepare):
            if (id(ref), r) not in strips:
                strips[id(ref), r] = prepare(ref[class_rows(r, dil), :])
            return strips[id(ref), r]

        def by_head(x, other):
            first = lax.broadcasted_iota(jnp.int32, x.shape, 1) < DIL_DH
            return jnp.where(first, x, other).astype(BF16), jnp.where(first, other, x).astype(BF16)

        def issue(r, u):
            qu = strip(qs_ref, r, lambda q: q)[u * c:(u + 1) * c]
            ku = tile_keys(strip(ks_ref, r, lambda k: k.astype(BF16)), u)
            return [_mm_nt(qh, ku) for qh in by_head(qu, 0.0)]

        def finish(r, u, scores):
            valid = band if u > 0 else causal
            res, mx = [], []
            for s, vh in zip(scores, strip(vs_ref, r, lambda v: by_head(v, 1.0))):
                s = jnp.where(valid, s, -jnp.inf)
                m = jnp.max(s, axis=-1, keepdims=True)
                res.append(_mm(jnp.exp(s - m).astype(BF16), tile_keys(vh, u)))
                mx.append(jnp.broadcast_to(m, (c, LANES)))
            start = r + u * c * dil
            rows = pl.ds(start, c, stride=dil) if dil > 1 else pl.ds(start, c)
            return (rows, jnp.where(head0, res[0], res[1]), jnp.where(head0, mx[0], mx[1]),
                    jnp.where(head0, res[1], res[0]))

        tiles = [(r, u) for r in range(dil) for u in range(blocks)]
        issued = []
        for i, tile in enumerate(tiles):
            issued.append(issue(*tile))
            if i >= DIL_LOOKAHEAD:
                yield finish(*tiles[i - DIL_LOOKAHEAD], issued[i - DIL_LOOKAHEAD])
        for i in range(max(len(tiles) - DIL_LOOKAHEAD, 0), len(tiles)):
            yield finish(*tiles[i], issued[i])

    n_br = len(DIL_PATTERNS)
    for br, (window, dil) in enumerate(DIL_PATTERNS):
        assert window // dil == c
        for rows, o, m, l in branch(dil):
            ob_ref[br, rows, :] = o
            mb_ref[br, rows, :] = m
            lb_ref[br, rows, :] = l

    def combine(i, carry):
        rows = _tile_rows(i, COMBINE_TILE)
        ms = [mb_ref[b, rows, :] for b in range(n_br)]
        m = functools.reduce(jnp.maximum, ms)
        es = [jnp.exp(x - m) for x in ms]
        num = sum(es[b] * ob_ref[b, rows, :] for b in range(n_br))
        den = sum(es[b] * pltpu.roll(lb_ref[b, rows, :], DIL_DH, 1) for b in range(n_br))
        o_ref[rows, :] = (num / den).astype(BF16)
        return carry

    lax.fori_loop(0, seq // COMBINE_TILE, combine, 0)


def _dilated(p, batch, seq):
    pairs = DIL_WIDTH // LANES
    pair_block = lambda first: pl.BlockSpec((None, None, seq, LANES), lambda b, j: (first + j, b, 0, 0))
    n_br = len(DIL_PATTERNS)
    return pl.pallas_call(
        _dilated_kernel,
        grid=(batch, pairs),
        in_specs=[pair_block(EVEN_DQ), pair_block(EVEN_DK), pair_block(EVEN_DV)],
        out_specs=pl.BlockSpec((None, None, seq, LANES), lambda b, j: (j, b, 0, 0)),
        out_shape=jax.ShapeDtypeStruct((pairs, batch, seq, LANES), BF16),
        scratch_shapes=[pltpu.VMEM((seq, LANES), F32)] * 3 + [pltpu.VMEM((n_br, seq, LANES), F32)] * 3,
        compiler_params=_params("parallel", "parallel"),
        name="dilated",
    )(p, p, p)


def _inproj_gla_kernel(x_ref, g_ref, w_ref, wa_ref, ba_ref, qk_ref, vr_ref, dec_ref):
    c = GLA_CHUNK
    per_tile = ROW_TILE // c
    kw, vw = GLA_KW, GLA_VW
    xg, r = _norm_parts(x_ref[...], g_ref[...])
    proj = lambda lo, hi: _mm(xg, w_ref[:, lo:hi]) * r
    a_low = proj(2 * kw + 2 * vw, GLA_IN_PADDED).astype(BF16)
    z = _mm(a_low, wa_ref[...]) + ba_ref[...]
    slabs = []
    for j in range(2 * GLA_HEADS):
        slabs.append(proj(2 * kw + j * GLA_DV, 2 * kw + (j + 1) * GLA_DV))
        vr_ref[j] = slabs[j].astype(BF16)
    q = proj(0, kw) * (GLA_DK ** -0.5)
    k = proj(kw, 2 * kw)
    row = lax.broadcasted_iota(jnp.int32, (c, kw), 0)
    for j in range(per_tile):
        rows = slice(j * c, (j + 1) * c)
        anchor = 0.0 * slabs[j % len(slabs)][rows, :LANES]
        zj = z[rows] + jnp.concatenate([anchor] * (kw // LANES), axis=1)
        b = (jnp.minimum(zj, 0.0) - jnp.log1p(jnp.exp(-jnp.abs(zj)))) * (1.0 / GLA_TAU)
        shift = 1
        while shift < c:
            b = b + jnp.where(row >= shift, pltpu.roll(b, shift, 0), 0.0)
            shift *= 2
        b_last = b[c - 1:c, :]
        dec_ref[j] = jnp.exp(b_last)
        operands = (q[rows] * jnp.exp(b), k[rows] * jnp.exp(-b), k[rows] * jnp.exp(b_last - b))
        for i, operand in enumerate(operands):
            for h in range(GLA_HEADS):
                qk_ref[i * GLA_HEADS + h, rows, :] = operand[:, h * GLA_DK:(h + 1) * GLA_DK].astype(BF16)


def _inproj_gla(x, gain, weight, wa, ba):
    t, d = x.shape
    per_tile = ROW_TILE // GLA_CHUNK
    operands = [(x, pl.BlockSpec((ROW_TILE, d), lambda i: (i, 0))), gain, weight, wa, ba]
    return pl.pallas_call(
        _inproj_gla_kernel,
        grid=(t // ROW_TILE,),
        in_specs=[spec for _, spec in operands],
        out_specs=[pl.BlockSpec((3 * GLA_HEADS, ROW_TILE, GLA_DK), lambda i: (0, i, 0)),
                   pl.BlockSpec((2 * GLA_HEADS, ROW_TILE, GLA_DV), lambda i: (0, i, 0)),
                   pl.BlockSpec((per_tile, 1, GLA_KW), lambda i: (i, 0, 0))],
        out_shape=[jax.ShapeDtypeStruct((3 * GLA_HEADS, t, GLA_DK), BF16),
                   jax.ShapeDtypeStruct((2 * GLA_HEADS, t, GLA_DV), BF16),
                   jax.ShapeDtypeStruct((t // GLA_CHUNK, 1, GLA_KW), F32)],
        compiler_params=_params("parallel"),
        name="inproj_gla",
    )(*[a for a, _ in operands])


def _gla_kernel(qt_ref, kt_ref, kl_ref, v_ref, r_ref, dec_ref, o_ref):
    c = GLA_CHUNK
    seq = qt_ref.shape[0]
    causal = lax.broadcasted_iota(jnp.int32, (c, c), 0) >= lax.broadcasted_iota(jnp.int32, (c, c), 1)

    def body(n, state):
        chunks = [n * CHUNK_GROUP + g for g in range(CHUNK_GROUP)]

        def issue(j):
            r = _tile_rows(j, c)
            v, q_t = v_ref[r, :], qt_ref[r, :]
            return r, v, q_t, _mm_nt(q_t, kt_ref[r, :]), _mm_tn(v, kl_ref[r, :])

        def advance(j, r, v, q_t, s, kv, state):
            o = _mm(jnp.where(causal, s, 0.0).astype(BF16), v) + _mm_nt(q_t, state.astype(BF16))
            o = _head_norm(o) * _silu(r_ref[r, :].astype(F32))
            o_ref[r, :] = o.astype(BF16)
            return state * dec_ref[j] + kv

        issued = []
        for g, j in enumerate(chunks):
            issued.append(issue(j))
            if g >= CHUNK_LOOKAHEAD:
                state = advance(chunks[g - CHUNK_LOOKAHEAD], *issued[g - CHUNK_LOOKAHEAD], state)
        for g in range(max(len(chunks) - CHUNK_LOOKAHEAD, 0), len(chunks)):
            state = advance(chunks[g], *issued[g], state)
        return state

    lax.fori_loop(0, seq // c // CHUNK_GROUP, body, jnp.zeros((GLA_DV, GLA_DK), F32))


def _gla(qk, vr, dec, batch, seq):
    h = GLA_HEADS
    slab = lambda first, width: pl.BlockSpec((None, None, seq, width), lambda b, j: (first + j, b, 0, 0))
    return pl.pallas_call(
        _gla_kernel,
        grid=(batch, h),
        in_specs=[slab(0, GLA_DK), slab(h, GLA_DK), slab(2 * h, GLA_DK), slab(0, GLA_DV), slab(h, GLA_DV),
                  pl.BlockSpec((None, seq // GLA_CHUNK, 1, GLA_DK), lambda b, j: (b, 0, 0, j))],
        out_specs=slab(0, GLA_DV),
        out_shape=jax.ShapeDtypeStruct((h, batch, seq, GLA_DV), BF16),
        compiler_params=_params("parallel", "parallel"),
        name="gla",
    )(qk, qk, qk, vr, vr, dec)


def kernel(x, ffn_pre_norm, ffn_pre_w_gate, ffn_pre_w_up, ffn_pre_w_down, mix_norm, ab_w_in, ab_w_out,
           gla_w_in, gla_w_a2, gla_b_a, gla_w_out, ffn_post_norm, ffn_post_w_gate, ffn_post_w_up,
           ffn_post_w_down, final_norm):
    batch, seq, d = x.shape
    depth = ffn_pre_norm.shape[0]
    bf = lambda w: w.astype(BF16)
    gains = lambda g: g.reshape(g.shape[0], 1, d)
    assert RET_HEADS * RET_DV == DIL_WIDTH

    pre = [gains(ffn_pre_norm), bf(ffn_pre_w_gate), bf(ffn_pre_w_up), bf(ffn_pre_w_down)]
    post = [gains(ffn_post_norm), bf(ffn_post_w_gate), bf(ffn_post_w_up), bf(ffn_post_w_down)]
    mix_g = gains(mix_norm)
    ab_in, ab_out = bf(ab_w_in), bf(ab_w_out)
    gla_in = jnp.pad(bf(gla_w_in), ((0, 0), (0, 0), (0, GLA_IN_PADDED - gla_w_in.shape[2])))
    gla_a2 = jnp.pad(bf(gla_w_a2), ((0, 0), (0, LANES - gla_w_a2.shape[1]), (0, 0)))
    gla_ba = gla_b_a.reshape(gla_b_a.shape[0], 1, -1)
    gla_out = bf(gla_w_out)
    tables = _even_tables(seq)

    xt = x.reshape(batch * seq, d)
    for l in range(depth):
        xt = _ffn(xt, [], [_layer(w, l) for w in pre])
        if l % 2 == 0:
            p = _inproj_even(xt, _layer(mix_g, l), _layer(ab_in, l // 2), tables, seq)
            p = p.reshape(EVEN_SLABS, batch, seq, LANES)
            o_r = _retention(p, tables, batch, seq).reshape(RET_HEADS, batch * seq, RET_DV)
            o_d = _dilated(p, batch, seq).reshape(DIL_WIDTH // LANES, batch * seq, LANES)
            mixes = [(o_r, _layer(ab_out, l // 2, rows=DIL_WIDTH, row_block=0)),
                     (o_d, _layer(ab_out, l // 2, rows=DIL_WIDTH, row_block=1))]
        else:
            qk, vr, dec = _inproj_gla(xt, _layer(mix_g, l), _layer(gla_in, l // 2), _layer(gla_a2, l // 2),
                                      _layer(gla_ba, l // 2))
            o = _gla(qk.reshape(3 * GLA_HEADS, batch, seq, GLA_DK), vr.reshape(2 * GLA_HEADS, batch, seq, GLA_DV),
                     dec.reshape(batch, seq // GLA_CHUNK, 1, GLA_KW), batch, seq)
            mixes = [(o.reshape(GLA_HEADS, batch * seq, GLA_DV), _layer(gla_out, l // 2))]
        xt = _ffn(xt, mixes, [_layer(w, l) for w in post],
                  final_g=final_norm.reshape(1, d) if l == depth - 1 else None)
    return xt.reshape(batch, seq, d)
```

```python
import functools

import jax
import jax.numpy as jnp
import numpy as np
from jax import lax
from jax.experimental import pallas as pl
from jax.experimental.pallas import tpu as pltpu

F32 = jnp.float32
BF16 = jnp.bfloat16

LANES = 128
VMEM_LIMIT_BYTES = 56 * 1024 * 1024

D_MODEL = 1024
FFN_HIDDEN = 2816
EPS = 1e-6

RET_HEADS = 4
RET_DK = 128
RET_DV = 128
RET_CHUNK = 128
RET_THETA = 10000.0
RET_WIDTH = RET_HEADS * RET_DK

DIL_HEADS = 8
DIL_DH = 64
DIL_PATTERNS = ((128, 1), (512, 4), (2048, 16))
DIL_BLOCK = 128
DIL_LOOKAHEAD = 3
ROPE_THETA = 500000.0
ROT_DIM = DIL_DH // 4
DIL_WIDTH = DIL_HEADS * DIL_DH

GLA_HEADS = 4
GLA_DK = (D_MODEL // 2) // GLA_HEADS
GLA_DV = D_MODEL // GLA_HEADS
GLA_LOWRANK = 16
GLA_TAU = 16.0
GLA_CHUNK = 64
GLA_KW = GLA_HEADS * GLA_DK
GLA_VW = GLA_HEADS * GLA_DV

CHUNK_GROUP = 32
CHUNK_LOOKAHEAD = 16

EVEN_RQ, EVEN_RQD, EVEN_RK, EVEN_RKD, EVEN_RV, EVEN_RG, EVEN_DQ, EVEN_DK, EVEN_DV = range(0, 36, 4)
EVEN_SLABS = 9 * RET_WIDTH // LANES
GLA_IN_PADDED = 2 * GLA_KW + 2 * GLA_VW + LANES

ROW_TILE = 512
PREP_TILE = 512
COMBINE_TILE = 1024

NT_DIMS = (((1,), (1,)), ((), ()))
TN_DIMS = (((0,), (0,)), ((), ()))


def _params(*semantics):
    return pltpu.CompilerParams(dimension_semantics=semantics,
                                vmem_limit_bytes=VMEM_LIMIT_BYTES)


def _resident(a):
    zeros = (0,) * a.ndim
    return a, pl.BlockSpec(a.shape, lambda *_: zeros, pipeline_mode=pl.Buffered(1))


def _layer(w, l, rows=None, row_block=0):
    shape = (None, w.shape[1] if rows is None else rows, w.shape[2])
    return w, pl.BlockSpec(shape, lambda *_: (l, row_block, 0), pipeline_mode=pl.Buffered(1))


def _rmsnorm(x, g):
    return x * lax.rsqrt(jnp.mean(x * x, axis=-1, keepdims=True) + EPS) * g


def _norm_parts(x, g):
    return (x * g).astype(BF16), lax.rsqrt(jnp.mean(x * x, axis=-1, keepdims=True) + EPS)


def _head_norm(o):
    mu = jnp.mean(o, axis=-1, keepdims=True)
    c = o - mu
    return c * lax.rsqrt(jnp.mean(c * c, axis=-1, keepdims=True) + EPS)


def _silu(x):
    return x * jax.nn.sigmoid(x)


def _mm(a, b):
    return jnp.dot(a, b, preferred_element_type=F32)


def _mm_nt(a, b):
    return lax.dot_general(a, b, NT_DIMS, preferred_element_type=F32)


def _mm_tn(a, b):
    return lax.dot_general(a, b, TN_DIMS, preferred_element_type=F32)


def _tile_rows(i, n):
    return pl.ds(pl.multiple_of(i * n, n), n)


def _lane_block(j, n=1):
    return slice(j * LANES, (j + n) * LANES)


def _ffn_kernel(*refs, n_mix, final):
    x_ref = refs[0]
    mix_refs = refs[1:1 + 2 * n_mix]
    g_ref, wg_ref, wu_ref, wd_ref = refs[1 + 2 * n_mix:5 + 2 * n_mix]
    o_ref = refs[-1]
    x = x_ref[...]
    for i in range(n_mix):
        mix_ref = mix_refs[2 * i]
        mix = jnp.concatenate([mix_ref[g] for g in range(mix_ref.shape[0])], axis=1)
        x = x + _mm(mix, mix_refs[2 * i + 1][...])
    xg, r = _norm_parts(x, g_ref[...])
    act = (_silu(_mm(xg, wg_ref[...]) * r) * (_mm(xg, wu_ref[...]) * r)).astype(BF16)
    y = x + 0.5 * _mm(act, wd_ref[...])
    if final:
        y = _rmsnorm(y, refs[-2][...])
    o_ref[...] = y


def _ffn(x, mixes, weights, final_g=None):
    t, d = x.shape
    row = lambda w: pl.BlockSpec((ROW_TILE, w), lambda i: (i, 0))
    args, specs = [x], [row(d)]
    for m, (w, spec) in mixes:
        args += [m, w]
        specs += [pl.BlockSpec((m.shape[0], ROW_TILE, m.shape[2]), lambda i: (0, i, 0)), spec]
    for w, spec in weights + ([_resident(final_g)] if final_g is not None else []):
        args.append(w)
        specs.append(spec)
    return pl.pallas_call(
        functools.partial(_ffn_kernel, n_mix=len(mixes), final=final_g is not None),
        grid=(t // ROW_TILE,),
        in_specs=specs,
        out_specs=row(d),
        out_shape=jax.ShapeDtypeStruct((t, d), F32),
        compiler_params=_params("parallel"),
        name="ffn",
    )(*args)


def _inproj_even_kernel(x_ref, g_ref, w_ref, rcos_ref, rsin_ref, qd_ref, kd_ref,
                        dcos_ref, dsina_ref, dsinb_ref, o_ref):
    xg, r = _norm_parts(x_ref[...], g_ref[...])
    proj = lambda first_block, n_blocks: _mm(xg, w_ref[:, _lane_block(first_block, n_blocks)]) * r
    heads = RET_WIDTH // LANES
    half = ROT_DIM // 2

    rcos, rsin = rcos_ref[...], rsin_ref[...]
    ret_rot = lambda x: x * rcos + pltpu.roll(x, RET_DK // 2, 1) * rsin
    q = proj(0, heads)
    for j in range(heads):
        qj = ret_rot(q[:, _lane_block(j)])
        o_ref[EVEN_RQ + j] = qj.astype(BF16)
        o_ref[EVEN_RQD + j] = (qj * qd_ref[:, _lane_block(j)]).astype(BF16)
    k = proj(heads, heads)
    for j in range(heads):
        kj = ret_rot(k[:, _lane_block(j)]) * (RET_DK ** -0.5)
        o_ref[EVEN_RK + j] = kj.astype(BF16)
        o_ref[EVEN_RKD + j] = (kj * kd_ref[:, _lane_block(j)]).astype(BF16)
    vg = proj(2 * heads, 2 * heads)
    for j in range(2 * heads):
        o_ref[EVEN_RV + j] = vg[:, _lane_block(j)].astype(BF16)

    dcos, dsina, dsinb = dcos_ref[...], dsina_ref[...], dsinb_ref[...]
    dil_rot = lambda x: x * dcos + pltpu.roll(x, LANES - half, 1) * dsina + pltpu.roll(x, half, 1) * dsinb
    pairs = DIL_WIDTH // LANES
    dq = proj(4 * heads, pairs)
    for j in range(pairs):
        o_ref[EVEN_DQ + j] = (dil_rot(dq[:, _lane_block(j)]) * (DIL_DH ** -0.5)).astype(BF16)
    dk = proj(4 * heads + pairs, pairs)
    for j in range(pairs):
        o_ref[EVEN_DK + j] = dil_rot(dk[:, _lane_block(j)]).astype(BF16)
    dv = proj(4 * heads + 2 * pairs, pairs)
    for j in range(pairs):
        o_ref[EVEN_DV + j] = dv[:, _lane_block(j)].astype(BF16)


def _even_tables(seq):
    c = RET_CHUNK
    log_g = np.log(1.0 - 2.0 ** (-5.0 - np.arange(RET_HEADS, dtype=np.float64)))
    i = np.arange(c, dtype=np.float64)
    rel = i[:, None] - i[None, :]
    dmask = np.where(rel >= 0, np.exp(log_g[:, None, None] * np.maximum(rel, 0.0)), 0.0)
    spread = lambda a: np.tile(np.repeat(a.T, RET_DK, axis=1), (ROW_TILE // c, 1))
    q_decay = spread(np.exp(log_g[:, None] * (i + 1.0)))
    k_decay = spread(np.exp(log_g[:, None] * (c - 1 - i)))
    chunk_decay = np.exp(log_g * c)[:, None, None] * np.ones((1, 1, RET_DV))
    pos = np.arange(seq, dtype=np.float32)[:, None]

    freq = np.float32(RET_THETA) ** (-np.linspace(0.0, 1.0, RET_DK // 2, dtype=np.float32))
    ang = pos * freq[None, :]
    rcos = np.concatenate([np.cos(ang), np.cos(ang)], axis=1)
    rsin = np.concatenate([-np.sin(ang), np.sin(ang)], axis=1)

    freq = np.float32(ROPE_THETA) ** (-np.arange(0, ROT_DIM, 2, dtype=np.float32) / np.float32(ROT_DIM))
    ang = pos * freq[None, :]
    cos, sin = np.cos(ang), np.sin(ang)
    zeros = np.zeros_like(sin)
    head = lambda a, b, fill: np.concatenate([a, b, np.full((seq, DIL_DH - ROT_DIM), fill, np.float32)], axis=1)
    two = lambda x: np.concatenate([x, x], axis=1)
    dcos, dsina, dsinb = two(head(cos, cos, 1.0)), two(head(-sin, zeros, 0.0)), two(head(zeros, sin, 0.0))
    f = lambda a: np.asarray(a, np.float32)
    return dict(rcos=f(rcos), rsin=f(rsin), q_decay=f(q_decay), k_decay=f(k_decay), dcos=f(dcos),
                dsina=f(dsina), dsinb=f(dsinb), dmask=f(dmask), chunk_decay=f(chunk_decay))


def _inproj_even(x, gain, weight, tables, seq):
    t, d = x.shape
    tiles_per_seq = seq // ROW_TILE
    by_position = lambda a: (a, pl.BlockSpec((ROW_TILE, LANES), lambda i: (i % tiles_per_seq, 0)))
    operands = [(x, pl.BlockSpec((ROW_TILE, d), lambda i: (i, 0))), gain, weight,
                by_position(tables["rcos"]), by_position(tables["rsin"]),
                _resident(tables["q_decay"]), _resident(tables["k_decay"]),
                by_position(tables["dcos"]), by_position(tables["dsina"]), by_position(tables["dsinb"])]
    return pl.pallas_call(
        _inproj_even_kernel,
        grid=(t // ROW_TILE,),
        in_specs=[spec for _, spec in operands],
        out_specs=pl.BlockSpec((EVEN_SLABS, ROW_TILE, LANES), lambda i: (0, i, 0)),
        out_shape=jax.ShapeDtypeStruct((EVEN_SLABS, t, LANES), BF16),
        compiler_params=_params("parallel"),
        name="inproj_even",
    )(*[a for a, _ in operands])


def _retention_kernel(q_ref, qdec_ref, k_ref, kdec_ref, v_ref, g_ref, dm_ref, cd_ref, o_ref):
    c = RET_CHUNK
    seq = q_ref.shape[0]
    dmask = dm_ref[...]
    chunk_decay = cd_ref[...]

    def body(n, state):
        rows = [_tile_rows(n * CHUNK_GROUP + g, c) for g in range(CHUNK_GROUP)]

        def issue(r):
            v = v_ref[r, :]
            return v, _mm_nt(q_ref[r, :], k_ref[r, :]), _mm_tn(kdec_ref[r, :], v)

        def advance(r, v, s, kv, state):
            lhs = jnp.concatenate([(s * dmask).astype(BF16), qdec_ref[r, :]], axis=1)
            rhs = jnp.concatenate([v, state.astype(BF16)], axis=0)
            o = _head_norm(_mm(lhs, rhs)) * _silu(g_ref[r, :].astype(F32))
            o_ref[r, :] = o.astype(BF16)
            return state * chunk_decay + kv

        issued = []
        for g, r in enumerate(rows):
            issued.append(issue(r))
            if g >= CHUNK_LOOKAHEAD:
                state = advance(rows[g - CHUNK_LOOKAHEAD], *issued[g - CHUNK_LOOKAHEAD], state)
        for g in range(max(len(rows) - CHUNK_LOOKAHEAD, 0), len(rows)):
            state = advance(rows[g], *issued[g], state)
        return state

    lax.fori_loop(0, seq // c // CHUNK_GROUP, body, jnp.zeros((RET_DK, RET_DV), F32))


def _retention(p, tables, batch, seq):
    head_block = lambda first: pl.BlockSpec((None, None, seq, RET_DK), lambda b, j: (first + j, b, 0, 0))
    per_head = lambda a: pl.BlockSpec((None,) + a.shape[1:], lambda b, j: (j, 0, 0))
    dmask, chunk_decay = tables["dmask"], tables["chunk_decay"]
    return pl.pallas_call(
        _retention_kernel,
        grid=(batch, RET_HEADS),
        in_specs=[head_block(EVEN_RQ), head_block(EVEN_RQD), head_block(EVEN_RK), head_block(EVEN_RKD),
                  head_block(EVEN_RV), head_block(EVEN_RG), per_head(dmask), per_head(chunk_decay)],
        out_specs=pl.BlockSpec((None, None, seq, RET_DV), lambda b, j: (j, b, 0, 0)),
        out_shape=jax.ShapeDtypeStruct((RET_HEADS, batch, seq, RET_DV), BF16),
        compiler_params=_params("parallel", "parallel"),
        name="retention",
    )(p, p, p, p, p, p, dmask, chunk_decay)


def _dilated_kernel(q_ref, k_ref, v_ref, o_ref, qs_ref, ks_ref, vs_ref, ob_ref, mb_ref, lb_ref):
    seq = q_ref.shape[0]
    c = DIL_BLOCK

    def stage(i, carry):
        rows = _tile_rows(i, PREP_TILE)
        qs_ref[rows, :] = q_ref[rows, :].astype(F32)
        ks_ref[rows, :] = k_ref[rows, :].astype(F32)
        vs_ref[rows, :] = v_ref[rows, :].astype(F32)
        return carry

    lax.fori_loop(0, seq // PREP_TILE, stage, 0)

    head0 = lax.broadcasted_iota(jnp.int32, (c, LANES), 1) < DIL_DH
    dist = lax.broadcasted_iota(jnp.int32, (c, 2 * c), 0) + c - lax.broadcasted_iota(jnp.int32, (c, 2 * c), 1)
    band = (dist >= 0) & (dist <= c)
    causal = lax.broadcasted_iota(jnp.int32, (c, c), 0) >= lax.broadcasted_iota(jnp.int32, (c, c), 1)
    band_bias = jnp.where(band, 0.0, -jnp.inf)
    causal_bias = jnp.where(causal, 0.0, -jnp.inf)

    def class_rows(r, dil):
        return pl.ds(r, seq // dil, stride=dil) if dil > 1 else pl.ds(0, seq)

    def tile_keys(strip, u):
        return strip[max(u - 1, 0) * c:(u + 1) * c]

    def branch(dil):
        blocks = seq // dil // c
        strips = {}

        def strip(ref, r, prepare):
            if (id(ref), r) not in strips:
                strips[id(ref), r] = prepare(ref[class_rows(r, dil), :])
            return strips[id(ref), r]

        def by_head(x, other):
            first = lax.broadcasted_iota(jnp.int32, x.shape, 1) < DIL_DH
            return jnp.where(first, x, other).astype(BF16), jnp.where(first, other, x).astype(BF16)

        def issue(r, u):
            qu = strip(qs_ref, r, lambda q: q)[u * c:(u + 1) * c]
            ku = tile_keys(strip(ks_ref, r, lambda k: k.astype(BF16)), u)
            bias = band_bias if u > 0 else causal_bias
            return [bias + _mm_nt(qh, ku) for qh in by_head(qu, 0.0)]

        def finish(r, u, scores):
            res, mx = [], []
            for s, vh in zip(scores, strip(vs_ref, r, lambda v: by_head(v, 1.0))):
                m = jnp.max(s, axis=-1, keepdims=True)
                res.append(_mm(jnp.exp(s - m).astype(BF16), tile_keys(vh, u)))
                mx.append(jnp.broadcast_to(m, (c, LANES)))
            start = r + u * c * dil
            rows = pl.ds(start, c, stride=dil) if dil > 1 else pl.ds(start, c)
            return (rows, jnp.where(head0, res[0], res[1]), jnp.where(head0, mx[0], mx[1]),
                    jnp.where(head0, res[1], res[0]))

        tiles = [(r, u) for r in range(dil) for u in range(blocks)]
        issued = []
        for i, tile in enumerate(tiles):
            issued.append(issue(*tile))
            if i >= DIL_LOOKAHEAD:
                yield finish(*tiles[i - DIL_LOOKAHEAD], issued[i - DIL_LOOKAHEAD])
        for i in range(max(len(tiles) - DIL_LOOKAHEAD, 0), len(tiles)):
            yield finish(*tiles[i], issued[i])

    n_br = len(DIL_PATTERNS)
    for br, (window, dil) in enumerate(DIL_PATTERNS):
        assert window // dil == c
        for rows, o, m, l in branch(dil):
            ob_ref[br, rows, :] = o
            mb_ref[br, rows, :] = m
            lb_ref[br, rows, :] = l

    def combine(i, carry):
        rows = _tile_rows(i, COMBINE_TILE)
        ms = [mb_ref[b, rows, :] for b in range(n_br)]
        m = functools.reduce(jnp.maximum, ms)
        es = [jnp.exp(x - m) for x in ms]
        num = sum(es[b] * ob_ref[b, rows, :] for b in range(n_br))
        den = sum(es[b] * pltpu.roll(lb_ref[b, rows, :], DIL_DH, 1) for b in range(n_br))
        o_ref[rows, :] = (num / den).astype(BF16)
        return carry

    lax.fori_loop(0, seq // COMBINE_TILE, combine, 0)


def _dilated(p, batch, seq):
    pairs = DIL_WIDTH // LANES
    pair_block = lambda first: pl.BlockSpec((None, None, seq, LANES), lambda b, j: (first + j, b, 0, 0))
    n_br = len(DIL_PATTERNS)
    return pl.pallas_call(
        _dilated_kernel,
        grid=(batch, pairs),
        in_specs=[pair_block(EVEN_DQ), pair_block(EVEN_DK), pair_block(EVEN_DV)],
        out_specs=pl.BlockSpec((None, None, seq, LANES), lambda b, j: (j, b, 0, 0)),
        out_shape=jax.ShapeDtypeStruct((pairs, batch, seq, LANES), BF16),
        scratch_shapes=[pltpu.VMEM((seq, LANES), F32)] * 3 + [pltpu.VMEM((n_br, seq, LANES), F32)] * 3,
        compiler_params=_params("parallel", "parallel"),
        name="dilated",
    )(p, p, p)


def _inproj_gla_kernel(x_ref, g_ref, w_ref, wa_ref, ba_ref, qk_ref, vr_ref, dec_ref):
    c = GLA_CHUNK
    per_tile = ROW_TILE // c
    kw, vw = GLA_KW, GLA_VW
    xg, r = _norm_parts(x_ref[...], g_ref[...])
    proj = lambda lo, hi: _mm(xg, w_ref[:, lo:hi]) * r
    a_low = proj(2 * kw + 2 * vw, GLA_IN_PADDED).astype(BF16)
    z = _mm(a_low, wa_ref[...]) + ba_ref[...]
    slabs = []
    for j in range(2 * GLA_HEADS):
        slabs.append(proj(2 * kw + j * GLA_DV, 2 * kw + (j + 1) * GLA_DV))
        vr_ref[j] = slabs[j].astype(BF16)
    q = proj(0, kw) * (GLA_DK ** -0.5)
    k = proj(kw, 2 * kw)
    row = lax.broadcasted_iota(jnp.int32, (c, kw), 0)
    for j in range(per_tile):
        rows = slice(j * c, (j + 1) * c)
        anchor = 0.0 * slabs[j % len(slabs)][rows, :LANES]
        zj = z[rows] + jnp.concatenate([anchor] * (kw // LANES), axis=1)
        b = (jnp.minimum(zj, 0.0) - jnp.log1p(jnp.exp(-jnp.abs(zj)))) * (1.0 / GLA_TAU)
        shift = 1
        while shift < c:
            b = b + jnp.where(row >= shift, pltpu.roll(b, shift, 0), 0.0)
            shift *= 2
        b_last = b[c - 1:c, :]
        dec_ref[j] = jnp.exp(b_last)
        operands = (q[rows] * jnp.exp(b), k[rows] * jnp.exp(-b), k[rows] * jnp.exp(b_last - b))
        for i, operand in enumerate(operands):
            for h in range(GLA_HEADS):
                qk_ref[i * GLA_HEADS + h, rows, :] = operand[:, h * GLA_DK:(h + 1) * GLA_DK].astype(BF16)


def _inproj_gla(x, gain, weight, wa, ba):
    t, d = x.shape
    per_tile = ROW_TILE // GLA_CHUNK
    operands = [(x, pl.BlockSpec((ROW_TILE, d), lambda i: (i, 0))), gain, weight, wa, ba]
    return pl.pallas_call(
        _inproj_gla_kernel,
        grid=(t // ROW_TILE,),
        in_specs=[spec for _, spec in operands],
        out_specs=[pl.BlockSpec((3 * GLA_HEADS, ROW_TILE, GLA_DK), lambda i: (0, i, 0)),
                   pl.BlockSpec((2 * GLA_HEADS, ROW_TILE, GLA_DV), lambda i: (0, i, 0)),
                   pl.BlockSpec((per_tile, 1, GLA_KW), lambda i: (i, 0, 0))],
        out_shape=[jax.ShapeDtypeStruct((3 * GLA_HEADS, t, GLA_DK), BF16),
                   jax.ShapeDtypeStruct((2 * GLA_HEADS, t, GLA_DV), BF16),
                   jax.ShapeDtypeStruct((t // GLA_CHUNK, 1, GLA_KW), F32)],
        compiler_params=_params("parallel"),
        name="inproj_gla",
    )(*[a for a, _ in operands])


def _gla_kernel(qt_ref, kt_ref, kl_ref, v_ref, r_ref, dec_ref, o_ref):
    c = GLA_CHUNK
    seq = qt_ref.shape[0]
    causal = lax.broadcasted_iota(jnp.int32, (c, c), 0) >= lax.broadcasted_iota(jnp.int32, (c, c), 1)

    def body(n, state):
        chunks = [n * CHUNK_GROUP + g for g in range(CHUNK_GROUP)]

        def issue(j):
            r = _tile_rows(j, c)
            v, q_t = v_ref[r, :], qt_ref[r, :]
            return r, v, q_t, _mm_nt(q_t, kt_ref[r, :]), _mm_tn(v, kl_ref[r, :])

        def advance(j, r, v, q_t, s, kv, state):
            o = _mm(jnp.where(causal, s, 0.0).astype(BF16), v) + _mm_nt(q_t, state.astype(BF16))
            o = _head_norm(o) * _silu(r_ref[r, :].astype(F32))
            o_ref[r, :] = o.astype(BF16)
            return state * dec_ref[j] + kv

        issued = []
        for g, j in enumerate(chunks):
            issued.append(issue(j))
            if g >= CHUNK_LOOKAHEAD:
                state = advance(chunks[g - CHUNK_LOOKAHEAD], *issued[g - CHUNK_LOOKAHEAD], state)
        for g in range(max(len(chunks) - CHUNK_LOOKAHEAD, 0), len(chunks)):
            state = advance(chunks[g], *issued[g], state)
        return state

    lax.fori_loop(0, seq // c // CHUNK_GROUP, body, jnp.zeros((GLA_DV, GLA_DK), F32))


def _gla(qk, vr, dec, batch, seq):
    h = GLA_HEADS
    slab = lambda first, width: pl.BlockSpec((None, None, seq, width), lambda b, j: (first + j, b, 0, 0))
    return pl.pallas_call(
        _gla_kernel,
        grid=(batch, h),
        in_specs=[slab(0, GLA_DK), slab(h, GLA_DK), slab(2 * h, GLA_DK), slab(0, GLA_DV), slab(h, GLA_DV),
                  pl.BlockSpec((None, seq // GLA_CHUNK, 1, GLA_DK), lambda b, j: (b, 0, 0, j))],
        out_specs=slab(0, GLA_DV),
        out_shape=jax.ShapeDtypeStruct((h, batch, seq, GLA_DV), BF16),
        compiler_params=_params("parallel", "parallel"),
        name="gla",
    )(qk, qk, qk, vr, vr, dec)


def kernel(x, ffn_pre_norm, ffn_pre_w_gate, ffn_pre_w_up, ffn_pre_w_down, mix_norm, ab_w_in, ab_w_out,
           gla_w_in, gla_w_a2, gla_b_a, gla_w_out, ffn_post_norm, ffn_post_w_gate, ffn_post_w_up,
           ffn_post_w_down, final_norm):
    batch, seq, d = x.shape
    depth = ffn_pre_norm.shape[0]
    bf = lambda w: w.astype(BF16)
    gains = lambda g: g.reshape(g.shape[0], 1, d)
    assert RET_HEADS * RET_DV == DIL_WIDTH

    pre = [gains(ffn_pre_norm), bf(ffn_pre_w_gate), bf(ffn_pre_w_up), bf(ffn_pre_w_down)]
    post = [gains(ffn_post_norm), bf(ffn_post_w_gate), bf(ffn_post_w_up), bf(ffn_post_w_down)]
    mix_g = gains(mix_norm)
    ab_in, ab_out = bf(ab_w_in), bf(ab_w_out)
    gla_in = jnp.pad(bf(gla_w_in), ((0, 0), (0, 0), (0, GLA_IN_PADDED - gla_w_in.shape[2])))
    gla_a2 = jnp.pad(bf(gla_w_a2), ((0, 0), (0, LANES - gla_w_a2.shape[1]), (0, 0)))
    gla_ba = gla_b_a.reshape(gla_b_a.shape[0], 1, -1)
    gla_out = bf(gla_w_out)
    tables = _even_tables(seq)

    xt = x.reshape(batch * seq, d)
    for l in range(depth):
        xt = _ffn(xt, [], [_layer(w, l) for w in pre])
        if l % 2 == 0:
            p = _inproj_even(xt, _layer(mix_g, l), _layer(ab_in, l // 2), tables, seq)
            p = p.reshape(EVEN_SLABS, batch, seq, LANES)
            o_r = _retention(p, tables, batch, seq).reshape(RET_HEADS, batch * seq, RET_DV)
            o_d = _dilated(p, batch, seq).reshape(DIL_WIDTH // LANES, batch * seq, LANES)
            mixes = [(o_r, _layer(ab_out, l // 2, rows=DIL_WIDTH, row_block=0)),
                     (o_d, _layer(ab_out, l // 2, rows=DIL_WIDTH, row_block=1))]
        else:
            qk, vr, dec = _inproj_gla(xt, _layer(mix_g, l), _layer(gla_in, l // 2), _layer(gla_a2, l // 2),
                                      _layer(gla_ba, l // 2))
            o = _gla(qk.reshape(3 * GLA_HEADS, batch, seq, GLA_DK), vr.reshape(2 * GLA_HEADS, batch, seq, GLA_DV),
                     dec.reshape(batch, seq // GLA_CHUNK, 1, GLA_KW), batch, seq)
            mixes = [(o.reshape(GLA_HEADS, batch * seq, GLA_DV), _layer(gla_out, l // 2))]
        xt = _ffn(xt, mixes, [_layer(w, l) for w in post],
                  final_g=final_norm.reshape(1, d) if l == depth - 1 else None)
    return xt.reshape(batch, seq, d)
```

```python
import functools

import jax
import jax.numpy as jnp
import numpy as np
from jax import lax
from jax.experimental import pallas as pl
from jax.experimental.pallas import tpu as pltpu

F32 = jnp.float32
BF16 = jnp.bfloat16

LANES = 128
VMEM_LIMIT_BYTES = 56 * 1024 * 1024

D_MODEL = 1024
FFN_HIDDEN = 2816
EPS = 1e-6

RET_HEADS = 4
RET_DK = 128
RET_DV = 128
RET_CHUNK = 128
RET_THETA = 10000.0
RET_WIDTH = RET_HEADS * RET_DK

DIL_HEADS = 8
DIL_DH = 64
DIL_PATTERNS = ((128, 1), (512, 4), (2048, 16))
DIL_BLOCK = 128
DIL_LOOKAHEAD = 3
ROPE_THETA = 500000.0
ROT_DIM = DIL_DH // 4
DIL_WIDTH = DIL_HEADS * DIL_DH

GLA_HEADS = 4
GLA_DK = (D_MODEL // 2) // GLA_HEADS
GLA_DV = D_MODEL // GLA_HEADS
GLA_LOWRANK = 16
GLA_TAU = 16.0
GLA_CHUNK = 64
GLA_KW = GLA_HEADS * GLA_DK
GLA_VW = GLA_HEADS * GLA_DV

CHUNK_GROUP = 32
CHUNK_LOOKAHEAD = 16

EVEN_RQ, EVEN_RQD, EVEN_RK, EVEN_RKD, EVEN_RV, EVEN_RG, EVEN_DQ, EVEN_DK, EVEN_DV = range(0, 36, 4)
EVEN_SLABS = 9 * RET_WIDTH // LANES
GLA_IN_PADDED = 2 * GLA_KW + 2 * GLA_VW + LANES

ROW_TILE = 512
PROJ_TILE = 1024
PREP_TILE = 512
COMBINE_TILE = 1024

NT_DIMS = (((1,), (1,)), ((), ()))
TN_DIMS = (((0,), (0,)), ((), ()))


def _params(*semantics):
    return pltpu.CompilerParams(dimension_semantics=semantics,
                                vmem_limit_bytes=VMEM_LIMIT_BYTES)


def _resident(a):
    zeros = (0,) * a.ndim
    return a, pl.BlockSpec(a.shape, lambda *_: zeros, pipeline_mode=pl.Buffered(1))


def _layer(w, l, rows=None, row_block=0):
    shape = (None, w.shape[1] if rows is None else rows, w.shape[2])
    return w, pl.BlockSpec(shape, lambda *_: (l, row_block, 0), pipeline_mode=pl.Buffered(1))


def _rmsnorm(x, g):
    return x * lax.rsqrt(jnp.mean(x * x, axis=-1, keepdims=True) + EPS) * g


def _norm_parts(x, g):
    return (x * g).astype(BF16), lax.rsqrt(jnp.mean(x * x, axis=-1, keepdims=True) + EPS)


def _head_norm(o):
    mu = jnp.mean(o, axis=-1, keepdims=True)
    c = o - mu
    return c * lax.rsqrt(jnp.mean(c * c, axis=-1, keepdims=True) + EPS)


def _silu(x):
    return x * jax.nn.sigmoid(x)


def _mm(a, b):
    return jnp.dot(a, b, preferred_element_type=F32)


def _mm_nt(a, b):
    return lax.dot_general(a, b, NT_DIMS, preferred_element_type=F32)


def _mm_tn(a, b):
    return lax.dot_general(a, b, TN_DIMS, preferred_element_type=F32)


def _tile_rows(i, n):
    return pl.ds(pl.multiple_of(i * n, n), n)


def _lane_block(j, n=1):
    return slice(j * LANES, (j + n) * LANES)


def _ffn_kernel(*refs, n_mix, final):
    x_ref = refs[0]
    mix_refs = refs[1:1 + 2 * n_mix]
    g_ref, wg_ref, wu_ref, wd_ref = refs[1 + 2 * n_mix:5 + 2 * n_mix]
    o_ref = refs[-1]
    x = x_ref[...]
    for i in range(n_mix):
        mix_ref = mix_refs[2 * i]
        mix = jnp.concatenate([mix_ref[g] for g in range(mix_ref.shape[0])], axis=1)
        x = x + _mm(mix, mix_refs[2 * i + 1][...])
    xg, r = _norm_parts(x, g_ref[...])
    act = (_silu(_mm(xg, wg_ref[...]) * r) * (_mm(xg, wu_ref[...]) * r)).astype(BF16)
    y = x + 0.5 * _mm(act, wd_ref[...])
    if final:
        y = _rmsnorm(y, refs[-2][...])
    o_ref[...] = y


def _ffn(x, mixes, weights, final_g=None):
    t, d = x.shape
    row = lambda w: pl.BlockSpec((ROW_TILE, w), lambda i: (i, 0))
    args, specs = [x], [row(d)]
    for m, (w, spec) in mixes:
        args += [m, w]
        specs += [pl.BlockSpec((m.shape[0], ROW_TILE, m.shape[2]), lambda i: (0, i, 0)), spec]
    for w, spec in weights + ([_resident(final_g)] if final_g is not None else []):
        args.append(w)
        specs.append(spec)
    return pl.pallas_call(
        functools.partial(_ffn_kernel, n_mix=len(mixes), final=final_g is not None),
        grid=(t // ROW_TILE,),
        in_specs=specs,
        out_specs=row(d),
        out_shape=jax.ShapeDtypeStruct((t, d), F32),
        compiler_params=_params("parallel"),
        name="ffn",
    )(*args)


def _inproj_even_kernel(x_ref, g_ref, w_ref, rcos_ref, rsin_ref, qd_ref, kd_ref,
                        dcos_ref, dsina_ref, dsinb_ref, o_ref):
    xg, r = _norm_parts(x_ref[...], g_ref[...])
    proj = lambda first_block, n_blocks: _mm(xg, w_ref[:, _lane_block(first_block, n_blocks)]) * r
    heads = RET_WIDTH // LANES
    half = ROT_DIM // 2

    rcos, rsin = rcos_ref[...], rsin_ref[...]
    ret_rot = lambda x: x * rcos + pltpu.roll(x, RET_DK // 2, 1) * rsin
    q = proj(0, heads)
    for j in range(heads):
        qj = ret_rot(q[:, _lane_block(j)])
        o_ref[EVEN_RQ + j] = qj.astype(BF16)
        o_ref[EVEN_RQD + j] = (qj * qd_ref[:, _lane_block(j)]).astype(BF16)
    k = proj(heads, heads)
    for j in range(heads):
        kj = ret_rot(k[:, _lane_block(j)]) * (RET_DK ** -0.5)
        o_ref[EVEN_RK + j] = kj.astype(BF16)
        o_ref[EVEN_RKD + j] = (kj * kd_ref[:, _lane_block(j)]).astype(BF16)
    vg = proj(2 * heads, 2 * heads)
    for j in range(2 * heads):
        o_ref[EVEN_RV + j] = vg[:, _lane_block(j)].astype(BF16)

    dcos, dsina, dsinb = dcos_ref[...], dsina_ref[...], dsinb_ref[...]
    dil_rot = lambda x: x * dcos + pltpu.roll(x, LANES - half, 1) * dsina + pltpu.roll(x, half, 1) * dsinb
    pairs = DIL_WIDTH // LANES
    dq = proj(4 * heads, pairs)
    for j in range(pairs):
        o_ref[EVEN_DQ + j] = (dil_rot(dq[:, _lane_block(j)]) * (DIL_DH ** -0.5)).astype(BF16)
    dk = proj(4 * heads + pairs, pairs)
    for j in range(pairs):
        o_ref[EVEN_DK + j] = dil_rot(dk[:, _lane_block(j)]).astype(BF16)
    dv = proj(4 * heads + 2 * pairs, pairs)
    for j in range(pairs):
        o_ref[EVEN_DV + j] = dv[:, _lane_block(j)].astype(BF16)


def _even_tables(seq):
    c = RET_CHUNK
    log_g = np.log(1.0 - 2.0 ** (-5.0 - np.arange(RET_HEADS, dtype=np.float64)))
    i = np.arange(c, dtype=np.float64)
    rel = i[:, None] - i[None, :]
    dmask = np.where(rel >= 0, np.exp(log_g[:, None, None] * np.maximum(rel, 0.0)), 0.0)
    spread = lambda a: np.tile(np.repeat(a.T, RET_DK, axis=1), (PROJ_TILE // c, 1))
    q_decay = spread(np.exp(log_g[:, None] * (i + 1.0)))
    k_decay = spread(np.exp(log_g[:, None] * (c - 1 - i)))
    chunk_decay = np.exp(log_g * c)[:, None, None] * np.ones((1, 1, RET_DV))
    pos = np.arange(seq, dtype=np.float32)[:, None]

    freq = np.float32(RET_THETA) ** (-np.linspace(0.0, 1.0, RET_DK // 2, dtype=np.float32))
    ang = pos * freq[None, :]
    rcos = np.concatenate([np.cos(ang), np.cos(ang)], axis=1)
    rsin = np.concatenate([-np.sin(ang), np.sin(ang)], axis=1)

    freq = np.float32(ROPE_THETA) ** (-np.arange(0, ROT_DIM, 2, dtype=np.float32) / np.float32(ROT_DIM))
    ang = pos * freq[None, :]
    cos, sin = np.cos(ang), np.sin(ang)
    zeros = np.zeros_like(sin)
    head = lambda a, b, fill: np.concatenate([a, b, np.full((seq, DIL_DH - ROT_DIM), fill, np.float32)], axis=1)
    two = lambda x: np.concatenate([x, x], axis=1)
    dcos, dsina, dsinb = two(head(cos, cos, 1.0)), two(head(-sin, zeros, 0.0)), two(head(zeros, sin, 0.0))
    f = lambda a: np.asarray(a, np.float32)
    return dict(rcos=f(rcos), rsin=f(rsin), q_decay=f(q_decay), k_decay=f(k_decay), dcos=f(dcos),
                dsina=f(dsina), dsinb=f(dsinb), dmask=f(dmask), chunk_decay=f(chunk_decay))


def _inproj_even(x, gain, weight, tables, seq):
    t, d = x.shape
    tiles_per_seq = seq // PROJ_TILE
    by_position = lambda a: (a, pl.BlockSpec((PROJ_TILE, LANES), lambda i: (i % tiles_per_seq, 0)))
    operands = [(x, pl.BlockSpec((PROJ_TILE, d), lambda i: (i, 0))), gain, weight,
                by_position(tables["rcos"]), by_position(tables["rsin"]),
                _resident(tables["q_decay"]), _resident(tables["k_decay"]),
                by_position(tables["dcos"]), by_position(tables["dsina"]), by_position(tables["dsinb"])]
    return pl.pallas_call(
        _inproj_even_kernel,
        grid=(t // PROJ_TILE,),
        in_specs=[spec for _, spec in operands],
        out_specs=pl.BlockSpec((EVEN_SLABS, PROJ_TILE, LANES), lambda i: (0, i, 0)),
        out_shape=jax.ShapeDtypeStruct((EVEN_SLABS, t, LANES), BF16),
        compiler_params=_params("parallel"),
        name="inproj_even",
    )(*[a for a, _ in operands])


def _retention_kernel(q_ref, qdec_ref, k_ref, kdec_ref, v_ref, g_ref, dm_ref, cd_ref, o_ref):
    c = RET_CHUNK
    seq = q_ref.shape[0]
    group = min(CHUNK_GROUP, seq // c)
    dmask = dm_ref[...]
    chunk_decay = cd_ref[...]

    def body(n, state):
        rows = [_tile_rows(n * group + g, c) for g in range(group)]

        def issue(r):
            v = v_ref[r, :]
            return v, _mm_nt(q_ref[r, :], k_ref[r, :]), _mm_tn(kdec_ref[r, :], v)

        def advance(r, v, s, kv, state):
            lhs = jnp.concatenate([(s * dmask).astype(BF16), qdec_ref[r, :]], axis=1)
            rhs = jnp.concatenate([v, state.astype(BF16)], axis=0)
            o = _head_norm(_mm(lhs, rhs)) * _silu(g_ref[r, :].astype(F32))
            o_ref[r, :] = o.astype(BF16)
            return state * chunk_decay + kv

        issued = []
        for g, r in enumerate(rows):
            issued.append(issue(r))
            if g >= CHUNK_LOOKAHEAD:
                state = advance(rows[g - CHUNK_LOOKAHEAD], *issued[g - CHUNK_LOOKAHEAD], state)
        for g in range(max(len(rows) - CHUNK_LOOKAHEAD, 0), len(rows)):
            state = advance(rows[g], *issued[g], state)
        return state

    lax.fori_loop(0, seq // c // group, body, jnp.zeros((RET_DK, RET_DV), F32))


def _retention(p, tables, batch, seq):
    head_block = lambda first: pl.BlockSpec((None, None, seq, RET_DK), lambda b, j: (first + j, b, 0, 0))
    per_head = lambda a: pl.BlockSpec((None,) + a.shape[1:], lambda b, j: (j, 0, 0))
    dmask, chunk_decay = tables["dmask"], tables["chunk_decay"]
    return pl.pallas_call(
        _retention_kernel,
        grid=(batch, RET_HEADS),
        in_specs=[head_block(EVEN_RQ), head_block(EVEN_RQD), head_block(EVEN_RK), head_block(EVEN_RKD),
                  head_block(EVEN_RV), head_block(EVEN_RG), per_head(dmask), per_head(chunk_decay)],
        out_specs=pl.BlockSpec((None, None, seq, RET_DV), lambda b, j: (j, b, 0, 0)),
        out_shape=jax.ShapeDtypeStruct((RET_HEADS, batch, seq, RET_DV), BF16),
        compiler_params=_params("parallel", "parallel"),
        name="retention",
    )(p, p, p, p, p, p, dmask, chunk_decay)


def _dilated_kernel(q_ref, k_ref, v_ref, o_ref, qs_ref, ks_ref, vs_ref, ob_ref, mb_ref, lb_ref):
    seq = q_ref.shape[0]
    c = DIL_BLOCK

    def stage(i, carry):
        rows = _tile_rows(i, PREP_TILE)
        qs_ref[rows, :] = q_ref[rows, :].astype(F32)
        ks_ref[rows, :] = k_ref[rows, :].astype(F32)
        vs_ref[rows, :] = v_ref[rows, :].astype(F32)
        return carry

    lax.fori_loop(0, seq // PREP_TILE, stage, 0)

    head0 = lax.broadcasted_iota(jnp.int32, (c, LANES), 1) < DIL_DH
    dist = lax.broadcasted_iota(jnp.int32, (c, 2 * c), 0) + c - lax.broadcasted_iota(jnp.int32, (c, 2 * c), 1)
    band = (dist >= 0) & (dist <= c)
    causal = lax.broadcasted_iota(jnp.int32, (c, c), 0) >= lax.broadcasted_iota(jnp.int32, (c, c), 1)
    band_bias = jnp.where(band, 0.0, -jnp.inf)
    causal_bias = jnp.where(causal, 0.0, -jnp.inf)

    def class_rows(r, dil):
        return pl.ds(r, seq // dil, stride=dil) if dil > 1 else pl.ds(0, seq)

    def tile_keys(strip, u):
        return strip[max(u - 1, 0) * c:(u + 1) * c]

    def branch(dil):
        blocks = seq // dil // c
        strips = {}

        def strip(ref, r, prepare):
            if (id(ref), r) not in strips:
                strips[id(ref), r] = prepare(ref[class_rows(r, dil), :])
            return strips[id(ref), r]

        def by_head(x, other):
            first = lax.broadcasted_iota(jnp.int32, x.shape, 1) < DIL_DH
            return jnp.where(first, x, other).astype(BF16), jnp.where(first, other, x).astype(BF16)

        def issue(r, u):
            qu = strip(qs_ref, r, lambda q: q)[u * c:(u + 1) * c]
            ku = tile_keys(strip(ks_ref, r, lambda k: k.astype(BF16)), u)
            bias = band_bias if u > 0 else causal_bias
            return [bias + _mm_nt(qh, ku) for qh in by_head(qu, 0.0)]

        def finish(r, u, scores):
            res, mx = [], []
            for s, vh in zip(scores, strip(vs_ref, r, lambda v: by_head(v, 1.0))):
                m = jnp.max(s, axis=-1, keepdims=True)
                res.append(_mm(jnp.exp(s - m).astype(BF16), tile_keys(vh, u)))
                mx.append(jnp.broadcast_to(m, (c, LANES)))
            start = r + u * c * dil
            rows = pl.ds(start, c, stride=dil) if dil > 1 else pl.ds(start, c)
            return (rows, jnp.where(head0, res[0], res[1]), jnp.where(head0, mx[0], mx[1]),
                    jnp.where(head0, res[1], res[0]))

        tiles = [(r, u) for r in range(dil) for u in range(blocks)]
        issued = []
        for i, tile in enumerate(tiles):
            issued.append(issue(*tile))
            if i >= DIL_LOOKAHEAD:
                yield finish(*tiles[i - DIL_LOOKAHEAD], issued[i - DIL_LOOKAHEAD])
        for i in range(max(len(tiles) - DIL_LOOKAHEAD, 0), len(tiles)):
            yield finish(*tiles[i], issued[i])

    n_br = len(DIL_PATTERNS)
    for br, (window, dil) in enumerate(DIL_PATTERNS):
        assert window // dil == c
        for rows, o, m, l in branch(dil):
            ob_ref[br, rows, :] = o
            mb_ref[br, rows, :] = m
            lb_ref[br, rows, :] = l

    def combine(i, carry):
        rows = _tile_rows(i, COMBINE_TILE)
        ms = [mb_ref[b, rows, :] for b in range(n_br)]
        m = functools.reduce(jnp.maximum, ms)
        es = [jnp.exp(x - m) for x in ms]
        num = sum(es[b] * ob_ref[b, rows, :] for b in range(n_br))
        den = sum(es[b] * pltpu.roll(lb_ref[b, rows, :], DIL_DH, 1) for b in range(n_br))
        o_ref[rows, :] = (num / den).astype(BF16)
        return carry

    lax.fori_loop(0, seq // COMBINE_TILE, combine, 0)


def _dilated(p, batch, seq):
    pairs = DIL_WIDTH // LANES
    pair_block = lambda first: pl.BlockSpec((None, None, seq, LANES), lambda b, j: (first + j, b, 0, 0))
    n_br = len(DIL_PATTERNS)
    return pl.pallas_call(
        _dilated_kernel,
        grid=(batch, pairs),
        in_specs=[pair_block(EVEN_DQ), pair_block(EVEN_DK), pair_block(EVEN_DV)],
        out_specs=pl.BlockSpec((None, None, seq, LANES), lambda b, j: (j, b, 0, 0)),
        out_shape=jax.ShapeDtypeStruct((pairs, batch, seq, LANES), BF16),
        scratch_shapes=[pltpu.VMEM((seq, LANES), F32)] * 3 + [pltpu.VMEM((n_br, seq, LANES), F32)] * 3,
        compiler_params=_params("parallel", "parallel"),
        name="dilated",
    )(p, p, p)


def _inproj_gla_kernel(x_ref, g_ref, w_ref, wa_ref, ba_ref, qk_ref, vr_ref, dec_ref):
    c = GLA_CHUNK
    per_tile = PROJ_TILE // c
    kw, vw = GLA_KW, GLA_VW
    xg, r = _norm_parts(x_ref[...], g_ref[...])
    proj = lambda lo, hi: _mm(xg, w_ref[:, lo:hi]) * r
    a_low = proj(2 * kw + 2 * vw, GLA_IN_PADDED).astype(BF16)
    z = _mm(a_low, wa_ref[...]) + ba_ref[...]
    slabs = []
    for j in range(2 * GLA_HEADS):
        slabs.append(proj(2 * kw + j * GLA_DV, 2 * kw + (j + 1) * GLA_DV))
        vr_ref[j] = slabs[j].astype(BF16)
    q = proj(0, kw) * (GLA_DK ** -0.5)
    k = proj(kw, 2 * kw)
    row = lax.broadcasted_iota(jnp.int32, (c, kw), 0)
    for j in range(per_tile):
        rows = slice(j * c, (j + 1) * c)
        anchor = 0.0 * slabs[j % len(slabs)][rows, :LANES]
        zj = z[rows] + jnp.concatenate([anchor] * (kw // LANES), axis=1)
        b = (jnp.minimum(zj, 0.0) - jnp.log1p(jnp.exp(-jnp.abs(zj)))) * (1.0 / GLA_TAU)
        shift = 1
        while shift < c:
            b = b + jnp.where(row >= shift, pltpu.roll(b, shift, 0), 0.0)
            shift *= 2
        b_last = b[c - 1:c, :]
        dec_ref[j] = jnp.exp(b_last)
        operands = (q[rows] * jnp.exp(b), k[rows] * jnp.exp(-b), k[rows] * jnp.exp(b_last - b))
        for i, operand in enumerate(operands):
            for h in range(GLA_HEADS):
                qk_ref[i * GLA_HEADS + h, rows, :] = operand[:, h * GLA_DK:(h + 1) * GLA_DK].astype(BF16)


def _inproj_gla(x, gain, weight, wa, ba):
    t, d = x.shape
    per_tile = PROJ_TILE // GLA_CHUNK
    operands = [(x, pl.BlockSpec((PROJ_TILE, d), lambda i: (i, 0))), gain, weight, wa, ba]
    return pl.pallas_call(
        _inproj_gla_kernel,
        grid=(t // PROJ_TILE,),
        in_specs=[spec for _, spec in operands],
        out_specs=[pl.BlockSpec((3 * GLA_HEADS, PROJ_TILE, GLA_DK), lambda i: (0, i, 0)),
                   pl.BlockSpec((2 * GLA_HEADS, PROJ_TILE, GLA_DV), lambda i: (0, i, 0)),
                   pl.BlockSpec((per_tile, 1, GLA_KW), lambda i: (i, 0, 0))],
        out_shape=[jax.ShapeDtypeStruct((3 * GLA_HEADS, t, GLA_DK), BF16),
                   jax.ShapeDtypeStruct((2 * GLA_HEADS, t, GLA_DV), BF16),
                   jax.ShapeDtypeStruct((t // GLA_CHUNK, 1, GLA_KW), F32)],
        compiler_params=_params("parallel"),
        name="inproj_gla",
    )(*[a for a, _ in operands])


def _gla_kernel(qt_ref, kt_ref, kl_ref, v_ref, r_ref, dec_ref, o_ref):
    c = GLA_CHUNK
    seq = qt_ref.shape[0]
    group = min(CHUNK_GROUP, seq // c)
    causal = lax.broadcasted_iota(jnp.int32, (c, c), 0) >= lax.broadcasted_iota(jnp.int32, (c, c), 1)

    def body(n, state):
        chunks = [n * group + g for g in range(group)]

        def issue(j):
            r = _tile_rows(j, c)
            v, q_t = v_ref[r, :], qt_ref[r, :]
            return r, v, q_t, _mm_nt(q_t, kt_ref[r, :]), _mm_tn(v, kl_ref[r, :])

        def advance(j, r, v, q_t, s, kv, state):
            o = _mm(jnp.where(causal, s, 0.0).astype(BF16), v) + _mm_nt(q_t, state.astype(BF16))
            o = _head_norm(o) * _silu(r_ref[r, :].astype(F32))
            o_ref[r, :] = o.astype(BF16)
            return state * dec_ref[j] + kv

        issued = []
        for g, j in enumerate(chunks):
            issued.append(issue(j))
            if g >= CHUNK_LOOKAHEAD:
                state = advance(chunks[g - CHUNK_LOOKAHEAD], *issued[g - CHUNK_LOOKAHEAD], state)
        for g in range(max(len(chunks) - CHUNK_LOOKAHEAD, 0), len(chunks)):
            state = advance(chunks[g], *issued[g], state)
        return state

    lax.fori_loop(0, seq // c // group, body, jnp.zeros((GLA_DV, GLA_DK), F32))


def _gla(qk, vr, dec, batch, seq):
    h = GLA_HEADS
    slab = lambda first, width: pl.BlockSpec((None, None, seq, width), lambda b, j: (first + j, b, 0, 0))
    return pl.pallas_call(
        _gla_kernel,
        grid=(batch, h),
        in_specs=[slab(0, GLA_DK), slab(h, GLA_DK), slab(2 * h, GLA_DK), slab(0, GLA_DV), slab(h, GLA_DV),
                  pl.BlockSpec((None, seq // GLA_CHUNK, 1, GLA_DK), lambda b, j: (b, 0, 0, j))],
        out_specs=slab(0, GLA_DV),
        out_shape=jax.ShapeDtypeStruct((h, batch, seq, GLA_DV), BF16),
        compiler_params=_params("parallel", "parallel"),
        name="gla",
    )(qk, qk, qk, vr, vr, dec)


def kernel(x, ffn_pre_norm, ffn_pre_w_gate, ffn_pre_w_up, ffn_pre_w_down, mix_norm, ab_w_in, ab_w_out,
           gla_w_in, gla_w_a2, gla_b_a, gla_w_out, ffn_post_norm, ffn_post_w_gate, ffn_post_w_up,
           ffn_post_w_down, final_norm):
    batch, seq, d = x.shape
    depth = ffn_pre_norm.shape[0]
    bf = lambda w: w.astype(BF16)
    gains = lambda g: g.reshape(g.shape[0], 1, d)
    assert RET_HEADS * RET_DV == DIL_WIDTH

    pre = [gains(ffn_pre_norm), bf(ffn_pre_w_gate), bf(ffn_pre_w_up), bf(ffn_pre_w_down)]
    post = [gains(ffn_post_norm), bf(ffn_post_w_gate), bf(ffn_post_w_up), bf(ffn_post_w_down)]
    mix_g = gains(mix_norm)
    ab_in, ab_out = bf(ab_w_in), bf(ab_w_out)
    gla_in = jnp.pad(bf(gla_w_in), ((0, 0), (0, 0), (0, GLA_IN_PADDED - gla_w_in.shape[2])))
    gla_a2 = jnp.pad(bf(gla_w_a2), ((0, 0), (0, LANES - gla_w_a2.shape[1]), (0, 0)))
    gla_ba = gla_b_a.reshape(gla_b_a.shape[0], 1, -1)
    gla_out = bf(gla_w_out)
    tables = _even_tables(seq)

    xt = x.reshape(batch * seq, d)
    for l in range(depth):
        xt = _ffn(xt, [], [_layer(w, l) for w in pre])
        if l % 2 == 0:
            p = _inproj_even(xt, _layer(mix_g, l), _layer(ab_in, l // 2), tables, seq)
            p = p.reshape(EVEN_SLABS, batch, seq, LANES)
            o_r = _retention(p, tables, batch, seq).reshape(RET_HEADS, batch * seq, RET_DV)
            o_d = _dilated(p, batch, seq).reshape(DIL_WIDTH // LANES, batch * seq, LANES)
            mixes = [(o_r, _layer(ab_out, l // 2, rows=DIL_WIDTH, row_block=0)),
                     (o_d, _layer(ab_out, l // 2, rows=DIL_WIDTH, row_block=1))]
        else:
            qk, vr, dec = _inproj_gla(xt, _layer(mix_g, l), _layer(gla_in, l // 2), _layer(gla_a2, l // 2),
                                      _layer(gla_ba, l // 2))
            o = _gla(qk.reshape(3 * GLA_HEADS, batch, seq, GLA_DK), vr.reshape(2 * GLA_HEADS, batch, seq, GLA_DV),
                     dec.reshape(batch, seq // GLA_CHUNK, 1, GLA_KW), batch, seq)
            mixes = [(o.reshape(GLA_HEADS, batch * seq, GLA_DV), _layer(gla_out, l // 2))]
        xt = _ffn(xt, mixes, [_layer(w, l) for w in post],
                  final_g=final_norm.reshape(1, d) if l == depth - 1 else None)
    return xt.reshape(batch, seq, d)
```

```python
import functools

import jax
import jax.numpy as jnp
import numpy as np
from jax import lax
from jax.experimental import pallas as pl
from jax.experimental.pallas import tpu as pltpu

F32 = jnp.float32
BF16 = jnp.bfloat16

LANES = 128
VMEM_LIMIT_BYTES = 56 * 1024 * 1024

D_MODEL = 1024
FFN_HIDDEN = 2816
EPS = 1e-6

RET_HEADS = 4
RET_DK = 128
RET_DV = 128
RET_CHUNK = 128
RET_THETA = 10000.0
RET_WIDTH = RET_HEADS * RET_DK

DIL_HEADS = 8
DIL_DH = 64
DIL_PATTERNS = ((128, 1), (512, 4), (2048, 16))
DIL_BLOCK = 128
DIL_LOOKAHEAD = 3
ROPE_THETA = 500000.0
ROT_DIM = DIL_DH // 4
DIL_WIDTH = DIL_HEADS * DIL_DH

GLA_HEADS = 4
GLA_DK = (D_MODEL // 2) // GLA_HEADS
GLA_DV = D_MODEL // GLA_HEADS
GLA_LOWRANK = 16
GLA_TAU = 16.0
GLA_CHUNK = 64
GLA_KW = GLA_HEADS * GLA_DK
GLA_VW = GLA_HEADS * GLA_DV

CHUNK_GROUP = 32
CHUNK_LOOKAHEAD = 16

EVEN_RQ, EVEN_RQD, EVEN_RK, EVEN_RKD, EVEN_RV, EVEN_RG, EVEN_DQ, EVEN_DK, EVEN_DV = range(0, 36, 4)
EVEN_SLABS = 9 * RET_WIDTH // LANES
GLA_IN_PADDED = 2 * GLA_KW + 2 * GLA_VW + LANES

ROW_TILE = 512
PREP_TILE = 512
COMBINE_TILE = 1024

NT_DIMS = (((1,), (1,)), ((), ()))
TN_DIMS = (((0,), (0,)), ((), ()))


def _params(*semantics):
    return pltpu.CompilerParams(dimension_semantics=semantics,
                                vmem_limit_bytes=VMEM_LIMIT_BYTES)


def _resident(a):
    zeros = (0,) * a.ndim
    return a, pl.BlockSpec(a.shape, lambda *_: zeros, pipeline_mode=pl.Buffered(1))


def _layer(w, l, rows=None, row_block=0):
    shape = (None, w.shape[1] if rows is None else rows, w.shape[2])
    return w, pl.BlockSpec(shape, lambda *_: (l, row_block, 0), pipeline_mode=pl.Buffered(1))


def _rmsnorm(x, g):
    return x * lax.rsqrt(jnp.mean(x * x, axis=-1, keepdims=True) + EPS) * g


def _norm_parts(x, g):
    return (x * g).astype(BF16), lax.rsqrt(jnp.mean(x * x, axis=-1, keepdims=True) + EPS)


def _head_norm(o):
    mu = jnp.mean(o, axis=-1, keepdims=True)
    c = o - mu
    return c * lax.rsqrt(jnp.mean(c * c, axis=-1, keepdims=True) + EPS)


def _silu(x):
    return x * (0.5 * jnp.tanh(0.5 * x) + 0.5)


def _mm(a, b):
    return jnp.dot(a, b, preferred_element_type=F32)


def _mm_nt(a, b):
    return lax.dot_general(a, b, NT_DIMS, preferred_element_type=F32)


def _mm_tn(a, b):
    return lax.dot_general(a, b, TN_DIMS, preferred_element_type=F32)


def _tile_rows(i, n):
    return pl.ds(pl.multiple_of(i * n, n), n)


def _lane_block(j, n=1):
    return slice(j * LANES, (j + n) * LANES)


def _ffn_kernel(*refs, n_mix, final):
    x_ref = refs[0]
    mix_refs = refs[1:1 + 2 * n_mix]
    g_ref, wg_ref, wu_ref, wd_ref = refs[1 + 2 * n_mix:5 + 2 * n_mix]
    o_ref = refs[-1]
    x = x_ref[...]
    for i in range(n_mix):
        mix_ref = mix_refs[2 * i]
        mix = jnp.concatenate([mix_ref[g] for g in range(mix_ref.shape[0])], axis=1)
        x = x + _mm(mix, mix_refs[2 * i + 1][...])
    xg, r = _norm_parts(x, g_ref[...])
    act = (_silu(_mm(xg, wg_ref[...]) * r) * (_mm(xg, wu_ref[...]) * r)).astype(BF16)
    y = x + 0.5 * _mm(act, wd_ref[...])
    if final:
        y = _rmsnorm(y, refs[-2][...])
    o_ref[...] = y


def _ffn(x, mixes, weights, final_g=None):
    t, d = x.shape
    row = lambda w: pl.BlockSpec((ROW_TILE, w), lambda i: (i, 0))
    args, specs = [x], [row(d)]
    for m, (w, spec) in mixes:
        args += [m, w]
        specs += [pl.BlockSpec((m.shape[0], ROW_TILE, m.shape[2]), lambda i: (0, i, 0)), spec]
    for w, spec in weights + ([_resident(final_g)] if final_g is not None else []):
        args.append(w)
        specs.append(spec)
    return pl.pallas_call(
        functools.partial(_ffn_kernel, n_mix=len(mixes), final=final_g is not None),
        grid=(t // ROW_TILE,),
        in_specs=specs,
        out_specs=row(d),
        out_shape=jax.ShapeDtypeStruct((t, d), F32),
        compiler_params=_params("parallel"),
        name="ffn",
    )(*args)


def _inproj_even_kernel(x_ref, g_ref, w_ref, rcos_ref, rsin_ref, qd_ref, kd_ref,
                        dcos_ref, dsina_ref, dsinb_ref, o_ref):
    xg, r = _norm_parts(x_ref[...], g_ref[...])
    proj = lambda first_block, n_blocks: _mm(xg, w_ref[:, _lane_block(first_block, n_blocks)]) * r
    heads = RET_WIDTH // LANES
    half = ROT_DIM // 2

    rcos, rsin = rcos_ref[...], rsin_ref[...]
    ret_rot = lambda x: x * rcos + pltpu.roll(x, RET_DK // 2, 1) * rsin
    q = proj(0, heads)
    for j in range(heads):
        qj = ret_rot(q[:, _lane_block(j)])
        o_ref[EVEN_RQ + j] = qj.astype(BF16)
        o_ref[EVEN_RQD + j] = (qj * qd_ref[:, _lane_block(j)]).astype(BF16)
    k = proj(heads, heads)
    for j in range(heads):
        kj = ret_rot(k[:, _lane_block(j)]) * (RET_DK ** -0.5)
        o_ref[EVEN_RK + j] = kj.astype(BF16)
        o_ref[EVEN_RKD + j] = (kj * kd_ref[:, _lane_block(j)]).astype(BF16)
    vg = proj(2 * heads, 2 * heads)
    for j in range(2 * heads):
        o_ref[EVEN_RV + j] = vg[:, _lane_block(j)].astype(BF16)

    dcos, dsina, dsinb = dcos_ref[...], dsina_ref[...], dsinb_ref[...]
    dil_rot = lambda x: x * dcos + pltpu.roll(x, LANES - half, 1) * dsina + pltpu.roll(x, half, 1) * dsinb
    pairs = DIL_WIDTH // LANES
    dq = proj(4 * heads, pairs)
    for j in range(pairs):
        o_ref[EVEN_DQ + j] = (dil_rot(dq[:, _lane_block(j)]) * (DIL_DH ** -0.5)).astype(BF16)
    dk = proj(4 * heads + pairs, pairs)
    for j in range(pairs):
        o_ref[EVEN_DK + j] = dil_rot(dk[:, _lane_block(j)]).astype(BF16)
    dv = proj(4 * heads + 2 * pairs, pairs)
    for j in range(pairs):
        o_ref[EVEN_DV + j] = dv[:, _lane_block(j)].astype(BF16)


def _even_tables(seq):
    c = RET_CHUNK
    log_g = np.log(1.0 - 2.0 ** (-5.0 - np.arange(RET_HEADS, dtype=np.float64)))
    i = np.arange(c, dtype=np.float64)
    rel = i[:, None] - i[None, :]
    dmask = np.where(rel >= 0, np.exp(log_g[:, None, None] * np.maximum(rel, 0.0)), 0.0)
    spread = lambda a: np.tile(np.repeat(a.T, RET_DK, axis=1), (ROW_TILE // c, 1))
    q_decay = spread(np.exp(log_g[:, None] * (i + 1.0)))
    k_decay = spread(np.exp(log_g[:, None] * (c - 1 - i)))
    chunk_decay = np.exp(log_g * c)[:, None, None] * np.ones((1, 1, RET_DV))
    pos = np.arange(seq, dtype=np.float32)[:, None]

    freq = np.float32(RET_THETA) ** (-np.linspace(0.0, 1.0, RET_DK // 2, dtype=np.float32))
    ang = pos * freq[None, :]
    rcos = np.concatenate([np.cos(ang), np.cos(ang)], axis=1)
    rsin = np.concatenate([-np.sin(ang), np.sin(ang)], axis=1)

    freq = np.float32(ROPE_THETA) ** (-np.arange(0, ROT_DIM, 2, dtype=np.float32) / np.float32(ROT_DIM))
    ang = pos * freq[None, :]
    cos, sin = np.cos(ang), np.sin(ang)
    zeros = np.zeros_like(sin)
    head = lambda a, b, fill: np.concatenate([a, b, np.full((seq, DIL_DH - ROT_DIM), fill, np.float32)], axis=1)
    two = lambda x: np.concatenate([x, x], axis=1)
    dcos, dsina, dsinb = two(head(cos, cos, 1.0)), two(head(-sin, zeros, 0.0)), two(head(zeros, sin, 0.0))
    f = lambda a: np.asarray(a, np.float32)
    return dict(rcos=f(rcos), rsin=f(rsin), q_decay=f(q_decay), k_decay=f(k_decay), dcos=f(dcos),
                dsina=f(dsina), dsinb=f(dsinb), dmask=f(dmask), chunk_decay=f(chunk_decay))


def _inproj_even(x, gain, weight, tables, seq):
    t, d = x.shape
    tiles_per_seq = seq // ROW_TILE
    by_position = lambda a: (a, pl.BlockSpec((ROW_TILE, LANES), lambda i: (i % tiles_per_seq, 0)))
    operands = [(x, pl.BlockSpec((ROW_TILE, d), lambda i: (i, 0))), gain, weight,
                by_position(tables["rcos"]), by_position(tables["rsin"]),
                _resident(tables["q_decay"]), _resident(tables["k_decay"]),
                by_position(tables["dcos"]), by_position(tables["dsina"]), by_position(tables["dsinb"])]
    return pl.pallas_call(
        _inproj_even_kernel,
        grid=(t // ROW_TILE,),
        in_specs=[spec for _, spec in operands],
        out_specs=pl.BlockSpec((EVEN_SLABS, ROW_TILE, LANES), lambda i: (0, i, 0)),
        out_shape=jax.ShapeDtypeStruct((EVEN_SLABS, t, LANES), BF16),
        compiler_params=_params("parallel"),
        name="inproj_even",
    )(*[a for a, _ in operands])


def _retention_kernel(q_ref, qdec_ref, k_ref, kdec_ref, v_ref, g_ref, dm_ref, cd_ref, o_ref):
    c = RET_CHUNK
    seq = q_ref.shape[0]
    dmask = dm_ref[...]
    chunk_decay = cd_ref[...]

    def body(n, state):
        rows = [_tile_rows(n * CHUNK_GROUP + g, c) for g in range(CHUNK_GROUP)]

        def issue(r):
            v = v_ref[r, :]
            return v, _mm_nt(q_ref[r, :], k_ref[r, :]), _mm_tn(kdec_ref[r, :], v)

        def advance(r, v, s, kv, state):
            lhs = jnp.concatenate([(s * dmask).astype(BF16), qdec_ref[r, :]], axis=1)
            rhs = jnp.concatenate([v, state.astype(BF16)], axis=0)
            o = _head_norm(_mm(lhs, rhs)) * _silu(g_ref[r, :].astype(F32))
            o_ref[r, :] = o.astype(BF16)
            return state * chunk_decay + kv

        issued = []
        for g, r in enumerate(rows):
            issued.append(issue(r))
            if g >= CHUNK_LOOKAHEAD:
                state = advance(rows[g - CHUNK_LOOKAHEAD], *issued[g - CHUNK_LOOKAHEAD], state)
        for g in range(max(len(rows) - CHUNK_LOOKAHEAD, 0), len(rows)):
            state = advance(rows[g], *issued[g], state)
        return state

    lax.fori_loop(0, seq // c // CHUNK_GROUP, body, jnp.zeros((RET_DK, RET_DV), F32))


def _retention(p, tables, batch, seq):
    head_block = lambda first: pl.BlockSpec((None, None, seq, RET_DK), lambda b, j: (first + j, b, 0, 0))
    per_head = lambda a: pl.BlockSpec((None,) + a.shape[1:], lambda b, j: (j, 0, 0))
    dmask, chunk_decay = tables["dmask"], tables["chunk_decay"]
    return pl.pallas_call(
        _retention_kernel,
        grid=(batch, RET_HEADS),
        in_specs=[head_block(EVEN_RQ), head_block(EVEN_RQD), head_block(EVEN_RK), head_block(EVEN_RKD),
                  head_block(EVEN_RV), head_block(EVEN_RG), per_head(dmask), per_head(chunk_decay)],
        out_specs=pl.BlockSpec((None, None, seq, RET_DV), lambda b, j: (j, b, 0, 0)),
        out_shape=jax.ShapeDtypeStruct((RET_HEADS, batch, seq, RET_DV), BF16),
        compiler_params=_params("parallel", "parallel"),
        name="retention",
    )(p, p, p, p, p, p, dmask, chunk_decay)


def _dilated_kernel(q_ref, k_ref, v_ref, o_ref, qs_ref, ks_ref, vs_ref, ob_ref, mb_ref, lb_ref):
    seq = q_ref.shape[0]
    c = DIL_BLOCK

    def stage(i, carry):
        rows = _tile_rows(i, PREP_TILE)
        qs_ref[rows, :] = q_ref[rows, :].astype(F32)
        ks_ref[rows, :] = k_ref[rows, :].astype(F32)
        vs_ref[rows, :] = v_ref[rows, :].astype(F32)
        return carry

    lax.fori_loop(0, seq // PREP_TILE, stage, 0)

    head0 = lax.broadcasted_iota(jnp.int32, (c, LANES), 1) < DIL_DH
    dist = lax.broadcasted_iota(jnp.int32, (c, 2 * c), 0) + c - lax.broadcasted_iota(jnp.int32, (c, 2 * c), 1)
    band = (dist >= 0) & (dist <= c)
    causal = lax.broadcasted_iota(jnp.int32, (c, c), 0) >= lax.broadcasted_iota(jnp.int32, (c, c), 1)
    band_bias = jnp.where(band, 0.0, -jnp.inf)
    causal_bias = jnp.where(causal, 0.0, -jnp.inf)

    def class_rows(r, dil):
        return pl.ds(r, seq // dil, stride=dil) if dil > 1 else pl.ds(0, seq)

    def tile_keys(strip, u):
        return strip[max(u - 1, 0) * c:(u + 1) * c]

    def branch(dil):
        blocks = seq // dil // c
        strips = {}

        def strip(ref, r, prepare):
            if (id(ref), r) not in strips:
                strips[id(ref), r] = prepare(ref[class_rows(r, dil), :])
            return strips[id(ref), r]

        def by_head(x, other):
            first = lax.broadcasted_iota(jnp.int32, x.shape, 1) < DIL_DH
            return jnp.where(first, x, other).astype(BF16), jnp.where(first, other, x).astype(BF16)

        def issue(r, u):
            qu = strip(qs_ref, r, lambda q: q)[u * c:(u + 1) * c]
            ku = tile_keys(strip(ks_ref, r, lambda k: k.astype(BF16)), u)
            bias = band_bias if u > 0 else causal_bias
            return [bias + _mm_nt(qh, ku) for qh in by_head(qu, 0.0)]

        def finish(r, u, scores):
            res, mx = [], []
            for s, vh in zip(scores, strip(vs_ref, r, lambda v: by_head(v, 1.0))):
                m = jnp.max(s, axis=-1, keepdims=True)
                res.append(_mm(jnp.exp(s - m).astype(BF16), tile_keys(vh, u)))
                mx.append(jnp.broadcast_to(m, (c, LANES)))
            start = r + u * c * dil
            rows = pl.ds(start, c, stride=dil) if dil > 1 else pl.ds(start, c)
            return (rows, jnp.where(head0, res[0], res[1]), jnp.where(head0, mx[0], mx[1]),
                    jnp.where(head0, res[1], res[0]))

        tiles = [(r, u) for r in range(dil) for u in range(blocks)]
        issued = []
        for i, tile in enumerate(tiles):
            issued.append(issue(*tile))
            if i >= DIL_LOOKAHEAD:
                yield finish(*tiles[i - DIL_LOOKAHEAD], issued[i - DIL_LOOKAHEAD])
        for i in range(max(len(tiles) - DIL_LOOKAHEAD, 0), len(tiles)):
            yield finish(*tiles[i], issued[i])

    n_br = len(DIL_PATTERNS)
    for br, (window, dil) in enumerate(DIL_PATTERNS):
        assert window // dil == c
        for rows, o, m, l in branch(dil):
            ob_ref[br, rows, :] = o
            mb_ref[br, rows, :] = m
            lb_ref[br, rows, :] = l

    def combine(i, carry):
        rows = _tile_rows(i, COMBINE_TILE)
        ms = [mb_ref[b, rows, :] for b in range(n_br)]
        m = functools.reduce(jnp.maximum, ms)
        es = [jnp.exp(x - m) for x in ms]
        num = sum(es[b] * ob_ref[b, rows, :] for b in range(n_br))
        den = sum(es[b] * pltpu.roll(lb_ref[b, rows, :], DIL_DH, 1) for b in range(n_br))
        o_ref[rows, :] = (num / den).astype(BF16)
        return carry

    lax.fori_loop(0, seq // COMBINE_TILE, combine, 0)


def _dilated(p, batch, seq):
    pairs = DIL_WIDTH // LANES
    pair_block = lambda first: pl.BlockSpec((None, None, seq, LANES), lambda b, j: (first + j, b, 0, 0))
    n_br = len(DIL_PATTERNS)
    return pl.pallas_call(
        _dilated_kernel,
        grid=(batch, pairs),
        in_specs=[pair_block(EVEN_DQ), pair_block(EVEN_DK), pair_block(EVEN_DV)],
        out_specs=pl.BlockSpec((None, None, seq, LANES), lambda b, j: (j, b, 0, 0)),
        out_shape=jax.ShapeDtypeStruct((pairs, batch, seq, LANES), BF16),
        scratch_shapes=[pltpu.VMEM((seq, LANES), F32)] * 3 + [pltpu.VMEM((n_br, seq, LANES), F32)] * 3,
        compiler_params=_params("parallel", "parallel"),
        name="dilated",
    )(p, p, p)


def _inproj_gla_kernel(x_ref, g_ref, w_ref, wa_ref, ba_ref, qk_ref, vr_ref, dec_ref):
    c = GLA_CHUNK
    per_tile = ROW_TILE // c
    kw, vw = GLA_KW, GLA_VW
    xg, r = _norm_parts(x_ref[...], g_ref[...])
    proj = lambda lo, hi: _mm(xg, w_ref[:, lo:hi]) * r
    a_low = proj(2 * kw + 2 * vw, GLA_IN_PADDED).astype(BF16)
    z = _mm(a_low, wa_ref[...]) + ba_ref[...]
    slabs = []
    for j in range(2 * GLA_HEADS):
        slabs.append(proj(2 * kw + j * GLA_DV, 2 * kw + (j + 1) * GLA_DV))
        vr_ref[j] = slabs[j].astype(BF16)
    q = proj(0, kw) * (GLA_DK ** -0.5)
    k = proj(kw, 2 * kw)
    row = lax.broadcasted_iota(jnp.int32, (c, kw), 0)
    for j in range(per_tile):
        rows = slice(j * c, (j + 1) * c)
        anchor = 0.0 * slabs[j % len(slabs)][rows, :LANES]
        zj = z[rows] + jnp.concatenate([anchor] * (kw // LANES), axis=1)
        b = (jnp.minimum(zj, 0.0) - jnp.log1p(jnp.exp(-jnp.abs(zj)))) * (1.0 / GLA_TAU)
        shift = 1
        while shift < c:
            b = b + jnp.where(row >= shift, pltpu.roll(b, shift, 0), 0.0)
            shift *= 2
        b_last = b[c - 1:c, :]
        dec_ref[j] = jnp.exp(b_last)
        operands = (q[rows] * jnp.exp(b), k[rows] * jnp.exp(-b), k[rows] * jnp.exp(b_last - b))
        for i, operand in enumerate(operands):
            for h in range(GLA_HEADS):
                qk_ref[i * GLA_HEADS + h, rows, :] = operand[:, h * GLA_DK:(h + 1) * GLA_DK].astype(BF16)


def _inproj_gla(x, gain, weight, wa, ba):
    t, d = x.shape
    per_tile = ROW_TILE // GLA_CHUNK
    operands = [(x, pl.BlockSpec((ROW_TILE, d), lambda i: (i, 0))), gain, weight, wa, ba]
    return pl.pallas_call(
        _inproj_gla_kernel,
        grid=(t // ROW_TILE,),
        in_specs=[spec for _, spec in operands],
        out_specs=[pl.BlockSpec((3 * GLA_HEADS, ROW_TILE, GLA_DK), lambda i: (0, i, 0)),
                   pl.BlockSpec((2 * GLA_HEADS, ROW_TILE, GLA_DV), lambda i: (0, i, 0)),
                   pl.BlockSpec((per_tile, 1, GLA_KW), lambda i: (i, 0, 0))],
        out_shape=[jax.ShapeDtypeStruct((3 * GLA_HEADS, t, GLA_DK), BF16),
                   jax.ShapeDtypeStruct((2 * GLA_HEADS, t, GLA_DV), BF16),
                   jax.ShapeDtypeStruct((t // GLA_CHUNK, 1, GLA_KW), F32)],
        compiler_params=_params("parallel"),
        name="inproj_gla",
    )(*[a for a, _ in operands])


def _gla_kernel(qt_ref, kt_ref, kl_ref, v_ref, r_ref, dec_ref, o_ref):
    c = GLA_CHUNK
    seq = qt_ref.shape[0]
    causal = lax.broadcasted_iota(jnp.int32, (c, c), 0) >= lax.broadcasted_iota(jnp.int32, (c, c), 1)

    def body(n, state):
        chunks = [n * CHUNK_GROUP + g for g in range(CHUNK_GROUP)]

        def issue(j):
            r = _tile_rows(j, c)
            v, q_t = v_ref[r, :], qt_ref[r, :]
            return r, v, q_t, _mm_nt(q_t, kt_ref[r, :]), _mm_tn(v, kl_ref[r, :])

        def advance(j, r, v, q_t, s, kv, state):
            o = _mm(jnp.where(causal, s, 0.0).astype(BF16), v) + _mm_nt(q_t, state.astype(BF16))
            o = _head_norm(o) * _silu(r_ref[r, :].astype(F32))
            o_ref[r, :] = o.astype(BF16)
            return state * dec_ref[j] + kv

        issued = []
        for g, j in enumerate(chunks):
            issued.append(issue(j))
            if g >= CHUNK_LOOKAHEAD:
                state = advance(chunks[g - CHUNK_LOOKAHEAD], *issued[g - CHUNK_LOOKAHEAD], state)
        for g in range(max(len(chunks) - CHUNK_LOOKAHEAD, 0), len(chunks)):
            state = advance(chunks[g], *issued[g], state)
        return state

    lax.fori_loop(0, seq // c // CHUNK_GROUP, body, jnp.zeros((GLA_DV, GLA_DK), F32))


def _gla(qk, vr, dec, batch, seq):
    h = GLA_HEADS
    slab = lambda first, width: pl.BlockSpec((None, None, seq, width), lambda b, j: (first + j, b, 0, 0))
    return pl.pallas_call(
        _gla_kernel,
        grid=(batch, h),
        in_specs=[slab(0, GLA_DK), slab(h, GLA_DK), slab(2 * h, GLA_DK), slab(0, GLA_DV), slab(h, GLA_DV),
                  pl.BlockSpec((None, seq // GLA_CHUNK, 1, GLA_DK), lambda b, j: (b, 0, 0, j))],
        out_specs=slab(0, GLA_DV),
        out_shape=jax.ShapeDtypeStruct((h, batch, seq, GLA_DV), BF16),
        compiler_params=_params("parallel", "parallel"),
        name="gla",
    )(qk, qk, qk, vr, vr, dec)


def kernel(x, ffn_pre_norm, ffn_pre_w_gate, ffn_pre_w_up, ffn_pre_w_down, mix_norm, ab_w_in, ab_w_out,
           gla_w_in, gla_w_a2, gla_b_a, gla_w_out, ffn_post_norm, ffn_post_w_gate, ffn_post_w_up,
           ffn_post_w_down, final_norm):
    batch, seq, d = x.shape
    depth = ffn_pre_norm.shape[0]
    bf = lambda w: w.astype(BF16)
    gains = lambda g: g.reshape(g.shape[0], 1, d)
    assert RET_HEADS * RET_DV == DIL_WIDTH

    pre = [gains(ffn_pre_norm), bf(ffn_pre_w_gate), bf(ffn_pre_w_up), bf(ffn_pre_w_down)]
    post = [gains(ffn_post_norm), bf(ffn_post_w_gate), bf(ffn_post_w_up), bf(ffn_post_w_down)]
    mix_g = gains(mix_norm)
    ab_in, ab_out = bf(ab_w_in), bf(ab_w_out)
    gla_in = jnp.pad(bf(gla_w_in), ((0, 0), (0, 0), (0, GLA_IN_PADDED - gla_w_in.shape[2])))
    gla_a2 = jnp.pad(bf(gla_w_a2), ((0, 0), (0, LANES - gla_w_a2.shape[1]), (0, 0)))
    gla_ba = gla_b_a.reshape(gla_b_a.shape[0], 1, -1)
    gla_out = bf(gla_w_out)
    tables = _even_tables(seq)

    xt = x.reshape(batch * seq, d)
    for l in range(depth):
        xt = _ffn(xt, [], [_layer(w, l) for w in pre])
        if l % 2 == 0:
            p = _inproj_even(xt, _layer(mix_g, l), _layer(ab_in, l // 2), tables, seq)
            p = p.reshape(EVEN_SLABS, batch, seq, LANES)
            o_r = _retention(p, tables, batch, seq).reshape(RET_HEADS, batch * seq, RET_DV)
            o_d = _dilated(p, batch, seq).reshape(DIL_WIDTH // LANES, batch * seq, LANES)
            mixes = [(o_r, _layer(ab_out, l // 2, rows=DIL_WIDTH, row_block=0)),
                     (o_d, _layer(ab_out, l // 2, rows=DIL_WIDTH, row_block=1))]
        else:
            qk, vr, dec = _inproj_gla(xt, _layer(mix_g, l), _layer(gla_in, l // 2), _layer(gla_a2, l // 2),
                                      _layer(gla_ba, l // 2))
            o = _gla(qk.reshape(3 * GLA_HEADS, batch, seq, GLA_DK), vr.reshape(2 * GLA_HEADS, batch, seq, GLA_DV),
                     dec.reshape(batch, seq // GLA_CHUNK, 1, GLA_KW), batch, seq)
            mixes = [(o.reshape(GLA_HEADS, batch * seq, GLA_DV), _layer(gla_out, l // 2))]
        xt = _ffn(xt, mixes, [_layer(w, l) for w in post],
                  final_g=final_norm.reshape(1, d) if l == depth - 1 else None)
    return xt.reshape(batch, seq, d)
```

```python
import functools

import jax
import jax.numpy as jnp
import numpy as np
from jax import lax
from jax.experimental import pallas as pl
from jax.experimental.pallas import tpu as pltpu

F32 = jnp.float32
BF16 = jnp.bfloat16

LANES = 128
VMEM_LIMIT_BYTES = 56 * 1024 * 1024

D_MODEL = 1024
FFN_HIDDEN = 2816
EPS = 1e-6

RET_HEADS = 4
RET_DK = 128
RET_DV = 128
RET_CHUNK = 128
RET_THETA = 10000.0
RET_WIDTH = RET_HEADS * RET_DK

DIL_HEADS = 8
DIL_DH = 64
DIL_PATTERNS = ((128, 1), (512, 4), (2048, 16))
DIL_BLOCK = 128
DIL_LOOKAHEAD = 3
ROPE_THETA = 500000.0
ROT_DIM = DIL_DH // 4
DIL_WIDTH = DIL_HEADS * DIL_DH

GLA_HEADS = 4
GLA_DK = (D_MODEL // 2) // GLA_HEADS
GLA_DV = D_MODEL // GLA_HEADS
GLA_LOWRANK = 16
GLA_TAU = 16.0
GLA_CHUNK = 64
GLA_KW = GLA_HEADS * GLA_DK
GLA_VW = GLA_HEADS * GLA_DV

CHUNK_GROUP = 32
CHUNK_LOOKAHEAD = 16

EVEN_RQ, EVEN_RQD, EVEN_RK, EVEN_RKD, EVEN_RV, EVEN_RG, EVEN_DQ, EVEN_DK, EVEN_DV = range(0, 36, 4)
EVEN_SLABS = 9 * RET_WIDTH // LANES
GLA_IN_PADDED = 2 * GLA_KW + 2 * GLA_VW + LANES

ROW_TILE = 512
PREP_TILE = 512
COMBINE_TILE = 1024

NT_DIMS = (((1,), (1,)), ((), ()))
TN_DIMS = (((0,), (0,)), ((), ()))


def _params(*semantics):
    return pltpu.CompilerParams(dimension_semantics=semantics,
                                vmem_limit_bytes=VMEM_LIMIT_BYTES)


def _resident(a):
    zeros = (0,) * a.ndim
    return a, pl.BlockSpec(a.shape, lambda *_: zeros, pipeline_mode=pl.Buffered(1))


def _layer(w, l, rows=None, row_block=0):
    shape = (None, w.shape[1] if rows is None else rows, w.shape[2])
    return w, pl.BlockSpec(shape, lambda *_: (l, row_block, 0), pipeline_mode=pl.Buffered(1))


def _rmsnorm(x, g):
    return x * lax.rsqrt(jnp.mean(x * x, axis=-1, keepdims=True) + EPS) * g


def _norm_parts(x, g):
    return (x * g).astype(BF16), lax.rsqrt(jnp.mean(x * x, axis=-1, keepdims=True) + EPS)


def _head_norm(o):
    mu = jnp.mean(o, axis=-1, keepdims=True)
    c = o - mu
    return c * lax.rsqrt(jnp.mean(c * c, axis=-1, keepdims=True) + EPS)


def _silu(x):
    return x * (0.5 * jnp.tanh(0.5 * x) + 0.5)


def _mm(a, b):
    return jnp.dot(a, b, preferred_element_type=F32)


def _mm_nt(a, b):
    return lax.dot_general(a, b, NT_DIMS, preferred_element_type=F32)


def _mm_tn(a, b):
    return lax.dot_general(a, b, TN_DIMS, preferred_element_type=F32)


def _tile_rows(i, n):
    return pl.ds(pl.multiple_of(i * n, n), n)


def _lane_block(j, n=1):
    return slice(j * LANES, (j + n) * LANES)


def _ffn_kernel(*refs, n_mix, final):
    x_ref = refs[0]
    mix_refs = refs[1:1 + 2 * n_mix]
    g_ref, wg_ref, wu_ref, wd_ref = refs[1 + 2 * n_mix:5 + 2 * n_mix]
    o_ref = refs[-1]
    x = x_ref[...]
    for i in range(n_mix):
        mix_ref = mix_refs[2 * i]
        mix = jnp.concatenate([mix_ref[g] for g in range(mix_ref.shape[0])], axis=1)
        x = x + _mm(mix, mix_refs[2 * i + 1][...])
    xg, r = _norm_parts(x, g_ref[...])
    half_r = 0.5 * r
    h = _mm(xg, wg_ref[...]) * half_r
    act = (h * (jnp.tanh(h) + 1.0) * _mm(xg, wu_ref[...])).astype(BF16)
    y = x + half_r * _mm(act, wd_ref[...])
    if final:
        y = _rmsnorm(y, refs[-2][...])
    o_ref[...] = y


def _ffn(x, mixes, weights, final_g=None):
    t, d = x.shape
    row = lambda w: pl.BlockSpec((ROW_TILE, w), lambda i: (i, 0))
    args, specs = [x], [row(d)]
    for m, (w, spec) in mixes:
        args += [m, w]
        specs += [pl.BlockSpec((m.shape[0], ROW_TILE, m.shape[2]), lambda i: (0, i, 0)), spec]
    for w, spec in weights + ([_resident(final_g)] if final_g is not None else []):
        args.append(w)
        specs.append(spec)
    return pl.pallas_call(
        functools.partial(_ffn_kernel, n_mix=len(mixes), final=final_g is not None),
        grid=(t // ROW_TILE,),
        in_specs=specs,
        out_specs=row(d),
        out_shape=jax.ShapeDtypeStruct((t, d), F32),
        compiler_params=_params("parallel"),
        name="ffn",
    )(*args)


def _inproj_even_kernel(x_ref, g_ref, w_ref, rcos_ref, rsin_ref, qd_ref, kd_ref,
                        dcos_ref, dsina_ref, dsinb_ref, o_ref):
    xg, r = _norm_parts(x_ref[...], g_ref[...])
    proj = lambda first_block, n_blocks: _mm(xg, w_ref[:, _lane_block(first_block, n_blocks)]) * r
    heads = RET_WIDTH // LANES
    half = ROT_DIM // 2

    rcos, rsin = rcos_ref[...], rsin_ref[...]
    ret_rot = lambda x: x * rcos + pltpu.roll(x, RET_DK // 2, 1) * rsin
    q = proj(0, heads)
    for j in range(heads):
        qj = ret_rot(q[:, _lane_block(j)])
        o_ref[EVEN_RQ + j] = qj.astype(BF16)
        o_ref[EVEN_RQD + j] = (qj * qd_ref[:, _lane_block(j)]).astype(BF16)
    k = proj(heads, heads)
    for j in range(heads):
        kj = ret_rot(k[:, _lane_block(j)]) * (RET_DK ** -0.5)
        o_ref[EVEN_RK + j] = kj.astype(BF16)
        o_ref[EVEN_RKD + j] = (kj * kd_ref[:, _lane_block(j)]).astype(BF16)
    vg = proj(2 * heads, 2 * heads)
    for j in range(2 * heads):
        o_ref[EVEN_RV + j] = vg[:, _lane_block(j)].astype(BF16)

    dcos, dsina, dsinb = dcos_ref[...], dsina_ref[...], dsinb_ref[...]
    dil_rot = lambda x: x * dcos + pltpu.roll(x, LANES - half, 1) * dsina + pltpu.roll(x, half, 1) * dsinb
    pairs = DIL_WIDTH // LANES
    dq = proj(4 * heads, pairs)
    for j in range(pairs):
        o_ref[EVEN_DQ + j] = (dil_rot(dq[:, _lane_block(j)]) * (DIL_DH ** -0.5)).astype(BF16)
    dk = proj(4 * heads + pairs, pairs)
    for j in range(pairs):
        o_ref[EVEN_DK + j] = dil_rot(dk[:, _lane_block(j)]).astype(BF16)
    dv = proj(4 * heads + 2 * pairs, pairs)
    for j in range(pairs):
        o_ref[EVEN_DV + j] = dv[:, _lane_block(j)].astype(BF16)


def _even_tables(seq):
    c = RET_CHUNK
    log_g = np.log(1.0 - 2.0 ** (-5.0 - np.arange(RET_HEADS, dtype=np.float64)))
    i = np.arange(c, dtype=np.float64)
    rel = i[:, None] - i[None, :]
    dmask = np.where(rel >= 0, np.exp(log_g[:, None, None] * np.maximum(rel, 0.0)), 0.0)
    spread = lambda a: np.tile(np.repeat(a.T, RET_DK, axis=1), (ROW_TILE // c, 1))
    q_decay = spread(np.exp(log_g[:, None] * (i + 1.0)))
    k_decay = spread(np.exp(log_g[:, None] * (c - 1 - i)))
    chunk_decay = np.exp(log_g * c)[:, None, None] * np.ones((1, 1, RET_DV))
    pos = np.arange(seq, dtype=np.float32)[:, None]

    freq = np.float32(RET_THETA) ** (-np.linspace(0.0, 1.0, RET_DK // 2, dtype=np.float32))
    ang = pos * freq[None, :]
    rcos = np.concatenate([np.cos(ang), np.cos(ang)], axis=1)
    rsin = np.concatenate([-np.sin(ang), np.sin(ang)], axis=1)

    freq = np.float32(ROPE_THETA) ** (-np.arange(0, ROT_DIM, 2, dtype=np.float32) / np.float32(ROT_DIM))
    ang = pos * freq[None, :]
    cos, sin = np.cos(ang), np.sin(ang)
    zeros = np.zeros_like(sin)
    head = lambda a, b, fill: np.concatenate([a, b, np.full((seq, DIL_DH - ROT_DIM), fill, np.float32)], axis=1)
    two = lambda x: np.concatenate([x, x], axis=1)
    dcos, dsina, dsinb = two(head(cos, cos, 1.0)), two(head(-sin, zeros, 0.0)), two(head(zeros, sin, 0.0))
    f = lambda a: np.asarray(a, np.float32)
    return dict(rcos=f(rcos), rsin=f(rsin), q_decay=f(q_decay), k_decay=f(k_decay), dcos=f(dcos),
                dsina=f(dsina), dsinb=f(dsinb), dmask=f(dmask), chunk_decay=f(chunk_decay))


def _inproj_even(x, gain, weight, tables, seq):
    t, d = x.shape
    tiles_per_seq = seq // ROW_TILE
    by_position = lambda a: (a, pl.BlockSpec((ROW_TILE, LANES), lambda i: (i % tiles_per_seq, 0)))
    operands = [(x, pl.BlockSpec((ROW_TILE, d), lambda i: (i, 0))), gain, weight,
                by_position(tables["rcos"]), by_position(tables["rsin"]),
                _resident(tables["q_decay"]), _resident(tables["k_decay"]),
                by_position(tables["dcos"]), by_position(tables["dsina"]), by_position(tables["dsinb"])]
    return pl.pallas_call(
        _inproj_even_kernel,
        grid=(t // ROW_TILE,),
        in_specs=[spec for _, spec in operands],
        out_specs=pl.BlockSpec((EVEN_SLABS, ROW_TILE, LANES), lambda i: (0, i, 0)),
        out_shape=jax.ShapeDtypeStruct((EVEN_SLABS, t, LANES), BF16),
        compiler_params=_params("parallel"),
        name="inproj_even",
    )(*[a for a, _ in operands])


def _retention_kernel(q_ref, qdec_ref, k_ref, kdec_ref, v_ref, g_ref, dm_ref, cd_ref, o_ref):
    c = RET_CHUNK
    seq = q_ref.shape[0]
    dmask = dm_ref[...]
    chunk_decay = cd_ref[...]

    def body(n, state):
        rows = [_tile_rows(n * CHUNK_GROUP + g, c) for g in range(CHUNK_GROUP)]

        def issue(r):
            v = v_ref[r, :]
            return v, _mm_nt(q_ref[r, :], k_ref[r, :]), _mm_tn(kdec_ref[r, :], v)

        def advance(r, v, s, kv, state):
            lhs = jnp.concatenate([(s * dmask).astype(BF16), qdec_ref[r, :]], axis=1)
            rhs = jnp.concatenate([v, state.astype(BF16)], axis=0)
            o = _head_norm(_mm(lhs, rhs)) * _silu(g_ref[r, :].astype(F32))
            o_ref[r, :] = o.astype(BF16)
            return state * chunk_decay + kv

        issued = []
        for g, r in enumerate(rows):
            issued.append(issue(r))
            if g >= CHUNK_LOOKAHEAD:
                state = advance(rows[g - CHUNK_LOOKAHEAD], *issued[g - CHUNK_LOOKAHEAD], state)
        for g in range(max(len(rows) - CHUNK_LOOKAHEAD, 0), len(rows)):
            state = advance(rows[g], *issued[g], state)
        return state

    lax.fori_loop(0, seq // c // CHUNK_GROUP, body, jnp.zeros((RET_DK, RET_DV), F32))


def _retention(p, tables, batch, seq):
    head_block = lambda first: pl.BlockSpec((None, None, seq, RET_DK), lambda b, j: (first + j, b, 0, 0))
    per_head = lambda a: pl.BlockSpec((None,) + a.shape[1:], lambda b, j: (j, 0, 0))
    dmask, chunk_decay = tables["dmask"], tables["chunk_decay"]
    return pl.pallas_call(
        _retention_kernel,
        grid=(batch, RET_HEADS),
        in_specs=[head_block(EVEN_RQ), head_block(EVEN_RQD), head_block(EVEN_RK), head_block(EVEN_RKD),
                  head_block(EVEN_RV), head_block(EVEN_RG), per_head(dmask), per_head(chunk_decay)],
        out_specs=pl.BlockSpec((None, None, seq, RET_DV), lambda b, j: (j, b, 0, 0)),
        out_shape=jax.ShapeDtypeStruct((RET_HEADS, batch, seq, RET_DV), BF16),
        compiler_params=_params("parallel", "parallel"),
        name="retention",
    )(p, p, p, p, p, p, dmask, chunk_decay)


def _dilated_kernel(q_ref, k_ref, v_ref, o_ref, qs_ref, ks_ref, vs_ref, ob_ref, mb_ref, lb_ref):
    seq = q_ref.shape[0]
    c = DIL_BLOCK

    def stage(i, carry):
        rows = _tile_rows(i, PREP_TILE)
        qs_ref[rows, :] = q_ref[rows, :].astype(F32)
        ks_ref[rows, :] = k_ref[rows, :].astype(F32)
        vs_ref[rows, :] = v_ref[rows, :].astype(F32)
        return carry

    lax.fori_loop(0, seq // PREP_TILE, stage, 0)

    head0 = lax.broadcasted_iota(jnp.int32, (c, LANES), 1) < DIL_DH
    dist = lax.broadcasted_iota(jnp.int32, (c, 2 * c), 0) + c - lax.broadcasted_iota(jnp.int32, (c, 2 * c), 1)
    band = (dist >= 0) & (dist <= c)
    causal = lax.broadcasted_iota(jnp.int32, (c, c), 0) >= lax.broadcasted_iota(jnp.int32, (c, c), 1)
    band_bias = jnp.where(band, 0.0, -jnp.inf)
    causal_bias = jnp.where(causal, 0.0, -jnp.inf)

    def class_rows(r, dil):
        return pl.ds(r, seq // dil, stride=dil) if dil > 1 else pl.ds(0, seq)

    def tile_keys(strip, u):
        return strip[max(u - 1, 0) * c:(u + 1) * c]

    def branch(dil):
        blocks = seq // dil // c
        strips = {}

        def strip(ref, r, prepare):
            if (id(ref), r) not in strips:
                strips[id(ref), r] = prepare(ref[class_rows(r, dil), :])
            return strips[id(ref), r]

        def by_head(x, other):
            first = lax.broadcasted_iota(jnp.int32, x.shape, 1) < DIL_DH
            return jnp.where(first, x, other).astype(BF16), jnp.where(first, other, x).astype(BF16)

        def issue(r, u):
            qu = strip(qs_ref, r, lambda q: q)[u * c:(u + 1) * c]
            ku = tile_keys(strip(ks_ref, r, lambda k: k.astype(BF16)), u)
            bias = band_bias if u > 0 else causal_bias
            return [bias + _mm_nt(qh, ku) for qh in by_head(qu, 0.0)]

        def finish(r, u, scores):
            res, mx = [], []
            for s, vh in zip(scores, strip(vs_ref, r, lambda v: by_head(v, 1.0))):
                m = jnp.max(s, axis=-1, keepdims=True)
                res.append(_mm(jnp.exp(s - m).astype(BF16), tile_keys(vh, u)))
                mx.append(jnp.broadcast_to(m, (c, LANES)))
            start = r + u * c * dil
            rows = pl.ds(start, c, stride=dil) if dil > 1 else pl.ds(start, c)
            return (rows, jnp.where(head0, res[0], res[1]), jnp.where(head0, mx[0], mx[1]),
                    jnp.where(head0, res[1], res[0]))

        tiles = [(r, u) for r in range(dil) for u in range(blocks)]
        issued = []
        for i, tile in enumerate(tiles):
            issued.append(issue(*tile))
            if i >= DIL_LOOKAHEAD:
                yield finish(*tiles[i - DIL_LOOKAHEAD], issued[i - DIL_LOOKAHEAD])
        for i in range(max(len(tiles) - DIL_LOOKAHEAD, 0), len(tiles)):
            yield finish(*tiles[i], issued[i])

    n_br = len(DIL_PATTERNS)
    for br, (window, dil) in enumerate(DIL_PATTERNS):
        assert window // dil == c
        for rows, o, m, l in branch(dil):
            ob_ref[br, rows, :] = o
            mb_ref[br, rows, :] = m
            lb_ref[br, rows, :] = l

    def combine(i, carry):
        rows = _tile_rows(i, COMBINE_TILE)
        ms = [mb_ref[b, rows, :] for b in range(n_br)]
        m = functools.reduce(jnp.maximum, ms)
        es = [jnp.exp(x - m) for x in ms]
        num = sum(es[b] * ob_ref[b, rows, :] for b in range(n_br))
        den = sum(es[b] * pltpu.roll(lb_ref[b, rows, :], DIL_DH, 1) for b in range(n_br))
        o_ref[rows, :] = (num / den).astype(BF16)
        return carry

    lax.fori_loop(0, seq // COMBINE_TILE, combine, 0)


def _dilated(p, batch, seq):
    pairs = DIL_WIDTH // LANES
    pair_block = lambda first: pl.BlockSpec((None, None, seq, LANES), lambda b, j: (first + j, b, 0, 0))
    n_br = len(DIL_PATTERNS)
    return pl.pallas_call(
        _dilated_kernel,
        grid=(batch, pairs),
        in_specs=[pair_block(EVEN_DQ), pair_block(EVEN_DK), pair_block(EVEN_DV)],
        out_specs=pl.BlockSpec((None, None, seq, LANES), lambda b, j: (j, b, 0, 0)),
        out_shape=jax.ShapeDtypeStruct((pairs, batch, seq, LANES), BF16),
        scratch_shapes=[pltpu.VMEM((seq, LANES), F32)] * 3 + [pltpu.VMEM((n_br, seq, LANES), F32)] * 3,
        compiler_params=_params("parallel", "parallel"),
        name="dilated",
    )(p, p, p)


def _inproj_gla_kernel(x_ref, g_ref, w_ref, wa_ref, ba_ref, qk_ref, vr_ref, dec_ref):
    c = GLA_CHUNK
    per_tile = ROW_TILE // c
    kw, vw = GLA_KW, GLA_VW
    xg, r = _norm_parts(x_ref[...], g_ref[...])
    proj = lambda lo, hi: _mm(xg, w_ref[:, lo:hi]) * r
    a_low = proj(2 * kw + 2 * vw, GLA_IN_PADDED).astype(BF16)
    z = _mm(a_low, wa_ref[...]) + ba_ref[...]
    slabs = []
    for j in range(2 * GLA_HEADS):
        slabs.append(proj(2 * kw + j * GLA_DV, 2 * kw + (j + 1) * GLA_DV))
        vr_ref[j] = slabs[j].astype(BF16)
    q = proj(0, kw) * (GLA_DK ** -0.5)
    k = proj(kw, 2 * kw)
    row = lax.broadcasted_iota(jnp.int32, (c, kw), 0)
    for j in range(per_tile):
        rows = slice(j * c, (j + 1) * c)
        anchor = 0.0 * slabs[j % len(slabs)][rows, :LANES]
        zj = z[rows] + jnp.concatenate([anchor] * (kw // LANES), axis=1)
        b = (jnp.minimum(zj, 0.0) - jnp.log1p(jnp.exp(-jnp.abs(zj)))) * (1.0 / GLA_TAU)
        shift = 1
        while shift < c:
            b = b + jnp.where(row >= shift, pltpu.roll(b, shift, 0), 0.0)
            shift *= 2
        b_last = b[c - 1:c, :]
        dec_ref[j] = jnp.exp(b_last)
        operands = (q[rows] * jnp.exp(b), k[rows] * jnp.exp(-b), k[rows] * jnp.exp(b_last - b))
        for i, operand in enumerate(operands):
            for h in range(GLA_HEADS):
                qk_ref[i * GLA_HEADS + h, rows, :] = operand[:, h * GLA_DK:(h + 1) * GLA_DK].astype(BF16)


def _inproj_gla(x, gain, weight, wa, ba):
    t, d = x.shape
    per_tile = ROW_TILE // GLA_CHUNK
    operands = [(x, pl.BlockSpec((ROW_TILE, d), lambda i: (i, 0))), gain, weight, wa, ba]
    return pl.pallas_call(
        _inproj_gla_kernel,
        grid=(t // ROW_TILE,),
        in_specs=[spec for _, spec in operands],
        out_specs=[pl.BlockSpec((3 * GLA_HEADS, ROW_TILE, GLA_DK), lambda i: (0, i, 0)),
                   pl.BlockSpec((2 * GLA_HEADS, ROW_TILE, GLA_DV), lambda i: (0, i, 0)),
                   pl.BlockSpec((per_tile, 1, GLA_KW), lambda i: (i, 0, 0))],
        out_shape=[jax.ShapeDtypeStruct((3 * GLA_HEADS, t, GLA_DK), BF16),
                   jax.ShapeDtypeStruct((2 * GLA_HEADS, t, GLA_DV), BF16),
                   jax.ShapeDtypeStruct((t // GLA_CHUNK, 1, GLA_KW), F32)],
        compiler_params=_params("parallel"),
        name="inproj_gla",
    )(*[a for a, _ in operands])


def _gla_kernel(qt_ref, kt_ref, kl_ref, v_ref, r_ref, dec_ref, o_ref):
    c = GLA_CHUNK
    seq = qt_ref.shape[0]
    causal = lax.broadcasted_iota(jnp.int32, (c, c), 0) >= lax.broadcasted_iota(jnp.int32, (c, c), 1)

    def body(n, state):
        chunks = [n * CHUNK_GROUP + g for g in range(CHUNK_GROUP)]

        def issue(j):
            r = _tile_rows(j, c)
            v, q_t = v_ref[r, :], qt_ref[r, :]
            return r, v, q_t, _mm_nt(q_t, kt_ref[r, :]), _mm_tn(v, kl_ref[r, :])

        def advance(j, r, v, q_t, s, kv, state):
            o = _mm(jnp.where(causal, s, 0.0).astype(BF16), v) + _mm_nt(q_t, state.astype(BF16))
            o = _head_norm(o) * _silu(r_ref[r, :].astype(F32))
            o_ref[r, :] = o.astype(BF16)
            return state * dec_ref[j] + kv

        issued = []
        for g, j in enumerate(chunks):
            issued.append(issue(j))
            if g >= CHUNK_LOOKAHEAD:
                state = advance(chunks[g - CHUNK_LOOKAHEAD], *issued[g - CHUNK_LOOKAHEAD], state)
        for g in range(max(len(chunks) - CHUNK_LOOKAHEAD, 0), len(chunks)):
            state = advance(chunks[g], *issued[g], state)
        return state

    lax.fori_loop(0, seq // c // CHUNK_GROUP, body, jnp.zeros((GLA_DV, GLA_DK), F32))


def _gla(qk, vr, dec, batch, seq):
    h = GLA_HEADS
    slab = lambda first, width: pl.BlockSpec((None, None, seq, width), lambda b, j: (first + j, b, 0, 0))
    return pl.pallas_call(
        _gla_kernel,
        grid=(batch, h),
        in_specs=[slab(0, GLA_DK), slab(h, GLA_DK), slab(2 * h, GLA_DK), slab(0, GLA_DV), slab(h, GLA_DV),
                  pl.BlockSpec((None, seq // GLA_CHUNK, 1, GLA_DK), lambda b, j: (b, 0, 0, j))],
        out_specs=slab(0, GLA_DV),
        out_shape=jax.ShapeDtypeStruct((h, batch, seq, GLA_DV), BF16),
        compiler_params=_params("parallel", "parallel"),
        name="gla",
    )(qk, qk, qk, vr, vr, dec)


def kernel(x, ffn_pre_norm, ffn_pre_w_gate, ffn_pre_w_up, ffn_pre_w_down, mix_norm, ab_w_in, ab_w_out,
           gla_w_in, gla_w_a2, gla_b_a, gla_w_out, ffn_post_norm, ffn_post_w_gate, ffn_post_w_up,
           ffn_post_w_down, final_norm):
    batch, seq, d = x.shape
    depth = ffn_pre_norm.shape[0]
    bf = lambda w: w.astype(BF16)
    gains = lambda g: g.reshape(g.shape[0], 1, d)
    assert RET_HEADS * RET_DV == DIL_WIDTH

    pre = [gains(ffn_pre_norm), bf(ffn_pre_w_gate), bf(ffn_pre_w_up), bf(ffn_pre_w_down)]
    post = [gains(ffn_post_norm), bf(ffn_post_w_gate), bf(ffn_post_w_up), bf(ffn_post_w_down)]
    mix_g = gains(mix_norm)
    ab_in, ab_out = bf(ab_w_in), bf(ab_w_out)
    gla_in = jnp.pad(bf(gla_w_in), ((0, 0), (0, 0), (0, GLA_IN_PADDED - gla_w_in.shape[2])))
    gla_a2 = jnp.pad(bf(gla_w_a2), ((0, 0), (0, LANES - gla_w_a2.shape[1]), (0, 0)))
    gla_ba = gla_b_a.reshape(gla_b_a.shape[0], 1, -1)
    gla_out = bf(gla_w_out)
    tables = _even_tables(seq)

    xt = x.reshape(batch * seq, d)
    for l in range(depth):
        xt = _ffn(xt, [], [_layer(w, l) for w in pre])
        if l % 2 == 0:
            p = _inproj_even(xt, _layer(mix_g, l), _layer(ab_in, l // 2), tables, seq)
            p = p.reshape(EVEN_SLABS, batch, seq, LANES)
            o_r = _retention(p, tables, batch, seq).reshape(RET_HEADS, batch * seq, RET_DV)
            o_d = _dilated(p, batch, seq).reshape(DIL_WIDTH // LANES, batch * seq, LANES)
            mixes = [(o_r, _layer(ab_out, l // 2, rows=DIL_WIDTH, row_block=0)),
                     (o_d, _layer(ab_out, l // 2, rows=DIL_WIDTH, row_block=1))]
        else:
            qk, vr, dec = _inproj_gla(xt, _layer(mix_g, l), _layer(gla_in, l // 2), _layer(gla_a2, l // 2),
                                      _layer(gla_ba, l // 2))
            o = _gla(qk.reshape(3 * GLA_HEADS, batch, seq, GLA_DK), vr.reshape(2 * GLA_HEADS, batch, seq, GLA_DV),
                     dec.reshape(batch, seq // GLA_CHUNK, 1, GLA_KW), batch, seq)
            mixes = [(o.reshape(GLA_HEADS, batch * seq, GLA_DV), _layer(gla_out, l // 2))]
        xt = _ffn(xt, mixes, [_layer(w, l) for w in post],
                  final_g=final_norm.reshape(1, d) if l == depth - 1 else None)
    return xt.reshape(batch, seq, d)
```

```python
import functools

import jax
import jax.numpy as jnp
import numpy as np
from jax import lax
from jax.experimental import pallas as pl
from jax.experimental.pallas import tpu as pltpu

F32 = jnp.float32
BF16 = jnp.bfloat16

LANES = 128
VMEM_LIMIT_BYTES = 56 * 1024 * 1024

D_MODEL = 1024
FFN_HIDDEN = 2816
EPS = 1e-6

RET_HEADS = 4
RET_DK = 128
RET_DV = 128
RET_CHUNK = 128
RET_THETA = 10000.0
RET_WIDTH = RET_HEADS * RET_DK

DIL_HEADS = 8
DIL_DH = 64
DIL_PATTERNS = ((128, 1), (512, 4), (2048, 16))
DIL_BLOCK = 128
DIL_LOOKAHEAD = 3
ROPE_THETA = 500000.0
ROT_DIM = DIL_DH // 4
DIL_WIDTH = DIL_HEADS * DIL_DH

GLA_HEADS = 4
GLA_DK = (D_MODEL // 2) // GLA_HEADS
GLA_DV = D_MODEL // GLA_HEADS
GLA_LOWRANK = 16
GLA_TAU = 16.0
GLA_CHUNK = 64
GLA_KW = GLA_HEADS * GLA_DK
GLA_VW = GLA_HEADS * GLA_DV

CHUNK_GROUP = 32
CHUNK_LOOKAHEAD = 16

EVEN_RQ, EVEN_RQD, EVEN_RK, EVEN_RKD, EVEN_RV, EVEN_RG, EVEN_DQ, EVEN_DK, EVEN_DV = range(0, 36, 4)
EVEN_SLABS = 9 * RET_WIDTH // LANES
GLA_IN_PADDED = 2 * GLA_KW + 2 * GLA_VW + LANES

ROW_TILE = 512
PREP_TILE = 512
COMBINE_TILE = 1024

NT_DIMS = (((1,), (1,)), ((), ()))
TN_DIMS = (((0,), (0,)), ((), ()))


def _params(*semantics):
    return pltpu.CompilerParams(dimension_semantics=semantics,
                                vmem_limit_bytes=VMEM_LIMIT_BYTES)


def _resident(a):
    zeros = (0,) * a.ndim
    return a, pl.BlockSpec(a.shape, lambda *_: zeros, pipeline_mode=pl.Buffered(1))


def _layer(w, l, rows=None, row_block=0):
    shape = (None, w.shape[1] if rows is None else rows, w.shape[2])
    return w, pl.BlockSpec(shape, lambda *_: (l, row_block, 0), pipeline_mode=pl.Buffered(1))


def _rmsnorm(x, g):
    return x * lax.rsqrt(jnp.mean(x * x, axis=-1, keepdims=True) + EPS) * g


def _norm_parts(x):
    return x.astype(BF16), lax.rsqrt(jnp.mean(x * x, axis=-1, keepdims=True) + EPS)


def _head_norm(o):
    mu = jnp.mean(o, axis=-1, keepdims=True)
    c = o - mu
    return c * lax.rsqrt(jnp.mean(c * c, axis=-1, keepdims=True) + EPS)


def _silu(x):
    return x * (0.5 * jnp.tanh(0.5 * x) + 0.5)


def _mm(a, b):
    return jnp.dot(a, b, preferred_element_type=F32)


def _mm_nt(a, b):
    return lax.dot_general(a, b, NT_DIMS, preferred_element_type=F32)


def _mm_tn(a, b):
    return lax.dot_general(a, b, TN_DIMS, preferred_element_type=F32)


def _tile_rows(i, n):
    return pl.ds(pl.multiple_of(i * n, n), n)


def _lane_block(j, n=1):
    return slice(j * LANES, (j + n) * LANES)


def _ffn_kernel(*refs, n_mix, final):
    x_ref = refs[0]
    mix_refs = refs[1:1 + 2 * n_mix]
    wg_ref, wu_ref, wd_ref = refs[1 + 2 * n_mix:4 + 2 * n_mix]
    o_ref = refs[-1]
    x = x_ref[...]
    for i in range(n_mix):
        mix_ref = mix_refs[2 * i]
        mix = jnp.concatenate([mix_ref[g] for g in range(mix_ref.shape[0])], axis=1)
        x = x + _mm(mix, mix_refs[2 * i + 1][...])
    xg, r = _norm_parts(x)
    half_r = 0.5 * r
    h = _mm(xg, wg_ref[...]) * half_r
    act = (h * (jnp.tanh(h) + 1.0) * _mm(xg, wu_ref[...])).astype(BF16)
    y = x + half_r * _mm(act, wd_ref[...])
    if final:
        y = _rmsnorm(y, refs[-2][...])
    o_ref[...] = y


def _ffn(x, mixes, weights, final_g=None):
    t, d = x.shape
    row = lambda w: pl.BlockSpec((ROW_TILE, w), lambda i: (i, 0))
    args, specs = [x], [row(d)]
    for m, (w, spec) in mixes:
        args += [m, w]
        specs += [pl.BlockSpec((m.shape[0], ROW_TILE, m.shape[2]), lambda i: (0, i, 0)), spec]
    for w, spec in weights + ([_resident(final_g)] if final_g is not None else []):
        args.append(w)
        specs.append(spec)
    return pl.pallas_call(
        functools.partial(_ffn_kernel, n_mix=len(mixes), final=final_g is not None),
        grid=(t // ROW_TILE,),
        in_specs=specs,
        out_specs=row(d),
        out_shape=jax.ShapeDtypeStruct((t, d), F32),
        compiler_params=_params("parallel"),
        name="ffn",
    )(*args)


def _inproj_even_kernel(x_ref, w_ref, rcos_ref, rsin_ref, qd_ref, kd_ref,
                        dcos_ref, dsina_ref, dsinb_ref, o_ref):
    xg, r = _norm_parts(x_ref[...])
    proj = lambda first_block, n_blocks: _mm(xg, w_ref[:, _lane_block(first_block, n_blocks)]) * r
    heads = RET_WIDTH // LANES
    half = ROT_DIM // 2

    rcos, rsin = rcos_ref[...], rsin_ref[...]
    ret_rot = lambda x: x * rcos + pltpu.roll(x, RET_DK // 2, 1) * rsin
    q = proj(0, heads)
    for j in range(heads):
        qj = ret_rot(q[:, _lane_block(j)])
        o_ref[EVEN_RQ + j] = qj.astype(BF16)
        o_ref[EVEN_RQD + j] = (qj * qd_ref[:, _lane_block(j)]).astype(BF16)
    k = proj(heads, heads)
    for j in range(heads):
        kj = ret_rot(k[:, _lane_block(j)]) * (RET_DK ** -0.5)
        o_ref[EVEN_RK + j] = kj.astype(BF16)
        o_ref[EVEN_RKD + j] = (kj * kd_ref[:, _lane_block(j)]).astype(BF16)
    vg = proj(2 * heads, 2 * heads)
    for j in range(2 * heads):
        o_ref[EVEN_RV + j] = vg[:, _lane_block(j)].astype(BF16)

    dcos, dsina, dsinb = dcos_ref[...], dsina_ref[...], dsinb_ref[...]
    dil_rot = lambda x: x * dcos + pltpu.roll(x, LANES - half, 1) * dsina + pltpu.roll(x, half, 1) * dsinb
    pairs = DIL_WIDTH // LANES
    dq = proj(4 * heads, pairs)
    for j in range(pairs):
        o_ref[EVEN_DQ + j] = (dil_rot(dq[:, _lane_block(j)]) * (DIL_DH ** -0.5)).astype(BF16)
    dk = proj(4 * heads + pairs, pairs)
    for j in range(pairs):
        o_ref[EVEN_DK + j] = dil_rot(dk[:, _lane_block(j)]).astype(BF16)
    dv = proj(4 * heads + 2 * pairs, pairs)
    for j in range(pairs):
        o_ref[EVEN_DV + j] = dv[:, _lane_block(j)].astype(BF16)


def _even_tables(seq):
    c = RET_CHUNK
    log_g = np.log(1.0 - 2.0 ** (-5.0 - np.arange(RET_HEADS, dtype=np.float64)))
    i = np.arange(c, dtype=np.float64)
    rel = i[:, None] - i[None, :]
    dmask = np.where(rel >= 0, np.exp(log_g[:, None, None] * np.maximum(rel, 0.0)), 0.0)
    spread = lambda a: np.tile(np.repeat(a.T, RET_DK, axis=1), (ROW_TILE // c, 1))
    q_decay = spread(np.exp(log_g[:, None] * (i + 1.0)))
    k_decay = spread(np.exp(log_g[:, None] * (c - 1 - i)))
    chunk_decay = np.exp(log_g * c)[:, None, None] * np.ones((1, 1, RET_DV))
    pos = np.arange(seq, dtype=np.float32)[:, None]

    freq = np.float32(RET_THETA) ** (-np.linspace(0.0, 1.0, RET_DK // 2, dtype=np.float32))
    ang = pos * freq[None, :]
    rcos = np.concatenate([np.cos(ang), np.cos(ang)], axis=1)
    rsin = np.concatenate([-np.sin(ang), np.sin(ang)], axis=1)

    freq = np.float32(ROPE_THETA) ** (-np.arange(0, ROT_DIM, 2, dtype=np.float32) / np.float32(ROT_DIM))
    ang = pos * freq[None, :]
    cos, sin = np.cos(ang), np.sin(ang)
    zeros = np.zeros_like(sin)
    head = lambda a, b, fill: np.concatenate([a, b, np.full((seq, DIL_DH - ROT_DIM), fill, np.float32)], axis=1)
    two = lambda x: np.concatenate([x, x], axis=1)
    dcos, dsina, dsinb = two(head(cos, cos, 1.0)), two(head(-sin, zeros, 0.0)), two(head(zeros, sin, 0.0))
    f = lambda a: np.asarray(a, np.float32)
    return dict(rcos=f(rcos), rsin=f(rsin), q_decay=f(q_decay), k_decay=f(k_decay), dcos=f(dcos),
                dsina=f(dsina), dsinb=f(dsinb), dmask=f(dmask), chunk_decay=f(chunk_decay))


def _inproj_even(x, weight, tables, seq):
    t, d = x.shape
    tiles_per_seq = seq // ROW_TILE
    by_position = lambda a: (a, pl.BlockSpec((ROW_TILE, LANES), lambda i: (i % tiles_per_seq, 0)))
    operands = [(x, pl.BlockSpec((ROW_TILE, d), lambda i: (i, 0))), weight,
                by_position(tables["rcos"]), by_position(tables["rsin"]),
                _resident(tables["q_decay"]), _resident(tables["k_decay"]),
                by_position(tables["dcos"]), by_position(tables["dsina"]), by_position(tables["dsinb"])]
    return pl.pallas_call(
        _inproj_even_kernel,
        grid=(t // ROW_TILE,),
        in_specs=[spec for _, spec in operands],
        out_specs=pl.BlockSpec((EVEN_SLABS, ROW_TILE, LANES), lambda i: (0, i, 0)),
        out_shape=jax.ShapeDtypeStruct((EVEN_SLABS, t, LANES), BF16),
        compiler_params=_params("parallel"),
        name="inproj_even",
    )(*[a for a, _ in operands])


def _retention_kernel(q_ref, qdec_ref, k_ref, kdec_ref, v_ref, g_ref, dm_ref, cd_ref, o_ref):
    c = RET_CHUNK
    seq = q_ref.shape[0]
    dmask = dm_ref[...]
    chunk_decay = cd_ref[...]

    def body(n, state):
        rows = [_tile_rows(n * CHUNK_GROUP + g, c) for g in range(CHUNK_GROUP)]

        def issue(r):
            v = v_ref[r, :]
            return v, _mm_nt(q_ref[r, :], k_ref[r, :]), _mm_tn(kdec_ref[r, :], v)

        def advance(r, v, s, kv, state):
            lhs = jnp.concatenate([(s * dmask).astype(BF16), qdec_ref[r, :]], axis=1)
            rhs = jnp.concatenate([v, state.astype(BF16)], axis=0)
            o = _head_norm(_mm(lhs, rhs)) * _silu(g_ref[r, :].astype(F32))
            o_ref[r, :] = o.astype(BF16)
            return state * chunk_decay + kv

        issued = []
        for g, r in enumerate(rows):
            issued.append(issue(r))
            if g >= CHUNK_LOOKAHEAD:
                state = advance(rows[g - CHUNK_LOOKAHEAD], *issued[g - CHUNK_LOOKAHEAD], state)
        for g in range(max(len(rows) - CHUNK_LOOKAHEAD, 0), len(rows)):
            state = advance(rows[g], *issued[g], state)
        return state

    lax.fori_loop(0, seq // c // CHUNK_GROUP, body, jnp.zeros((RET_DK, RET_DV), F32))


def _retention(p, tables, batch, seq):
    head_block = lambda first: pl.BlockSpec((None, None, seq, RET_DK), lambda b, j: (first + j, b, 0, 0))
    per_head = lambda a: pl.BlockSpec((None,) + a.shape[1:], lambda b, j: (j, 0, 0))
    dmask, chunk_decay = tables["dmask"], tables["chunk_decay"]
    return pl.pallas_call(
        _retention_kernel,
        grid=(batch, RET_HEADS),
        in_specs=[head_block(EVEN_RQ), head_block(EVEN_RQD), head_block(EVEN_RK), head_block(EVEN_RKD),
                  head_block(EVEN_RV), head_block(EVEN_RG), per_head(dmask), per_head(chunk_decay)],
        out_specs=pl.BlockSpec((None, None, seq, RET_DV), lambda b, j: (j, b, 0, 0)),
        out_shape=jax.ShapeDtypeStruct((RET_HEADS, batch, seq, RET_DV), BF16),
        compiler_params=_params("parallel", "parallel"),
        name="retention",
    )(p, p, p, p, p, p, dmask, chunk_decay)


def _dilated_kernel(q_ref, k_ref, v_ref, o_ref, qs_ref, ks_ref, vs_ref, ob_ref, mb_ref, lb_ref):
    seq = q_ref.shape[0]
    c = DIL_BLOCK

    def stage(i, carry):
        rows = _tile_rows(i, PREP_TILE)
        qs_ref[rows, :] = q_ref[rows, :].astype(F32)
        ks_ref[rows, :] = k_ref[rows, :].astype(F32)
        vs_ref[rows, :] = v_ref[rows, :].astype(F32)
        return carry

    lax.fori_loop(0, seq // PREP_TILE, stage, 0)

    head0 = lax.broadcasted_iota(jnp.int32, (c, LANES), 1) < DIL_DH
    dist = lax.broadcasted_iota(jnp.int32, (c, 2 * c), 0) + c - lax.broadcasted_iota(jnp.int32, (c, 2 * c), 1)
    band = (dist >= 0) & (dist <= c)
    causal = lax.broadcasted_iota(jnp.int32, (c, c), 0) >= lax.broadcasted_iota(jnp.int32, (c, c), 1)
    band_bias = jnp.where(band, 0.0, -jnp.inf)
    causal_bias = jnp.where(causal, 0.0, -jnp.inf)

    def class_rows(r, dil):
        return pl.ds(r, seq // dil, stride=dil) if dil > 1 else pl.ds(0, seq)

    def tile_keys(strip, u):
        return strip[max(u - 1, 0) * c:(u + 1) * c]

    def branch(dil):
        blocks = seq // dil // c
        strips = {}

        def strip(ref, r, prepare):
            if (id(ref), r) not in strips:
                strips[id(ref), r] = prepare(ref[class_rows(r, dil), :])
            return strips[id(ref), r]

        def by_head(x, other):
            first = lax.broadcasted_iota(jnp.int32, x.shape, 1) < DIL_DH
            return jnp.where(first, x, other).astype(BF16), jnp.where(first, other, x).astype(BF16)

        def issue(r, u):
            qu = strip(qs_ref, r, lambda q: q)[u * c:(u + 1) * c]
            ku = tile_keys(strip(ks_ref, r, lambda k: k.astype(BF16)), u)
            bias = band_bias if u > 0 else causal_bias
            return [bias + _mm_nt(qh, ku) for qh in by_head(qu, 0.0)]

        def finish(r, u, scores):
            res, mx = [], []
            for s, vh in zip(scores, strip(vs_ref, r, lambda v: by_head(v, 1.0))):
                m = jnp.max(s, axis=-1, keepdims=True)
                res.append(_mm(jnp.exp(s - m).astype(BF16), tile_keys(vh, u)))
                mx.append(jnp.broadcast_to(m, (c, LANES)))
            start = r + u * c * dil
            rows = pl.ds(start, c, stride=dil) if dil > 1 else pl.ds(start, c)
            return (rows, jnp.where(head0, res[0], res[1]), jnp.where(head0, mx[0], mx[1]),
                    jnp.where(head0, res[1], res[0]))

        tiles = [(r, u) for r in range(dil) for u in range(blocks)]
        issued = []
        for i, tile in enumerate(tiles):
            issued.append(issue(*tile))
            if i >= DIL_LOOKAHEAD:
                yield finish(*tiles[i - DIL_LOOKAHEAD], issued[i - DIL_LOOKAHEAD])
        for i in range(max(len(tiles) - DIL_LOOKAHEAD, 0), len(tiles)):
            yield finish(*tiles[i], issued[i])

    n_br = len(DIL_PATTERNS)
    for br, (window, dil) in enumerate(DIL_PATTERNS):
        assert window // dil == c
        for rows, o, m, l in branch(dil):
            ob_ref[br, rows, :] = o
            mb_ref[br, rows, :] = m
            lb_ref[br, rows, :] = l

    def combine(i, carry):
        rows = _tile_rows(i, COMBINE_TILE)
        ms = [mb_ref[b, rows, :] for b in range(n_br)]
        m = functools.reduce(jnp.maximum, ms)
        es = [jnp.exp(x - m) for x in ms]
        num = sum(es[b] * ob_ref[b, rows, :] for b in range(n_br))
        den = sum(es[b] * pltpu.roll(lb_ref[b, rows, :], DIL_DH, 1) for b in range(n_br))
        o_ref[rows, :] = (num / den).astype(BF16)
        return carry

    lax.fori_loop(0, seq // COMBINE_TILE, combine, 0)


def _dilated(p, batch, seq):
    pairs = DIL_WIDTH // LANES
    pair_block = lambda first: pl.BlockSpec((None, None, seq, LANES), lambda b, j: (first + j, b, 0, 0))
    n_br = len(DIL_PATTERNS)
    return pl.pallas_call(
        _dilated_kernel,
        grid=(batch, pairs),
        in_specs=[pair_block(EVEN_DQ), pair_block(EVEN_DK), pair_block(EVEN_DV)],
        out_specs=pl.BlockSpec((None, None, seq, LANES), lambda b, j: (j, b, 0, 0)),
        out_shape=jax.ShapeDtypeStruct((pairs, batch, seq, LANES), BF16),
        scratch_shapes=[pltpu.VMEM((seq, LANES), F32)] * 3 + [pltpu.VMEM((n_br, seq, LANES), F32)] * 3,
        compiler_params=_params("parallel", "parallel"),
        name="dilated",
    )(p, p, p)


def _inproj_gla_kernel(x_ref, w_ref, wa_ref, ba_ref, qk_ref, vr_ref, dec_ref):
    c = GLA_CHUNK
    per_tile = ROW_TILE // c
    kw, vw = GLA_KW, GLA_VW
    xg, r = _norm_parts(x_ref[...])
    proj = lambda lo, hi: _mm(xg, w_ref[:, lo:hi]) * r
    a_low = proj(2 * kw + 2 * vw, GLA_IN_PADDED).astype(BF16)
    z = _mm(a_low, wa_ref[...]) + ba_ref[...]
    slabs = []
    for j in range(2 * GLA_HEADS):
        slabs.append(proj(2 * kw + j * GLA_DV, 2 * kw + (j + 1) * GLA_DV))
        vr_ref[j] = slabs[j].astype(BF16)
    q = proj(0, kw) * (GLA_DK ** -0.5)
    k = proj(kw, 2 * kw)
    row = lax.broadcasted_iota(jnp.int32, (c, kw), 0)
    for j in range(per_tile):
        rows = slice(j * c, (j + 1) * c)
        anchor = 0.0 * slabs[j % len(slabs)][rows, :LANES]
        zj = z[rows] + jnp.concatenate([anchor] * (kw // LANES), axis=1)
        b = (jnp.minimum(zj, 0.0) - jnp.log1p(jnp.exp(-jnp.abs(zj)))) * (1.0 / GLA_TAU)
        shift = 1
        while shift < c:
            b = b + jnp.where(row >= shift, pltpu.roll(b, shift, 0), 0.0)
            shift *= 2
        b_last = b[c - 1:c, :]
        dec_ref[j] = jnp.exp(b_last)
        operands = (q[rows] * jnp.exp(b), k[rows] * jnp.exp(-b), k[rows] * jnp.exp(b_last - b))
        for i, operand in enumerate(operands):
            for h in range(GLA_HEADS):
                qk_ref[i * GLA_HEADS + h, rows, :] = operand[:, h * GLA_DK:(h + 1) * GLA_DK].astype(BF16)


def _inproj_gla(x, weight, wa, ba):
    t, d = x.shape
    per_tile = ROW_TILE // GLA_CHUNK
    operands = [(x, pl.BlockSpec((ROW_TILE, d), lambda i: (i, 0))), weight, wa, ba]
    return pl.pallas_call(
        _inproj_gla_kernel,
        grid=(t // ROW_TILE,),
        in_specs=[spec for _, spec in operands],
        out_specs=[pl.BlockSpec((3 * GLA_HEADS, ROW_TILE, GLA_DK), lambda i: (0, i, 0)),
                   pl.BlockSpec((2 * GLA_HEADS, ROW_TILE, GLA_DV), lambda i: (0, i, 0)),
                   pl.BlockSpec((per_tile, 1, GLA_KW), lambda i: (i, 0, 0))],
        out_shape=[jax.ShapeDtypeStruct((3 * GLA_HEADS, t, GLA_DK), BF16),
                   jax.ShapeDtypeStruct((2 * GLA_HEADS, t, GLA_DV), BF16),
                   jax.ShapeDtypeStruct((t // GLA_CHUNK, 1, GLA_KW), F32)],
        compiler_params=_params("parallel"),
        name="inproj_gla",
    )(*[a for a, _ in operands])


def _gla_kernel(qt_ref, kt_ref, kl_ref, v_ref, r_ref, dec_ref, o_ref):
    c = GLA_CHUNK
    seq = qt_ref.shape[0]
    causal = lax.broadcasted_iota(jnp.int32, (c, c), 0) >= lax.broadcasted_iota(jnp.int32, (c, c), 1)

    def body(n, state):
        chunks = [n * CHUNK_GROUP + g for g in range(CHUNK_GROUP)]

        def issue(j):
            r = _tile_rows(j, c)
            v, q_t = v_ref[r, :], qt_ref[r, :]
            return r, v, q_t, _mm_nt(q_t, kt_ref[r, :]), _mm_tn(v, kl_ref[r, :])

        def advance(j, r, v, q_t, s, kv, state):
            o = _mm(jnp.where(causal, s, 0.0).astype(BF16), v) + _mm_nt(q_t, state.astype(BF16))
            o = _head_norm(o) * _silu(r_ref[r, :].astype(F32))
            o_ref[r, :] = o.astype(BF16)
            return state * dec_ref[j] + kv

        issued = []
        for g, j in enumerate(chunks):
            issued.append(issue(j))
            if g >= CHUNK_LOOKAHEAD:
                state = advance(chunks[g - CHUNK_LOOKAHEAD], *issued[g - CHUNK_LOOKAHEAD], state)
        for g in range(max(len(chunks) - CHUNK_LOOKAHEAD, 0), len(chunks)):
            state = advance(chunks[g], *issued[g], state)
        return state

    lax.fori_loop(0, seq // c // CHUNK_GROUP, body, jnp.zeros((GLA_DV, GLA_DK), F32))


def _gla(qk, vr, dec, batch, seq):
    h = GLA_HEADS
    slab = lambda first, width: pl.BlockSpec((None, None, seq, width), lambda b, j: (first + j, b, 0, 0))
    return pl.pallas_call(
        _gla_kernel,
        grid=(batch, h),
        in_specs=[slab(0, GLA_DK), slab(h, GLA_DK), slab(2 * h, GLA_DK), slab(0, GLA_DV), slab(h, GLA_DV),
                  pl.BlockSpec((None, seq // GLA_CHUNK, 1, GLA_DK), lambda b, j: (b, 0, 0, j))],
        out_specs=slab(0, GLA_DV),
        out_shape=jax.ShapeDtypeStruct((h, batch, seq, GLA_DV), BF16),
        compiler_params=_params("parallel", "parallel"),
        name="gla",
    )(qk, qk, qk, vr, vr, dec)


def kernel(x, ffn_pre_norm, ffn_pre_w_gate, ffn_pre_w_up, ffn_pre_w_down, mix_norm, ab_w_in, ab_w_out,
           gla_w_in, gla_w_a2, gla_b_a, gla_w_out, ffn_post_norm, ffn_post_w_gate, ffn_post_w_up,
           ffn_post_w_down, final_norm):
    batch, seq, d = x.shape
    depth = ffn_pre_norm.shape[0]
    bf = lambda w: w.astype(BF16)
    gained = lambda w, g: bf(w * g[:, :, None])
    assert RET_HEADS * RET_DV == DIL_WIDTH

    pre = [gained(ffn_pre_w_gate, ffn_pre_norm), gained(ffn_pre_w_up, ffn_pre_norm), bf(ffn_pre_w_down)]
    post = [gained(ffn_post_w_gate, ffn_post_norm), gained(ffn_post_w_up, ffn_post_norm), bf(ffn_post_w_down)]
    ab_in, ab_out = gained(ab_w_in, mix_norm[0::2]), bf(ab_w_out)
    gla_in = jnp.pad(gained(gla_w_in, mix_norm[1::2]), ((0, 0), (0, 0), (0, GLA_IN_PADDED - gla_w_in.shape[2])))
    gla_a2 = jnp.pad(bf(gla_w_a2), ((0, 0), (0, LANES - gla_w_a2.shape[1]), (0, 0)))
    gla_ba = gla_b_a.reshape(gla_b_a.shape[0], 1, -1)
    gla_out = bf(gla_w_out)
    tables = _even_tables(seq)

    xt = x.reshape(batch * seq, d)
    for l in range(depth):
        xt = _ffn(xt, [], [_layer(w, l) for w in pre])
        if l % 2 == 0:
            p = _inproj_even(xt, _layer(ab_in, l // 2), tables, seq)
            p = p.reshape(EVEN_SLABS, batch, seq, LANES)
            o_r = _retention(p, tables, batch, seq).reshape(RET_HEADS, batch * seq, RET_DV)
            o_d = _dilated(p, batch, seq).reshape(DIL_WIDTH // LANES, batch * seq, LANES)
            mixes = [(o_r, _layer(ab_out, l // 2, rows=DIL_WIDTH, row_block=0)),
                     (o_d, _layer(ab_out, l // 2, rows=DIL_WIDTH, row_block=1))]
        else:
            qk, vr, dec = _inproj_gla(xt, _layer(gla_in, l // 2), _layer(gla_a2, l // 2), _layer(gla_ba, l // 2))
            o = _gla(qk.reshape(3 * GLA_HEADS, batch, seq, GLA_DK), vr.reshape(2 * GLA_HEADS, batch, seq, GLA_DV),
                     dec.reshape(batch, seq // GLA_CHUNK, 1, GLA_KW), batch, seq)
            mixes = [(o.reshape(GLA_HEADS, batch * seq, GLA_DV), _layer(gla_out, l // 2))]
        xt = _ffn(xt, mixes, [_layer(w, l) for w in post],
                  final_g=final_norm.reshape(1, d) if l == depth - 1 else None)
    return xt.reshape(batch, seq, d)
```

```python
import functools

import jax
import jax.numpy as jnp
import numpy as np
from jax import lax
from jax.experimental import pallas as pl
from jax.experimental.pallas import tpu as pltpu

F32 = jnp.float32
BF16 = jnp.bfloat16

LANES = 128
VMEM_LIMIT_BYTES = 56 * 1024 * 1024

D_MODEL = 1024
FFN_HIDDEN = 2816
EPS = 1e-6

RET_HEADS = 4
RET_DK = 128
RET_DV = 128
RET_CHUNK = 128
RET_THETA = 10000.0
RET_WIDTH = RET_HEADS * RET_DK

DIL_HEADS = 8
DIL_DH = 64
DIL_PATTERNS = ((128, 1), (512, 4), (2048, 16))
DIL_BLOCK = 128
DIL_LOOKAHEAD = 3
ROPE_THETA = 500000.0
ROT_DIM = DIL_DH // 4
DIL_WIDTH = DIL_HEADS * DIL_DH

GLA_HEADS = 4
GLA_DK = (D_MODEL // 2) // GLA_HEADS
GLA_DV = D_MODEL // GLA_HEADS
GLA_LOWRANK = 16
GLA_TAU = 16.0
GLA_CHUNK = 64
GLA_KW = GLA_HEADS * GLA_DK
GLA_VW = GLA_HEADS * GLA_DV

CHUNK_GROUP = 32
CHUNK_LOOKAHEAD = 16

EVEN_RQ, EVEN_RQD, EVEN_RK, EVEN_RKD, EVEN_RV, EVEN_RG, EVEN_DQ, EVEN_DK, EVEN_DV = range(0, 36, 4)
EVEN_SLABS = 9 * RET_WIDTH // LANES
GLA_IN_PADDED = 2 * GLA_KW + 2 * GLA_VW + LANES

ROW_TILE = 512
PREP_TILE = 512
COMBINE_TILE = 1024

NT_DIMS = (((1,), (1,)), ((), ()))
TN_DIMS = (((0,), (0,)), ((), ()))


def _params(*semantics):
    return pltpu.CompilerParams(dimension_semantics=semantics,
                                vmem_limit_bytes=VMEM_LIMIT_BYTES)


def _resident(a):
    zeros = (0,) * a.ndim
    return a, pl.BlockSpec(a.shape, lambda *_: zeros, pipeline_mode=pl.Buffered(1))


def _layer(w, l, rows=None, row_block=0):
    shape = (None, w.shape[1] if rows is None else rows, w.shape[2])
    return w, pl.BlockSpec(shape, lambda *_: (l, row_block, 0), pipeline_mode=pl.Buffered(1))


def _rmsnorm(x, g):
    return x * lax.rsqrt(jnp.mean(x * x, axis=-1, keepdims=True) + EPS) * g


def _norm_parts(x):
    return x.astype(BF16), lax.rsqrt(jnp.mean(x * x, axis=-1, keepdims=True) + EPS)


def _head_norm(o):
    mu = jnp.mean(o, axis=-1, keepdims=True)
    c = o - mu
    return c * lax.rsqrt(jnp.mean(c * c, axis=-1, keepdims=True) + EPS)


def _silu(x):
    return x * (0.5 * jnp.tanh(0.5 * x) + 0.5)


def _mm(a, b):
    return jnp.dot(a, b, preferred_element_type=F32)


def _mm_nt(a, b):
    return lax.dot_general(a, b, NT_DIMS, preferred_element_type=F32)


def _mm_tn(a, b):
    return lax.dot_general(a, b, TN_DIMS, preferred_element_type=F32)


def _tile_rows(i, n):
    return pl.ds(pl.multiple_of(i * n, n), n)


def _lane_block(j, n=1):
    return slice(j * LANES, (j + n) * LANES)


def _ffn_kernel(*refs, n_mix, final):
    x_ref = refs[0]
    mix_refs = refs[1:1 + 2 * n_mix]
    wg_ref, wu_ref, wd_ref = refs[1 + 2 * n_mix:4 + 2 * n_mix]
    o_ref = refs[-1]
    x = x_ref[...]
    for i in range(n_mix):
        mix_ref = mix_refs[2 * i]
        mix = jnp.concatenate([mix_ref[g] for g in range(mix_ref.shape[0])], axis=1)
        x = x + _mm(mix, mix_refs[2 * i + 1][...])
    xg, r = _norm_parts(x)
    half_r = 0.5 * r
    h = _mm(xg, wg_ref[...]) * half_r
    act = (h * (jnp.tanh(h) + 1.0) * _mm(xg, wu_ref[...])).astype(BF16)
    y = x + half_r * _mm(act, wd_ref[...])
    if final:
        y = _rmsnorm(y, refs[-2][...])
    o_ref[...] = y


def _ffn(x, mixes, weights, final_g=None):
    t, d = x.shape
    row = lambda w: pl.BlockSpec((ROW_TILE, w), lambda i: (i, 0))
    args, specs = [x], [row(d)]
    for m, (w, spec) in mixes:
        args += [m, w]
        specs += [pl.BlockSpec((m.shape[0], ROW_TILE, m.shape[2]), lambda i: (0, i, 0)), spec]
    for w, spec in weights + ([_resident(final_g)] if final_g is not None else []):
        args.append(w)
        specs.append(spec)
    return pl.pallas_call(
        functools.partial(_ffn_kernel, n_mix=len(mixes), final=final_g is not None),
        grid=(t // ROW_TILE,),
        in_specs=specs,
        out_specs=row(d),
        out_shape=jax.ShapeDtypeStruct((t, d), F32),
        compiler_params=_params("parallel"),
        name="ffn",
    )(*args)


def _inproj_even_kernel(x_ref, w_ref, rcos_ref, rsin_ref, qd_ref, kd_ref,
                        dcos_ref, dsina_ref, dsinb_ref, o_ref):
    xg, r = _norm_parts(x_ref[...])
    proj = lambda first_block, n_blocks: _mm(xg, w_ref[:, _lane_block(first_block, n_blocks)]) * r
    heads = RET_WIDTH // LANES
    half = ROT_DIM // 2

    rcos, rsin = rcos_ref[...], rsin_ref[...]
    ret_rot = lambda x: x * rcos + pltpu.roll(x, RET_DK // 2, 1) * rsin
    q = proj(0, heads)
    for j in range(heads):
        qj = ret_rot(q[:, _lane_block(j)])
        o_ref[EVEN_RQ + j] = qj.astype(BF16)
        o_ref[EVEN_RQD + j] = (qj * qd_ref[:, _lane_block(j)]).astype(BF16)
    k = proj(heads, heads)
    for j in range(heads):
        kj = ret_rot(k[:, _lane_block(j)])
        o_ref[EVEN_RK + j] = kj.astype(BF16)
        o_ref[EVEN_RKD + j] = (kj * kd_ref[:, _lane_block(j)]).astype(BF16)
    vg = proj(2 * heads, 2 * heads)
    for j in range(2 * heads):
        blk = vg[:, _lane_block(j)]
        o_ref[EVEN_RV + j] = (blk if j < heads else _silu(blk)).astype(BF16)

    dcos, dsina, dsinb = dcos_ref[...], dsina_ref[...], dsinb_ref[...]
    dil_rot = lambda x: x * dcos + pltpu.roll(x, LANES - half, 1) * dsina + pltpu.roll(x, half, 1) * dsinb
    pairs = DIL_WIDTH // LANES
    dq = proj(4 * heads, pairs)
    for j in range(pairs):
        o_ref[EVEN_DQ + j] = dil_rot(dq[:, _lane_block(j)]).astype(BF16)
    dk = proj(4 * heads + pairs, pairs)
    for j in range(pairs):
        o_ref[EVEN_DK + j] = dil_rot(dk[:, _lane_block(j)]).astype(BF16)
    dv = proj(4 * heads + 2 * pairs, pairs)
    for j in range(pairs):
        o_ref[EVEN_DV + j] = dv[:, _lane_block(j)].astype(BF16)


def _even_tables(seq):
    c = RET_CHUNK
    log_g = np.log(1.0 - 2.0 ** (-5.0 - np.arange(RET_HEADS, dtype=np.float64)))
    i = np.arange(c, dtype=np.float64)
    rel = i[:, None] - i[None, :]
    dmask = np.where(rel >= 0, np.exp(log_g[:, None, None] * np.maximum(rel, 0.0)), 0.0)
    spread = lambda a: np.tile(np.repeat(a.T, RET_DK, axis=1), (ROW_TILE // c, 1))
    q_decay = spread(np.exp(log_g[:, None] * (i + 1.0)))
    k_decay = spread(np.exp(log_g[:, None] * (c - 1 - i)))
    chunk_decay = np.exp(log_g * c)[:, None, None] * np.ones((1, 1, RET_DV))
    pos = np.arange(seq, dtype=np.float32)[:, None]

    freq = np.float32(RET_THETA) ** (-np.linspace(0.0, 1.0, RET_DK // 2, dtype=np.float32))
    ang = pos * freq[None, :]
    rcos = np.concatenate([np.cos(ang), np.cos(ang)], axis=1)
    rsin = np.concatenate([-np.sin(ang), np.sin(ang)], axis=1)

    freq = np.float32(ROPE_THETA) ** (-np.arange(0, ROT_DIM, 2, dtype=np.float32) / np.float32(ROT_DIM))
    ang = pos * freq[None, :]
    cos, sin = np.cos(ang), np.sin(ang)
    zeros = np.zeros_like(sin)
    head = lambda a, b, fill: np.concatenate([a, b, np.full((seq, DIL_DH - ROT_DIM), fill, np.float32)], axis=1)
    two = lambda x: np.concatenate([x, x], axis=1)
    dcos, dsina, dsinb = two(head(cos, cos, 1.0)), two(head(-sin, zeros, 0.0)), two(head(zeros, sin, 0.0))
    f = lambda a: np.asarray(a, np.float32)
    return dict(rcos=f(rcos), rsin=f(rsin), q_decay=f(q_decay), k_decay=f(k_decay), dcos=f(dcos),
                dsina=f(dsina), dsinb=f(dsinb), dmask=f(dmask), chunk_decay=f(chunk_decay))


def _inproj_even(x, weight, tables, seq):
    t, d = x.shape
    tiles_per_seq = seq // ROW_TILE
    by_position = lambda a: (a, pl.BlockSpec((ROW_TILE, LANES), lambda i: (i % tiles_per_seq, 0)))
    operands = [(x, pl.BlockSpec((ROW_TILE, d), lambda i: (i, 0))), weight,
                by_position(tables["rcos"]), by_position(tables["rsin"]),
                _resident(tables["q_decay"]), _resident(tables["k_decay"]),
                by_position(tables["dcos"]), by_position(tables["dsina"]), by_position(tables["dsinb"])]
    return pl.pallas_call(
        _inproj_even_kernel,
        grid=(t // ROW_TILE,),
        in_specs=[spec for _, spec in operands],
        out_specs=pl.BlockSpec((EVEN_SLABS, ROW_TILE, LANES), lambda i: (0, i, 0)),
        out_shape=jax.ShapeDtypeStruct((EVEN_SLABS, t, LANES), BF16),
        compiler_params=_params("parallel"),
        name="inproj_even",
    )(*[a for a, _ in operands])


def _retention_kernel(q_ref, qdec_ref, k_ref, kdec_ref, v_ref, g_ref, dm_ref, cd_ref, o_ref):
    c = RET_CHUNK
    seq = q_ref.shape[0]
    dmask = dm_ref[...]
    chunk_decay = cd_ref[...]

    def body(n, state):
        rows = [_tile_rows(n * CHUNK_GROUP + g, c) for g in range(CHUNK_GROUP)]

        def issue(r):
            v = v_ref[r, :]
            return v, _mm_nt(q_ref[r, :], k_ref[r, :]), _mm_tn(kdec_ref[r, :], v)

        def advance(r, v, s, kv, state):
            lhs = jnp.concatenate([(s * dmask).astype(BF16), qdec_ref[r, :]], axis=1)
            rhs = jnp.concatenate([v, state.astype(BF16)], axis=0)
            o = _head_norm(_mm(lhs, rhs)) * g_ref[r, :].astype(F32)
            o_ref[r, :] = o.astype(BF16)
            return state * chunk_decay + kv

        issued = []
        for g, r in enumerate(rows):
            issued.append(issue(r))
            if g >= CHUNK_LOOKAHEAD:
                state = advance(rows[g - CHUNK_LOOKAHEAD], *issued[g - CHUNK_LOOKAHEAD], state)
        for g in range(max(len(rows) - CHUNK_LOOKAHEAD, 0), len(rows)):
            state = advance(rows[g], *issued[g], state)
        return state

    lax.fori_loop(0, seq // c // CHUNK_GROUP, body, jnp.zeros((RET_DK, RET_DV), F32))


def _retention(p, tables, batch, seq):
    head_block = lambda first: pl.BlockSpec((None, None, seq, RET_DK), lambda b, j: (first + j, b, 0, 0))
    per_head = lambda a: pl.BlockSpec((None,) + a.shape[1:], lambda b, j: (j, 0, 0))
    dmask, chunk_decay = tables["dmask"], tables["chunk_decay"]
    return pl.pallas_call(
        _retention_kernel,
        grid=(batch, RET_HEADS),
        in_specs=[head_block(EVEN_RQ), head_block(EVEN_RQD), head_block(EVEN_RK), head_block(EVEN_RKD),
                  head_block(EVEN_RV), head_block(EVEN_RG), per_head(dmask), per_head(chunk_decay)],
        out_specs=pl.BlockSpec((None, None, seq, RET_DV), lambda b, j: (j, b, 0, 0)),
        out_shape=jax.ShapeDtypeStruct((RET_HEADS, batch, seq, RET_DV), BF16),
        compiler_params=_params("parallel", "parallel"),
        name="retention",
    )(p, p, p, p, p, p, dmask, chunk_decay)


def _dilated_kernel(q_ref, k_ref, v_ref, o_ref, qs_ref, ks_ref, vs_ref, ob_ref, mb_ref, lb_ref):
    seq = q_ref.shape[0]
    c = DIL_BLOCK

    def stage(i, carry):
        rows = _tile_rows(i, PREP_TILE)
        qs_ref[rows, :] = q_ref[rows, :].astype(F32)
        ks_ref[rows, :] = k_ref[rows, :].astype(F32)
        vs_ref[rows, :] = v_ref[rows, :].astype(F32)
        return carry

    lax.fori_loop(0, seq // PREP_TILE, stage, 0)

    head0 = lax.broadcasted_iota(jnp.int32, (c, LANES), 1) < DIL_DH
    dist = lax.broadcasted_iota(jnp.int32, (c, 2 * c), 0) + c - lax.broadcasted_iota(jnp.int32, (c, 2 * c), 1)
    band = (dist >= 0) & (dist <= c)
    causal = lax.broadcasted_iota(jnp.int32, (c, c), 0) >= lax.broadcasted_iota(jnp.int32, (c, c), 1)
    band_bias = jnp.where(band, 0.0, -jnp.inf)
    causal_bias = jnp.where(causal, 0.0, -jnp.inf)

    def class_rows(r, dil):
        return pl.ds(r, seq // dil, stride=dil) if dil > 1 else pl.ds(0, seq)

    def tile_keys(strip, u):
        return strip[max(u - 1, 0) * c:(u + 1) * c]

    def branch(dil):
        blocks = seq // dil // c
        strips = {}

        def strip(ref, r, prepare):
            if (id(ref), r) not in strips:
                strips[id(ref), r] = prepare(ref[class_rows(r, dil), :])
            return strips[id(ref), r]

        def by_head(x, other):
            first = lax.broadcasted_iota(jnp.int32, x.shape, 1) < DIL_DH
            return jnp.where(first, x, other).astype(BF16), jnp.where(first, other, x).astype(BF16)

        def issue(r, u):
            qu = strip(qs_ref, r, lambda q: q)[u * c:(u + 1) * c]
            ku = tile_keys(strip(ks_ref, r, lambda k: k.astype(BF16)), u)
            bias = band_bias if u > 0 else causal_bias
            return [bias + _mm_nt(qh, ku) for qh in by_head(qu, 0.0)]

        def finish(r, u, scores):
            res, mx = [], []
            for s, vh in zip(scores, strip(vs_ref, r, lambda v: by_head(v, 1.0))):
                m = jnp.max(s, axis=-1, keepdims=True)
                res.append(_mm(jnp.exp(s - m).astype(BF16), tile_keys(vh, u)))
                mx.append(jnp.broadcast_to(m, (c, LANES)))
            start = r + u * c * dil
            rows = pl.ds(start, c, stride=dil) if dil > 1 else pl.ds(start, c)
            return (rows, jnp.where(head0, res[0], res[1]), jnp.where(head0, mx[0], mx[1]),
                    jnp.where(head0, res[1], res[0]))

        tiles = [(r, u) for r in range(dil) for u in range(blocks)]
        issued = []
        for i, tile in enumerate(tiles):
            issued.append(issue(*tile))
            if i >= DIL_LOOKAHEAD:
                yield finish(*tiles[i - DIL_LOOKAHEAD], issued[i - DIL_LOOKAHEAD])
        for i in range(max(len(tiles) - DIL_LOOKAHEAD, 0), len(tiles)):
            yield finish(*tiles[i], issued[i])

    n_br = len(DIL_PATTERNS)
    for br, (window, dil) in enumerate(DIL_PATTERNS):
        assert window // dil == c
        for rows, o, m, l in branch(dil):
            ob_ref[br, rows, :] = o
            mb_ref[br, rows, :] = m
            lb_ref[br, rows, :] = l

    def combine(i, carry):
        rows = _tile_rows(i, COMBINE_TILE)
        ms = [mb_ref[b, rows, :] for b in range(n_br)]
        m = functools.reduce(jnp.maximum, ms)
        es = [jnp.exp(x - m) for x in ms]
        num = sum(es[b] * ob_ref[b, rows, :] for b in range(n_br))
        den = sum(es[b] * pltpu.roll(lb_ref[b, rows, :], DIL_DH, 1) for b in range(n_br))
        o_ref[rows, :] = (num / den).astype(BF16)
        return carry

    lax.fori_loop(0, seq // COMBINE_TILE, combine, 0)


def _dilated(p, batch, seq):
    pairs = DIL_WIDTH // LANES
    pair_block = lambda first: pl.BlockSpec((None, None, seq, LANES), lambda b, j: (first + j, b, 0, 0))
    n_br = len(DIL_PATTERNS)
    return pl.pallas_call(
        _dilated_kernel,
        grid=(batch, pairs),
        in_specs=[pair_block(EVEN_DQ), pair_block(EVEN_DK), pair_block(EVEN_DV)],
        out_specs=pl.BlockSpec((None, None, seq, LANES), lambda b, j: (j, b, 0, 0)),
        out_shape=jax.ShapeDtypeStruct((pairs, batch, seq, LANES), BF16),
        scratch_shapes=[pltpu.VMEM((seq, LANES), F32)] * 3 + [pltpu.VMEM((n_br, seq, LANES), F32)] * 3,
        compiler_params=_params("parallel", "parallel"),
        name="dilated",
    )(p, p, p)


def _inproj_gla_kernel(x_ref, w_ref, wa_ref, ba_ref, qk_ref, vr_ref, dec_ref):
    c = GLA_CHUNK
    per_tile = ROW_TILE // c
    kw, vw = GLA_KW, GLA_VW
    xg, r = _norm_parts(x_ref[...])
    proj = lambda lo, hi: _mm(xg, w_ref[:, lo:hi]) * r
    a_low = proj(2 * kw + 2 * vw, GLA_IN_PADDED).astype(BF16)
    z = _mm(a_low, wa_ref[...]) + ba_ref[...]
    slabs = []
    for j in range(2 * GLA_HEADS):
        slabs.append(proj(2 * kw + j * GLA_DV, 2 * kw + (j + 1) * GLA_DV))
        vr_ref[j] = slabs[j].astype(BF16)
    q = proj(0, kw)
    k = proj(kw, 2 * kw)
    row = lax.broadcasted_iota(jnp.int32, (c, kw), 0)
    for j in range(per_tile):
        rows = slice(j * c, (j + 1) * c)
        anchor = 0.0 * slabs[j % len(slabs)][rows, :LANES]
        zj = z[rows] + jnp.concatenate([anchor] * (kw // LANES), axis=1)
        b = (jnp.minimum(zj, 0.0) - jnp.log1p(jnp.exp(-jnp.abs(zj)))) * (1.0 / GLA_TAU)
        shift = 1
        while shift < c:
            b = b + jnp.where(row >= shift, pltpu.roll(b, shift, 0), 0.0)
            shift *= 2
        b_last = b[c - 1:c, :]
        dec_ref[j] = jnp.exp(b_last)
        operands = (q[rows] * jnp.exp(b), k[rows] * jnp.exp(-b), k[rows] * jnp.exp(b_last - b))
        for i, operand in enumerate(operands):
            for h in range(GLA_HEADS):
                qk_ref[i * GLA_HEADS + h, rows, :] = operand[:, h * GLA_DK:(h + 1) * GLA_DK].astype(BF16)


def _inproj_gla(x, weight, wa, ba):
    t, d = x.shape
    per_tile = ROW_TILE // GLA_CHUNK
    operands = [(x, pl.BlockSpec((ROW_TILE, d), lambda i: (i, 0))), weight, wa, ba]
    return pl.pallas_call(
        _inproj_gla_kernel,
        grid=(t // ROW_TILE,),
        in_specs=[spec for _, spec in operands],
        out_specs=[pl.BlockSpec((3 * GLA_HEADS, ROW_TILE, GLA_DK), lambda i: (0, i, 0)),
                   pl.BlockSpec((2 * GLA_HEADS, ROW_TILE, GLA_DV), lambda i: (0, i, 0)),
                   pl.BlockSpec((per_tile, 1, GLA_KW), lambda i: (i, 0, 0))],
        out_shape=[jax.ShapeDtypeStruct((3 * GLA_HEADS, t, GLA_DK), BF16),
                   jax.ShapeDtypeStruct((2 * GLA_HEADS, t, GLA_DV), BF16),
                   jax.ShapeDtypeStruct((t // GLA_CHUNK, 1, GLA_KW), F32)],
        compiler_params=_params("parallel"),
        name="inproj_gla",
    )(*[a for a, _ in operands])


def _gla_kernel(qt_ref, kt_ref, kl_ref, v_ref, r_ref, dec_ref, o_ref):
    c = GLA_CHUNK
    seq = qt_ref.shape[0]
    causal = lax.broadcasted_iota(jnp.int32, (c, c), 0) >= lax.broadcasted_iota(jnp.int32, (c, c), 1)

    def body(n, state):
        chunks = [n * CHUNK_GROUP + g for g in range(CHUNK_GROUP)]

        def issue(j):
            r = _tile_rows(j, c)
            v, q_t = v_ref[r, :], qt_ref[r, :]
            return r, v, q_t, _mm_nt(q_t, kt_ref[r, :]), _mm_tn(v, kl_ref[r, :])

        def advance(j, r, v, q_t, s, kv, state):
            o = _mm(jnp.where(causal, s, 0.0).astype(BF16), v) + _mm_nt(q_t, state.astype(BF16))
            o = _head_norm(o) * _silu(r_ref[r, :].astype(F32))
            o_ref[r, :] = o.astype(BF16)
            return state * dec_ref[j] + kv

        issued = []
        for g, j in enumerate(chunks):
            issued.append(issue(j))
            if g >= CHUNK_LOOKAHEAD:
                state = advance(chunks[g - CHUNK_LOOKAHEAD], *issued[g - CHUNK_LOOKAHEAD], state)
        for g in range(max(len(chunks) - CHUNK_LOOKAHEAD, 0), len(chunks)):
            state = advance(chunks[g], *issued[g], state)
        return state

    lax.fori_loop(0, seq // c // CHUNK_GROUP, body, jnp.zeros((GLA_DV, GLA_DK), F32))


def _gla(qk, vr, dec, batch, seq):
    h = GLA_HEADS
    slab = lambda first, width: pl.BlockSpec((None, None, seq, width), lambda b, j: (first + j, b, 0, 0))
    return pl.pallas_call(
        _gla_kernel,
        grid=(batch, h),
        in_specs=[slab(0, GLA_DK), slab(h, GLA_DK), slab(2 * h, GLA_DK), slab(0, GLA_DV), slab(h, GLA_DV),
                  pl.BlockSpec((None, seq // GLA_CHUNK, 1, GLA_DK), lambda b, j: (b, 0, 0, j))],
        out_specs=slab(0, GLA_DV),
        out_shape=jax.ShapeDtypeStruct((h, batch, seq, GLA_DV), BF16),
        compiler_params=_params("parallel", "parallel"),
        name="gla",
    )(qk, qk, qk, vr, vr, dec)


def kernel(x, ffn_pre_norm, ffn_pre_w_gate, ffn_pre_w_up, ffn_pre_w_down, mix_norm, ab_w_in, ab_w_out,
           gla_w_in, gla_w_a2, gla_b_a, gla_w_out, ffn_post_norm, ffn_post_w_gate, ffn_post_w_up,
           ffn_post_w_down, final_norm):
    batch, seq, d = x.shape
    depth = ffn_pre_norm.shape[0]
    bf = lambda w: w.astype(BF16)
    gained = lambda w, g: bf(w * g[:, :, None])
    assert RET_HEADS * RET_DV == DIL_WIDTH

    pre = [gained(ffn_pre_w_gate, ffn_pre_norm), gained(ffn_pre_w_up, ffn_pre_norm), bf(ffn_pre_w_down)]
    post = [gained(ffn_post_w_gate, ffn_post_norm), gained(ffn_post_w_up, ffn_post_norm), bf(ffn_post_w_down)]
    even_cols = np.ones(ab_w_in.shape[2], np.float32)
    even_cols[RET_WIDTH:2 * RET_WIDTH] = RET_DK ** -0.5
    even_cols[4 * RET_WIDTH:4 * RET_WIDTH + DIL_WIDTH] = DIL_DH ** -0.5
    gla_cols = np.ones(gla_w_in.shape[2], np.float32)
    gla_cols[:GLA_KW] = GLA_DK ** -0.5
    ab_in, ab_out = gained(ab_w_in * even_cols, mix_norm[0::2]), bf(ab_w_out)
    gla_in = jnp.pad(gained(gla_w_in * gla_cols, mix_norm[1::2]),
                     ((0, 0), (0, 0), (0, GLA_IN_PADDED - gla_w_in.shape[2])))
    gla_a2 = jnp.pad(bf(gla_w_a2), ((0, 0), (0, LANES - gla_w_a2.shape[1]), (0, 0)))
    gla_ba = gla_b_a.reshape(gla_b_a.shape[0], 1, -1)
    gla_out = bf(gla_w_out)
    tables = _even_tables(seq)

    xt = x.reshape(batch * seq, d)
    for l in range(depth):
        xt = _ffn(xt, [], [_layer(w, l) for w in pre])
        if l % 2 == 0:
            p = _inproj_even(xt, _layer(ab_in, l // 2), tables, seq)
            p = p.reshape(EVEN_SLABS, batch, seq, LANES)
            o_r = _retention(p, tables, batch, seq).reshape(RET_HEADS, batch * seq, RET_DV)
            o_d = _dilated(p, batch, seq).reshape(DIL_WIDTH // LANES, batch * seq, LANES)
            mixes = [(o_r, _layer(ab_out, l // 2, rows=DIL_WIDTH, row_block=0)),
                     (o_d, _layer(ab_out, l // 2, rows=DIL_WIDTH, row_block=1))]
        else:
            qk, vr, dec = _inproj_gla(xt, _layer(gla_in, l // 2), _layer(gla_a2, l // 2), _layer(gla_ba, l // 2))
            o = _gla(qk.reshape(3 * GLA_HEADS, batch, seq, GLA_DK), vr.reshape(2 * GLA_HEADS, batch, seq, GLA_DV),
                     dec.reshape(batch, seq // GLA_CHUNK, 1, GLA_KW), batch, seq)
            mixes = [(o.reshape(GLA_HEADS, batch * seq, GLA_DV), _layer(gla_out, l // 2))]
        xt = _ffn(xt, mixes, [_layer(w, l) for w in post],
                  final_g=final_norm.reshape(1, d) if l == depth - 1 else None)
    return xt.reshape(batch, seq, d)
```
